```python
import jax, jax.numpy as jnp
from jax import lax
import numpy as np

D_MODEL = 1024
BATCH = 32
SEQ = 256
DEPTH = 4
DEC_BATCH = 2
DEC_SEQ = 1024
PAST_LEN = 512

GRID_W = 64
HEAD_DIM = 64
D_MIX = D_MODEL
N_GROUPS = 4
W_GROUP = D_MIX // N_GROUPS
H_GROUP = W_GROUP // HEAD_DIM
N_IN_PARTS = 12
NA_WR = 8
NA_WC = 16
Q_BLOCK = 128
RET_CHUNK = 64
ROPE_BASE = 10000.0
SC_WIDTH = 3
CF_WIDTH = 31
PEER_HEADS = 8
PEER_NKEYS = 128
PEER_DK = 256
PEER_TOPK = 16
PEER_BLOCK = 128
N_EXPERTS = PEER_NKEYS * PEER_NKEYS
LN_EPS = 1e-5
ALPHA = (2.0 * DEPTH) ** 0.25
BETA = (8.0 * DEPTH) ** -0.25

kernel_name = 'hybrid_natten_retnet_conv_peer_diffusion_step'


def _layernorm(x, g, b):
    xf = x.astype(jnp.float32)
    mu = jnp.mean(xf, -1, keepdims=True)
    var = jnp.mean(jnp.square(xf - mu), -1, keepdims=True)
    return ((xf - mu) * lax.rsqrt(var + LN_EPS) * g + b).astype(x.dtype)


def _modulation(cond, w_mod, b_mod):
    m = (jax.nn.silu(cond) @ w_mod + b_mod)[..., None, :]
    return jnp.split(m, 6, axis=-1)


def _axial_rope(T):
    t = jnp.arange(T)
    row = (t // GRID_W).astype(jnp.float32)
    col = (t % GRID_W).astype(jnp.float32)
    n_freq = HEAD_DIM // 4
    inv = ROPE_BASE ** (-jnp.arange(n_freq, dtype=jnp.float32) / n_freq)
    ang = jnp.concatenate([row[:, None] * inv, col[:, None] * inv], -1)
    return jnp.cos(ang)[None, :, None, :], jnp.sin(ang)[None, :, None, :]


def _rope(x, cos, sin):
    x1, x2 = jnp.split(x.astype(jnp.float32), 2, axis=-1)
    return jnp.concatenate([x1 * cos - x2 * sin, x1 * sin + x2 * cos], -1)


def _dwconv(x, w):
    K, C = w.shape
    pad = (K - 1) // 2
    return lax.conv_general_dilated(x, w[:, None, :].astype(x.dtype), window_strides=(1,),
                                    padding=[(pad, pad)], dimension_numbers=('NWC', 'WIO', 'NWC'),
                                    feature_group_count=C)


def _ctx_attention(q, k, v):
    B, C, H, d = q.shape
    nb = C // Q_BLOCK
    qb = (q * d ** -0.5).reshape(B, nb, Q_BLOCK, H, d).transpose(1, 0, 2, 3, 4)

    def blk(q_blk):
        s = jnp.einsum('bqhd,bkhd->bhqk', q_blk, k).astype(jnp.float32)
        p = jax.nn.softmax(s, axis=-1).astype(v.dtype)
        return jnp.einsum('bhqk,bkhd->bqhd', p, v)

    o = lax.map(blk, qb)
    return o.transpose(1, 0, 2, 3, 4).reshape(B, C, H, d)


def _na_latent(q, k, v, k_ctx, v_ctx, rpb):
    B, T, H, d = q.shape
    rows = T // GRID_W
    wr = min(NA_WR, rows)
    qr = (q * d ** -0.5).reshape(B, rows, GRID_W, H, d).transpose(1, 0, 2, 3, 4)
    kr = k.reshape(B, rows, GRID_W, H, d)
    vr = v.reshape(B, rows, GRID_W, H, d)
    r_idx = jnp.arange(rows)
    r_start = jnp.clip(r_idx - wr // 2, 0, rows - wr)
    cols = jnp.arange(GRID_W)
    c_start = jnp.clip(cols - NA_WC // 2, 0, GRID_W - NA_WC)
    col_in = (cols[None, :] >= c_start[:, None]) & (cols[None, :] < c_start[:, None] + NA_WC)
    col_bias_idx = jnp.clip(cols[None, :] - cols[:, None], 1 - NA_WC, NA_WC - 1) + NA_WC - 1
    rpb_cols = jnp.take(rpb, col_bias_idx, axis=2)

    def row_block(args):
        q_blk, r, rs = args
        k_blk = lax.dynamic_slice_in_dim(kr, rs, wr, axis=1)
        v_blk = lax.dynamic_slice_in_dim(vr, rs, wr, axis=1)
        bias = jnp.take(rpb_cols, rs + jnp.arange(wr) - r + NA_WR - 1, axis=1)
        s_loc = jnp.einsum('bqhd,bwkhd->bhqwk', q_blk, k_blk).astype(jnp.float32)
        s_loc = s_loc + bias.transpose(0, 2, 1, 3)[None].astype(jnp.float32)
        s_loc = jnp.where(col_in[:, None, :], s_loc, -jnp.inf).reshape(B, H, GRID_W, wr * GRID_W)
        s_ctx = jnp.einsum('bqhd,bchd->bhqc', q_blk, k_ctx).astype(jnp.float32)
        p = jax.nn.softmax(jnp.concatenate([s_loc, s_ctx], -1), axis=-1).astype(v.dtype)
        p_loc = p[..., :wr * GRID_W].reshape(B, H, GRID_W, wr, GRID_W)
        p_ctx = p[..., wr * GRID_W:]
        return (jnp.einsum('bhqwk,bwkhd->bqhd', p_loc, v_blk)
                + jnp.einsum('bhqc,bchd->bqhd', p_ctx, v_ctx))

    o = lax.map(row_block, (qr, r_idx, r_start))
    return o.transpose(1, 0, 2, 3, 4).reshape(B, T, H, d)


def _retention_scan(q, k, v, log_gamma, s0):
    B, T, H, d = q.shape
    n = T // RET_CHUNK
    to_chunks = lambda a: a.reshape(B, n, RET_CHUNK, H, d).transpose(1, 0, 3, 2, 4)
    pos = jnp.arange(RET_CHUNK, dtype=jnp.float32)
    diff = pos[:, None] - pos[None, :]
    intra_decay = jnp.where(diff >= 0, jnp.exp(jnp.maximum(diff, 0.0) * log_gamma[:, None, None]), 0.0)
    q_decay = jnp.exp((pos + 1.0) * log_gamma[:, None])[..., None]
    k_decay = jnp.exp((RET_CHUNK - 1.0 - pos) * log_gamma[:, None])[..., None]
    chunk_decay = jnp.exp(RET_CHUNK * log_gamma)[:, None, None]

    def step(state, qkv):
        qc, kc, vc = qkv
        scores = jnp.einsum('bhid,bhjd->bhij', qc, kc) * intra_decay
        out = (jnp.einsum('bhij,bhjd->bhid', scores, vc)
               + jnp.einsum('bhid,bhde->bhie', qc, state) * q_decay)
        state = state * chunk_decay + jnp.einsum('bhjd,bhje->bhde', kc * k_decay, vc)
        return state, out

    state, out = lax.scan(step, s0, (to_chunks(q), to_chunks(k), to_chunks(v)))
    return out.transpose(1, 0, 3, 2, 4).reshape(B, T, H, d), state


def _retention(q, k, v, g, log_gamma, s0):
    B, T, H, d = q.shape
    q = q.astype(jnp.float32)
    k = k.astype(jnp.float32) * d ** -0.5
    v = v.astype(jnp.float32)
    s0 = s0.astype(jnp.float32)
    o_f, s_f = _retention_scan(q, k, v, log_gamma[0], s0[:, 0])
    o_b, s_b = _retention_scan(jnp.flip(q, 1), jnp.flip(k, 1), jnp.flip(v, 1), log_gamma[1], s0[:, 1])
    o = o_f + jnp.flip(o_b, 1)
    mu = jnp.mean(o, -1, keepdims=True)
    var = jnp.mean(jnp.square(o - mu), -1, keepdims=True)
    o = ((o - mu) * lax.rsqrt(var + LN_EPS)).reshape(B, T, H * d)
    return jax.nn.silu(g) * o.astype(g.dtype), jnp.stack([s_f, s_b], axis=1)


def _mixing(h, w_in, w_out, rpb, log_gamma, sc_w, cf_w, cf_g, cf_b, rope, ctx_k, ctx_v, ret_s0):
    B, T, _ = h.shape
    (na_q, na_k, na_v, r_q, r_k, r_v, r_g,
     sc_b, sc_c, sc_x, cf_a, cf_gate) = jnp.split(h @ w_in, N_IN_PARTS, axis=-1)
    heads = lambda a: a.reshape(B, T, H_GROUP, HEAD_DIM)
    na_q, na_k, na_v, r_q, r_k, r_v = [heads(a) for a in (na_q, na_k, na_v, r_q, r_k, r_v)]
    if ctx_k is None:
        o_a = _ctx_attention(na_q, na_k, na_v)
    else:
        o_a = _na_latent(na_q, na_k, na_v, ctx_k, ctx_v, rpb)
        cos, sin = rope
        r_q = _rope(r_q, cos, sin)
        r_k = _rope(r_k, cos, sin)
    o_b, ret_state = _retention(r_q, r_k, r_v, r_g, log_gamma, ret_s0)
    o_c = sc_b * _dwconv(sc_c * sc_x, sc_w)
    u = _dwconv(cf_a * jax.nn.sigmoid(cf_gate), cf_w)
    o_d = jax.nn.silu(_layernorm(u, cf_g, cf_b))
    mix = jnp.concatenate([o_a.reshape(B, T, W_GROUP).astype(h.dtype), o_b, o_c, o_d], -1) @ w_out
    return mix, na_k, na_v, ret_state


def _peer(h, wq, keys, u, v):
    B, T, D = h.shape
    xt = h.reshape(-1, PEER_BLOCK, D)

    def blk(xb):
        q = (xb @ wq).reshape(PEER_BLOCK, PEER_HEADS, 2, PEER_DK // 2)
        s = jnp.einsum('thpk,hpnk->thpn', q, keys).astype(jnp.float32)
        s_top, i_top = lax.top_k(s, PEER_TOPK)
        cand = s_top[..., 0, :, None] + s_top[..., 1, None, :]
        cand_idx = i_top[..., 0, :, None] * PEER_NKEYS + i_top[..., 1, None, :]
        best, pos = lax.top_k(cand.reshape(PEER_BLOCK, PEER_HEADS, -1), PEER_TOPK)
        experts = jnp.take_along_axis(cand_idx.reshape(PEER_BLOCK, PEER_HEADS, -1), pos, axis=-1)
        gate = jax.nn.softmax(best, axis=-1).astype(xb.dtype)
        ue = jnp.take(u, experts, axis=0)
        ve = jnp.take(v, experts, axis=0)
        act = jax.nn.gelu(jnp.einsum('thkd,td->thk', ue, xb), approximate=False)
        return jnp.einsum('thk,thkd->td', gate * act, ve)

    return lax.map(blk, xt).reshape(B, T, D)


def setup_inputs(seed: int = 0) -> dict:
    key = jax.random.key(seed)
    ks = jax.random.split(key, 26)
    nrm = lambda k, shape, s: s * jax.random.normal(k, shape, jnp.float32)
    D = D_MODEL
    d_in = N_IN_PARTS * W_GROUP
    decay_base = jnp.log(2.0 ** (5.0 + jnp.arange(H_GROUP, dtype=jnp.float32)) - 1.0)
    return {
        'x_prompt': nrm(ks[0], (BATCH, SEQ, D), 1.0),
        'x_sample': nrm(ks[1], (DEC_BATCH, DEC_SEQ, D), 1.0),
        'c': nrm(ks[2], (DEC_BATCH, D), 1.0),
        'cache_na_k': nrm(ks[3], (DEC_BATCH, DEPTH, PAST_LEN, H_GROUP, HEAD_DIM), 1.0),
        'cache_na_v': nrm(ks[4], (DEC_BATCH, DEPTH, PAST_LEN, H_GROUP, HEAD_DIM), 1.0),
        'state_ret': nrm(ks[5], (DEC_BATCH, DEPTH, 2, H_GROUP, HEAD_DIM, HEAD_DIM), 1.0),
        'c_ctx': nrm(ks[6], (D,), 1.0),
        'w_mod': nrm(ks[7], (DEPTH, D, 6 * D), 0.5 * D ** -0.5),
        'b_mod': nrm(ks[8], (DEPTH, 6 * D), 0.02),
        'w_in': nrm(ks[9], (DEPTH, D, d_in), D ** -0.5),
        'w_out': nrm(ks[10], (DEPTH, D_MIX, D), BETA * D_MIX ** -0.5),
        'na_rpb': nrm(ks[11], (DEPTH, H_GROUP, 2 * NA_WR - 1, 2 * NA_WC - 1), 0.1),
        'ret_decay': decay_base[None, None, :] + nrm(ks[12], (DEPTH, 2, H_GROUP), 0.1),
        'sc_w': nrm(ks[13], (DEPTH, SC_WIDTH, W_GROUP), SC_WIDTH ** -0.5),
        'cf_w': nrm(ks[14], (DEPTH, CF_WIDTH, W_GROUP), CF_WIDTH ** -0.5),
        'cf_ln_g': 1.0 + nrm(ks[15], (DEPTH, W_GROUP), 0.02),
        'cf_ln_b': nrm(ks[16], (DEPTH, W_GROUP), 0.02),
        'ln1_g': 1.0 + nrm(ks[17], (DEPTH, D), 0.02),
        'ln1_b': nrm(ks[18], (DEPTH, D), 0.02),
        'ln2_g': 1.0 + nrm(ks[19], (DEPTH, D), 0.02),
        'ln2_b': nrm(ks[20], (DEPTH, D), 0.02),
        'peer_wq': nrm(ks[21], (DEPTH, D, PEER_HEADS * PEER_DK), D ** -0.5),
        'peer_keys': nrm(ks[22], (DEPTH, PEER_HEADS, 2, PEER_NKEYS, PEER_DK // 2), (PEER_DK // 2) ** -0.5),
        'peer_u': nrm(ks[23], (DEPTH, N_EXPERTS, D), D ** -0.5),
        'peer_v': nrm(ks[24], (DEPTH, N_EXPERTS, D), BETA),
    }


def reference(x_prompt, x_sample, c, cache_na_k, cache_na_v, state_ret, c_ctx, w_mod, b_mod, w_in, w_out,
              na_rpb, ret_decay, sc_w, cf_w, cf_ln_g, cf_ln_b, ln1_g, ln1_b, ln2_g, ln2_b,
              peer_wq, peer_keys, peer_u, peer_v):
    n_req = x_prompt.shape[0]
    rope = _axial_rope(x_sample.shape[1])
    ret_zero = jnp.zeros((n_req, 2, H_GROUP, HEAD_DIM, HEAD_DIM), jnp.float32)
    xp, xs = x_prompt, x_sample
    ks_out, vs_out, ss_out = [], [], []
    for l in range(DEPTH):
        log_gamma = jax.nn.log_sigmoid(ret_decay[l].astype(jnp.float32))
        sh1, sc1, g1, sh2, sc2, g2 = _modulation(c_ctx, w_mod[l], b_mod[l])
        mix, k_ctx, v_ctx, s_ctx = _mixing(xp * (1 + sc1) + sh1, w_in[l], w_out[l], na_rpb[l], log_gamma,
                                           sc_w[l], cf_w[l], cf_ln_g[l], cf_ln_b[l],
                                           None, None, None, ret_zero)
        xp = _layernorm(ALPHA * xp + g1 * mix, ln1_g[l], ln1_b[l])
        ffn = _peer(xp * (1 + sc2) + sh2, peer_wq[l], peer_keys[l], peer_u[l], peer_v[l])
        xp = _layernorm(ALPHA * xp + g2 * ffn, ln2_g[l], ln2_b[l])
        ks_out.append(k_ctx)
        vs_out.append(v_ctx)
        ss_out.append(s_ctx)
        sh1, sc1, g1, sh2, sc2, g2 = _modulation(c, w_mod[l], b_mod[l])
        mix, _, _, _ = _mixing(xs * (1 + sc1) + sh1, w_in[l], w_out[l], na_rpb[l], log_gamma,
                               sc_w[l], cf_w[l], cf_ln_g[l], cf_ln_b[l],
                               rope, cache_na_k[:, l], cache_na_v[:, l], state_ret[:, l])
        xs = _layernorm(ALPHA * xs + g1 * mix, ln1_g[l], ln1_b[l])
        ffn = _peer(xs * (1 + sc2) + sh2, peer_wq[l], peer_keys[l], peer_u[l], peer_v[l])
        xs = _layernorm(ALPHA * xs + g2 * ffn, ln2_g[l], ln2_b[l])
    new_na_k = jnp.stack(ks_out, axis=1)
    new_na_v = jnp.stack(vs_out, axis=1)
    new_state_ret = jnp.stack(ss_out, axis=1)
    return (xp, xs, new_na_k, new_na_v, new_state_ret)
```

```python
import functools
import math

import jax
import jax.numpy as jnp
import numpy as np
from jax import lax
from jax.experimental import pallas as pl
from jax.experimental.pallas import tpu as pltpu

F32 = jnp.float32
BF16 = jnp.bfloat16

D_MODEL = 1024
DEPTH = 4
GRID_W = 64
HEAD_DIM = 64
W_GROUP = 256
H_GROUP = 4
N_IN_PARTS = 12
NA_WR = 8
NA_WC = 16
ROPE_BASE = 10000.0
SC_WIDTH = 3
CF_WIDTH = 31
PEER_HEADS = 8
PEER_NKEYS = 128
PEER_TOPK = 16
N_EXPERTS = PEER_NKEYS * PEER_NKEYS
LN_EPS = 1e-5
ALPHA = (2.0 * DEPTH) ** 0.25
NEG_BIG = -1e30

VMEM_LIMIT_BYTES = 56 * 1024 * 1024
LANES = 128

TOKEN_BLOCK = 512
PEER_TOKEN_BLOCK = 512
PEER_EXPERT_CHUNK = 1024


def _params(*sem):
    return pltpu.CompilerParams(dimension_semantics=sem, vmem_limit_bytes=VMEM_LIMIT_BYTES)


def _sigmoid(x):
    return 1.0 / (1.0 + jnp.exp(-x))


def _silu(x):
    return x * _sigmoid(x)


def _layernorm_rows(x, g, b):
    mu = jnp.mean(x, axis=-1, keepdims=True)
    d = x - mu
    var = jnp.mean(d * d, axis=-1, keepdims=True)
    return d * lax.rsqrt(var + LN_EPS) * g + b


def _dot(a, b):
    return jnp.dot(a, b, preferred_element_type=F32)


def _dot_nt(a, b):
    return lax.dot_general(a, b, (((1,), (1,)), ((), ())), preferred_element_type=F32)


def _dot_f32(a, b):
    hi = a.astype(BF16)
    lo = (a - hi.astype(F32)).astype(BF16)
    return _dot(hi, b) + _dot(lo, b)


def _head_masks():
    lane = lax.broadcasted_iota(jnp.int32, (1, W_GROUP), 1)
    return [((lane >= h * HEAD_DIM) & (lane < (h + 1) * HEAD_DIM)).astype(F32) for h in range(H_GROUP)]


def _mod_kernel(c_ref, w_ref, b_ref, o_ref):
    c = c_ref[...]
    s = _silu(c)
    o_ref[0] = jnp.dot(s, w_ref[0], precision=lax.Precision.HIGHEST,
                       preferred_element_type=F32) + b_ref[0]


def _modulation(cond8, w_mod, b_mod):
    nc = 1024
    n_out = w_mod.shape[-1]
    return pl.pallas_call(
        _mod_kernel,
        grid=(DEPTH, n_out // nc),
        in_specs=[
            pl.BlockSpec((8, D_MODEL), lambda l, j: (0, 0)),
            pl.BlockSpec((1, D_MODEL, nc), lambda l, j: (l, 0, j)),
            pl.BlockSpec((1, 1, nc), lambda l, j: (l, 0, j)),
        ],
        out_specs=pl.BlockSpec((1, 8, nc), lambda l, j: (l, 0, j)),
        out_shape=jax.ShapeDtypeStruct((DEPTH, 8, n_out), F32),
        compiler_params=_params("arbitrary", "arbitrary"),
        name="modulation",
    )(cond8, w_mod, b_mod.reshape(DEPTH, 1, n_out))


def _modmm_kernel(x_ref, sh_ref, sc_ref, w_ref, o_ref):
    h = x_ref[...] * (1.0 + sc_ref[0]) + sh_ref[0]
    o_ref[...] = _dot(h.astype(BF16), w_ref[...])


def _modmm(x, sh, sc, w, rows_per_cond):
    n, d = x.shape
    n_out = w.shape[1]
    tb = TOKEN_BLOCK
    bpc = rows_per_cond // tb
    vec = pl.BlockSpec((1, 1, d), lambda b: (b // bpc, 0, 0))
    return pl.pallas_call(
        _modmm_kernel,
        grid=(n // tb,),
        in_specs=[
            pl.BlockSpec((tb, d), lambda b: (b, 0)),
            vec, vec,
            pl.BlockSpec((d, n_out), lambda b: (0, 0)),
        ],
        out_specs=pl.BlockSpec((tb, n_out), lambda b: (b, 0)),
        out_shape=jax.ShapeDtypeStruct((n, n_out), F32),
        compiler_params=_params("arbitrary"),
        name="modulated_projection",
    )(x, sh, sc, w)


def _log_gamma_tile(decay_tile):
    y = -decay_tile
    return -(jnp.maximum(y, 0.0) + jnp.log1p(jnp.exp(-jnp.abs(y))))


def _lane_vector(lg_tile, row0, masks):
    out = lg_tile[row0:row0 + 1, 0:1] * masks[0]
    for h in range(1, H_GROUP):
        out = out + lg_tile[row0 + h:row0 + h + 1, 0:1] * masks[h]
    return out


def _retention_pairs(rq, rk, rv, lg_tile, masks, seq_len, q_block):
    kb = rk.astype(BF16)
    vb = rv.astype(BF16)
    blocks = []
    for q0 in range(0, seq_len, q_block):
        rqb = rq[q0:q0 + q_block]
        ti = lax.broadcasted_iota(jnp.int32, (q_block, seq_len), 0) + q0
        si = lax.broadcasted_iota(jnp.int32, (q_block, seq_len), 1)
        dist = (ti - si).astype(F32)
        fwd = jnp.maximum(dist, 0.0)
        bwd = jnp.maximum(-dist, 0.0)
        acc = jnp.zeros((q_block, W_GROUP), F32)
        for h in range(H_GROUP):
            lgf = lg_tile[h:h + 1, 0:1]
            lgb = lg_tile[H_GROUP + h:H_GROUP + h + 1, 0:1]
            decay = (jnp.where(dist >= 0, jnp.exp(lgf * fwd), 0.0)
                     + jnp.where(dist <= 0, jnp.exp(lgb * bwd), 0.0))
            s = _dot_nt((rqb * masks[h]).astype(BF16), kb)
            acc = acc + _dot((s * decay).astype(BF16), vb) * masks[h]
        blocks.append(acc)
    return blocks[0] if len(blocks) == 1 else jnp.concatenate(blocks, axis=0)


def _retention_finish(o, r_g, avg):
    mu = _dot_f32(o, avg)
    d = o - mu
    var = _dot_f32(d * d, avg)
    return _silu(r_g) * (d * lax.rsqrt(var + LN_EPS))


def _shift_rows(z, off, seq_len):
    if off == 0:
        return z
    rolled = pltpu.roll(z, (-off) % seq_len, 0)
    t = lax.broadcasted_iota(jnp.int32, z.shape, 0)
    valid = (t + off >= 0) & (t + off < seq_len)
    return jnp.where(valid, rolled, 0.0)


def _dwconv(z, w_ref, width, seq_len):
    pad = (width - 1) // 2
    acc = None
    for k in range(width):
        term = _shift_rows(z, k - pad, seq_len) * w_ref[k:k + 1, :]
        acc = term if acc is None else acc + term
    return acc


def _conv_mixers(sc_b, sc_c, sc_x, cf_a, cf_gate, scw_ref, cfw_ref, cfg_ref, cfb_ref, seq_len):
    o_c = sc_b * _dwconv(sc_c * sc_x, scw_ref, SC_WIDTH, seq_len)
    u = _dwconv(cf_a * _sigmoid(cf_gate), cfw_ref, CF_WIDTH, seq_len)
    o_d = _silu(_layernorm_rows(u, cfg_ref[...], cfb_ref[...]))
    return o_c, o_d


def _ctx_mixer_kernel(p_ref, dec_ref, avg_ref, scw_ref, cfw_ref, cfg_ref, cfb_ref,
                      mix_ref, k_ref, v_ref, st_ref, *, seq_len):
    masks = _head_masks()
    part = lambda j: p_ref[:, j * W_GROUP:(j + 1) * W_GROUP]
    na_q, na_k, na_v = part(0), part(1), part(2)
    k_ref[...] = na_k
    v_ref[...] = na_v

    kb = na_k.astype(BF16)
    vb = na_v.astype(BF16)
    qs = na_q * (HEAD_DIM ** -0.5)
    o_a = jnp.zeros((seq_len, W_GROUP), F32)
    for h in range(H_GROUP):
        s = _dot_nt((qs * masks[h]).astype(BF16), kb)
        m = jnp.max(s, axis=-1, keepdims=True)
        e = jnp.exp(s - m)
        p = e / jnp.sum(e, axis=-1, keepdims=True)
        o_a = o_a + _dot(p.astype(BF16), vb) * masks[h]
    mix_ref[:, 0:W_GROUP] = o_a

    lg = _log_gamma_tile(dec_ref[...])
    rq, rv, r_g = part(3), part(5), part(6)
    rk = part(4) * (HEAD_DIM ** -0.5)
    o = _retention_pairs(rq, rk, rv, lg, masks, seq_len, seq_len)
    mix_ref[:, W_GROUP:2 * W_GROUP] = _retention_finish(o, r_g, avg_ref[...])

    pos = lax.broadcasted_iota(jnp.int32, (seq_len, W_GROUP), 0).astype(F32)
    vb_r = rv.astype(BF16)
    for d in range(2):
        lane_lg = _lane_vector(lg, d * H_GROUP, masks)
        expo = (seq_len - 1.0 - pos) if d == 0 else pos
        kd = rk * jnp.exp(lane_lg * expo)
        full = _dot(kd.T.astype(BF16), vb_r)
        for h in range(H_GROUP):
            st_ref[0, d, h] = full[h * HEAD_DIM:(h + 1) * HEAD_DIM, h * HEAD_DIM:(h + 1) * HEAD_DIM]

    o_c, o_d = _conv_mixers(part(7), part(8), part(9), part(10), part(11),
                            scw_ref, cfw_ref, cfg_ref, cfb_ref, seq_len)
    mix_ref[:, 2 * W_GROUP:3 * W_GROUP] = o_c
    mix_ref[:, 3 * W_GROUP:4 * W_GROUP] = o_d


def _ctx_mixers(proj, n_req, seq_len, dec_tile, avg, scw, cfw, cfg, cfb):
    n = n_req * seq_len
    const = lambda shape: pl.BlockSpec(shape, lambda b: (0,) * len(shape))
    return pl.pallas_call(
        functools.partial(_ctx_mixer_kernel, seq_len=seq_len),
        grid=(n_req,),
        in_specs=[
            pl.BlockSpec((seq_len, N_IN_PARTS * W_GROUP), lambda b: (b, 0)),
            const((8, LANES)), const((W_GROUP, W_GROUP)), const((8, W_GROUP)),
            const((32, W_GROUP)), const((1, W_GROUP)), const((1, W_GROUP)),
        ],
        out_specs=[
            pl.BlockSpec((seq_len, D_MODEL), lambda b: (b, 0)),
            pl.BlockSpec((seq_len, W_GROUP), lambda b: (b, 0)),
            pl.BlockSpec((seq_len, W_GROUP), lambda b: (b, 0)),
            pl.BlockSpec((1, 2, H_GROUP, HEAD_DIM, HEAD_DIM), lambda b: (b, 0, 0, 0, 0)),
        ],
        out_shape=[
            jax.ShapeDtypeStruct((n, D_MODEL), F32),
            jax.ShapeDtypeStruct((n, W_GROUP), F32),
            jax.ShapeDtypeStruct((n, W_GROUP), F32),
            jax.ShapeDtypeStruct((n_req, 2, H_GROUP, HEAD_DIM, HEAD_DIM), F32),
        ],
        compiler_params=_params("arbitrary"),
        name="context_mixers",
    )(proj, dec_tile, avg, scw, cfw, cfg, cfb)


def _lat_attn_kernel(q_ref, k_ref, v_ref, kc_ref, vc_ref, bias_ref, o_ref, *, rows):
    masks = _head_masks()
    kcb = kc_ref[0].astype(BF16)
    vcb = vc_ref[0].astype(BF16)
    wr = min(NA_WR, rows)

    def row_block(r, carry):
        rs = jnp.clip(r - wr // 2, 0, rows - wr)
        q0 = pl.multiple_of(r * GRID_W, GRID_W)
        k0 = pl.multiple_of(rs * GRID_W, GRID_W)
        qb = q_ref[pl.ds(q0, GRID_W), :] * (HEAD_DIM ** -0.5)
        qs = jnp.concatenate([qb * masks[h] for h in range(H_GROUP)], axis=0).astype(BF16)
        kl = k_ref[pl.ds(k0, wr * GRID_W), :].astype(BF16)
        vl = v_ref[pl.ds(k0, wr * GRID_W), :].astype(BF16)
        s_loc = _dot_nt(qs, kl) + bias_ref[r]
        s_ctx = _dot_nt(qs, kcb)
        m = jnp.maximum(jnp.max(s_loc, axis=-1, keepdims=True), jnp.max(s_ctx, axis=-1, keepdims=True))
        e_loc = jnp.exp(s_loc - m)
        e_ctx = jnp.exp(s_ctx - m)
        z = jnp.sum(e_loc, axis=-1, keepdims=True) + jnp.sum(e_ctx, axis=-1, keepdims=True)
        o = (_dot(e_loc.astype(BF16), vl) + _dot(e_ctx.astype(BF16), vcb)) / z
        acc = o[0:GRID_W] * masks[0]
        for h in range(1, H_GROUP):
            acc = acc + o[h * GRID_W:(h + 1) * GRID_W] * masks[h]
        o_ref[pl.ds(q0, GRID_W), :] = acc
        return carry

    lax.fori_loop(0, rows, row_block, 0)


def _lat_attention(proj, n_req, seq_len, kc, vc, bias):
    rows = seq_len // GRID_W
    part = lambda j: pl.BlockSpec((seq_len, W_GROUP), lambda b, j=j: (b, j))
    past = kc.shape[1]
    return pl.pallas_call(
        functools.partial(_lat_attn_kernel, rows=rows),
        grid=(n_req,),
        in_specs=[
            part(0), part(1), part(2),
            pl.BlockSpec((1, past, W_GROUP), lambda b: (b, 0, 0)),
            pl.BlockSpec((1, past, W_GROUP), lambda b: (b, 0, 0)),
            pl.BlockSpec(bias.shape, lambda b: (0, 0, 0)),
        ],
        out_specs=pl.BlockSpec((seq_len, W_GROUP), lambda b: (b, 0)),
        out_shape=jax.ShapeDtypeStruct((n_req * seq_len, W_GROUP), F32),
        compiler_params=_params("arbitrary"),
        name="latent_attention",
    )(proj, proj, proj, kc, vc, bias)


def _lat_ret_kernel(q_ref, k_ref, v_ref, g_ref, qs_ref, ks_ref, cos_ref, sin_ref, s0_ref,
                    dec_ref, avg_ref, o_ref, *, seq_len):
    masks = _head_masks()
    lg = _log_gamma_tile(dec_ref[...])
    cos = cos_ref[...]
    sin = sin_ref[...]
    rq = q_ref[...] * cos + qs_ref[...] * sin
    rk = (k_ref[...] * cos + ks_ref[...] * sin) * (HEAD_DIM ** -0.5)
    rv = v_ref[...]
    o = _retention_pairs(rq, rk, rv, lg, masks, seq_len, 256)
    pos = lax.broadcasted_iota(jnp.int32, (seq_len, W_GROUP), 0).astype(F32)
    rqb = rq.astype(BF16)
    lgf = _lane_vector(lg, 0, masks)
    lgb = _lane_vector(lg, H_GROUP, masks)
    o = o + _dot(rqb, s0_ref[0, 0].astype(BF16)) * jnp.exp(lgf * (pos + 1.0))
    o = o + _dot(rqb, s0_ref[0, 1].astype(BF16)) * jnp.exp(lgb * (seq_len - pos))
    o_ref[...] = _retention_finish(o, g_ref[...], avg_ref[...])


def _lat_retention(proj, n_req, seq_len, cos, sin, s0_bd, dec_tile, avg):
    part = lambda j: pl.BlockSpec((seq_len, W_GROUP), lambda b, j=j: (b, j))
    const = lambda shape: pl.BlockSpec(shape, lambda b: (0,) * len(shape))
    return pl.pallas_call(
        functools.partial(_lat_ret_kernel, seq_len=seq_len),
        grid=(n_req,),
        in_specs=[
            part(3), part(4), part(5), part(6), part(12), part(13),
            const((seq_len, W_GROUP)), const((seq_len, W_GROUP)),
            pl.BlockSpec((1, 2, W_GROUP, W_GROUP), lambda b: (b, 0, 0, 0)),
            const((8, LANES)), const((W_GROUP, W_GROUP)),
        ],
        out_specs=pl.BlockSpec((seq_len, W_GROUP), lambda b: (b, 0)),
        out_shape=jax.ShapeDtypeStruct((n_req * seq_len, W_GROUP), F32),
        compiler_params=_params("arbitrary"),
        name="latent_retention",
    )(proj, proj, proj, proj, proj, proj, cos, sin, s0_bd, dec_tile, avg)


def _lat_conv_kernel(b_ref, c_ref, x_ref, a_ref, gate_ref, scw_ref, cfw_ref, cfg_ref, cfb_ref,
                     o_ref, *, seq_len):
    o_c, o_d = _conv_mixers(b_ref[...], c_ref[...], x_ref[...], a_ref[...], gate_ref[...],
                            scw_ref, cfw_ref, cfg_ref, cfb_ref, seq_len)
    o_ref[:, 0:W_GROUP] = o_c
    o_ref[:, W_GROUP:2 * W_GROUP] = o_d


def _lat_convs(proj, n_req, seq_len, scw, cfw, cfg, cfb):
    part = lambda j: pl.BlockSpec((seq_len, W_GROUP), lambda b, j=j: (b, j))
    const = lambda shape: pl.BlockSpec(shape, lambda b: (0,) * len(shape))
    return pl.pallas_call(
        functools.partial(_lat_conv_kernel, seq_len=seq_len),
        grid=(n_req,),
        in_specs=[
            part(7), part(8), part(9), part(10), part(11),
            const((8, W_GROUP)), const((32, W_GROUP)), const((1, W_GROUP)), const((1, W_GROUP)),
        ],
        out_specs=pl.BlockSpec((seq_len, 2 * W_GROUP), lambda b: (b, 0)),
        out_shape=jax.ShapeDtypeStruct((n_req * seq_len, 2 * W_GROUP), F32),
        compiler_params=_params("arbitrary"),
        name="latent_convs",
    )(proj, proj, proj, proj, proj, scw, cfw, cfg, cfb)


def _outproj_kernel(*refs, widths):
    n_parts = len(widths)
    part_refs = refs[:n_parts]
    x_ref, g_ref, w_ref, lg_ref, lb_ref, o_ref = refs[n_parts:]
    mix = None
    row = 0
    for p_ref, wd in zip(part_refs, widths):
        term = _dot(p_ref[...].astype(BF16), w_ref[row:row + wd, :])
        mix = term if mix is None else mix + term
        row += wd
    y = ALPHA * x_ref[...] + g_ref[0] * mix
    o_ref[...] = _layernorm_rows(y, lg_ref[...], lb_ref[...])


def _outproj_ln(parts, x, gate, w_out, ln_g, ln_b, rows_per_cond):
    n, d = x.shape
    tb = TOKEN_BLOCK
    bpc = rows_per_cond // tb
    widths = tuple(p.shape[1] for p in parts)
    return pl.pallas_call(
        functools.partial(_outproj_kernel, widths=widths),
        grid=(n // tb,),
        in_specs=[pl.BlockSpec((tb, wd), lambda b: (b, 0)) for wd in widths] + [
            pl.BlockSpec((tb, d), lambda b: (b, 0)),
            pl.BlockSpec((1, 1, d), lambda b: (b // bpc, 0, 0)),
            pl.BlockSpec((d, d), lambda b: (0, 0)),
            pl.BlockSpec((1, d), lambda b: (0, 0)),
            pl.BlockSpec((1, d), lambda b: (0, 0)),
        ],
        out_specs=pl.BlockSpec((tb, d), lambda b: (b, 0)),
        out_shape=jax.ShapeDtypeStruct((n, d), F32),
        compiler_params=_params("arbitrary"),
        name="output_projection_ln",
    )(*parts, x, gate, w_out, ln_g, ln_b)


def _extract_top(s, count):
    rows = []
    for _ in range(count):
        m = jnp.max(s, axis=0, keepdims=True)
        rows.append(m)
        s = jnp.where(s == m, -jnp.inf, s)
    return rows


def _peer_select_kernel(x_ref, sh_ref, sc_ref, wq_ref, keys_ref, hm_ref, a_ref, b_ref, tau_ref):
    hm = (x_ref[...] * (1.0 + sc_ref[0]) + sh_ref[0]).astype(BF16)
    hm_ref[...] = hm
    q = _dot(hm, wq_ref[...]).astype(BF16)
    dk = PEER_NKEYS
    tops = [[None, None] for _ in range(PEER_HEADS)]
    for h in range(PEER_HEADS):
        for p in range(2):
            j = h * 2 + p
            s = _dot_nt(keys_ref[j], q[:, j * dk:(j + 1) * dk])
            top = _extract_top(s, PEER_TOPK)
            e = jnp.where(s >= top[-1], jnp.exp(s - top[0]), 0.0)
            tops[h][p] = [jnp.exp(t - top[0]) for t in top]
            if p == 0:
                a_ref[h] = e
            else:
                b_ref[h] = e

    stack = lambda rows_per_head: jnp.concatenate(rows_per_head, axis=0)
    a_rank = [stack([tops[h][0][r] for h in range(PEER_HEADS)]) for r in range(PEER_TOPK)]
    b_rank = [stack([tops[h][1][r] for h in range(PEER_HEADS)]) for r in range(PEER_TOPK)]
    pairs = [(r, c) for r in range(PEER_TOPK) for c in range(PEER_TOPK) if (r + 1) * (c + 1) <= PEER_TOPK]
    cand = [a_rank[r] * b_rank[c] for r, c in pairs]
    rest = list(cand)
    kth = None
    for _ in range(PEER_TOPK):
        m = functools.reduce(jnp.maximum, rest)
        kth = m
        rest = [jnp.where(v == m, -1.0, v) for v in rest]
    z = functools.reduce(jnp.add, [jnp.where(v >= kth, v, 0.0) for v in cand])
    rz = 1.0 / z
    scaled = [a_rank[r] * (b_rank[c] * rz) for r, c in pairs]
    tau = functools.reduce(jnp.minimum,
                           [jnp.where(v >= kth, sv, jnp.inf) for v, sv in zip(cand, scaled)])
    tau_ref[...] = tau
    for h in range(PEER_HEADS):
        b_ref[h] = b_ref[h] * rz[h:h + 1, :]


def _peer_select(x, sh, sc, wq, keys, rows_per_cond):
    n, d = x.shape
    tb = TOKEN_BLOCK
    bpc = rows_per_cond // tb
    vec = pl.BlockSpec((1, 1, d), lambda b: (b // bpc, 0, 0))
    grid_hn = lambda: pl.BlockSpec((PEER_HEADS, PEER_NKEYS, tb), lambda b: (0, 0, b))
    return pl.pallas_call(
        _peer_select_kernel,
        grid=(n // tb,),
        in_specs=[
            pl.BlockSpec((tb, d), lambda b: (b, 0)),
            vec, vec,
            pl.BlockSpec(wq.shape, lambda b: (0, 0)),
            pl.BlockSpec(keys.shape, lambda b: (0, 0, 0)),
        ],
        out_specs=[
            pl.BlockSpec((tb, d), lambda b: (b, 0)),
            grid_hn(), grid_hn(),
            pl.BlockSpec((PEER_HEADS, tb), lambda b: (0, b)),
        ],
        out_shape=[
            jax.ShapeDtypeStruct((n, d), BF16),
            jax.ShapeDtypeStruct((PEER_HEADS, PEER_NKEYS, n), F32),
            jax.ShapeDtypeStruct((PEER_HEADS, PEER_NKEYS, n), F32),
            jax.ShapeDtypeStruct((PEER_HEADS, n), F32),
        ],
        compiler_params=_params("arbitrary"),
        name="peer_select",
    )(x, sh, sc, wq, keys)


def _gelu(x):
    return 0.5 * x * (1.0 + lax.erf(x * (2.0 ** -0.5)))


def _peer_expert_kernel(hm_ref, a_ref, b_ref, tau_ref, u_ref, v_ref, x_ref, g_ref, lg_ref, lb_ref,
                        o_ref, acc_ref, ht_ref, gh_ref, *, n_chunks, tok_block):
    c = pl.program_id(1)

    @pl.when(c == 0)
    def _():
        acc_ref[...] = jnp.zeros_like(acc_ref)

    ht_ref[...] = _dot_nt(u_ref[...], hm_ref[...])

    rows_per_chunk = PEER_EXPERT_CHUNK // PEER_NKEYS
    groups = tok_block // LANES

    def tile(tg, carry):
        tok = pl.ds(pl.multiple_of(tg * LANES, LANES), LANES)
        for i in range(rows_per_chunk):
            rows = pl.ds(i * PEER_NKEYS, PEER_NKEYS)
            gate = jnp.zeros((PEER_NKEYS, LANES), F32)
            for h in range(PEER_HEADS):
                e = a_ref[h, i:i + 1, tok] * b_ref[h, :, tok]
                gate = jnp.where(e >= tau_ref[h:h + 1, tok], gate + e, gate)
            gh_ref[rows, tok] = (gate * _gelu(ht_ref[rows, tok])).astype(BF16)
        return carry

    lax.fori_loop(0, groups, tile, 0)

    acc_ref[...] += _dot(v_ref[...].T, gh_ref[...])

    @pl.when(c == n_chunks - 1)
    def _():
        ffn = acc_ref[...].T
        y = ALPHA * x_ref[...] + g_ref[0] * ffn
        o_ref[...] = _layernorm_rows(y, lg_ref[...], lb_ref[...])


def _peer_experts(hm, a_t, b_t, tau, u, v, x, gate, ln_g, ln_b, rows_per_cond):
    n, d = x.shape
    tb = PEER_TOKEN_BLOCK
    ec = PEER_EXPERT_CHUNK
    n_chunks = N_EXPERTS // ec
    rpc = ec // PEER_NKEYS
    bpc = rows_per_cond // tb
    return pl.pallas_call(
        functools.partial(_peer_expert_kernel, n_chunks=n_chunks, tok_block=tb),
        grid=(n // tb, n_chunks),
        in_specs=[
            pl.BlockSpec((tb, d), lambda b, c: (b, 0)),
            pl.BlockSpec((PEER_HEADS, rpc, tb), lambda b, c: (0, c, b)),
            pl.BlockSpec((PEER_HEADS, PEER_NKEYS, tb), lambda b, c: (0, 0, b)),
            pl.BlockSpec((PEER_HEADS, tb), lambda b, c: (0, b)),
            pl.BlockSpec((ec, d), lambda b, c: (c, 0)),
            pl.BlockSpec((ec, d), lambda b, c: (c, 0)),
            pl.BlockSpec((tb, d), lambda b, c: (b, 0)),
            pl.BlockSpec((1, 1, d), lambda b, c: (b // bpc, 0, 0)),
            pl.BlockSpec((1, d), lambda b, c: (0, 0)),
            pl.BlockSpec((1, d), lambda b, c: (0, 0)),
        ],
        out_specs=pl.BlockSpec((tb, d), lambda b, c: (b, 0)),
        out_shape=jax.ShapeDtypeStruct((n, d), F32),
        scratch_shapes=[
            pltpu.VMEM((d, tb), F32),
            pltpu.VMEM((ec, tb), F32),
            pltpu.VMEM((ec, tb), BF16),
        ],
        compiler_params=_params("arbitrary", "arbitrary"),
        name="peer_experts",
    )(hm, a_t, b_t, tau, u, v, x, gate, ln_g, ln_b)


def _rope_tables(seq_len):
    t = np.arange(seq_len)
    row = (t // GRID_W).astype(np.float32)
    col = (t % GRID_W).astype(np.float32)
    n_freq = HEAD_DIM // 4
    inv = (ROPE_BASE ** (-np.arange(n_freq, dtype=np.float32) / n_freq)).astype(np.float32)
    ang = jnp.asarray(np.concatenate([row[:, None] * inv, col[:, None] * inv], -1).astype(np.float32))
    cos, sin = jnp.cos(ang), jnp.sin(ang)
    cos_t = jnp.tile(jnp.concatenate([cos, cos], -1), (1, H_GROUP))
    sin_t = jnp.tile(jnp.concatenate([-sin, sin], -1), (1, H_GROUP))
    return cos_t, sin_t


def _swap_halves_perm():
    idx = np.arange(W_GROUP)
    half = HEAD_DIM // 2
    return np.where((idx % HEAD_DIM) < half, idx + half, idx - half)


def _attention_bias(rpb, rows):
    wr = min(NA_WR, rows)
    cols = np.arange(GRID_W)
    c_start = np.clip(cols - NA_WC // 2, 0, GRID_W - NA_WC)
    col_in = (cols[None, :] >= c_start[:, None]) & (cols[None, :] < c_start[:, None] + NA_WC)
    col_idx = np.clip(cols[None, :] - cols[:, None], 1 - NA_WC, NA_WC - 1) + NA_WC - 1
    rpb_cols = rpb[:, :, col_idx]
    masked = jnp.where(jnp.asarray(col_in)[None, None], rpb_cols, NEG_BIG)
    tables = []
    for r in range(rows):
        rs = int(np.clip(r - wr // 2, 0, rows - wr))
        d0 = rs - r + NA_WR - 1
        blk = masked[:, d0:d0 + wr]
        tables.append(blk.transpose(0, 2, 1, 3).reshape(H_GROUP * GRID_W, wr * GRID_W))
    return jnp.stack(tables, axis=0)


def _block_diag_states(s0):
    b = s0.shape[0]
    eye = jnp.eye(H_GROUP, dtype=s0.dtype)
    full = jnp.einsum('bchde,hg->bchdge', s0, eye)
    return full.reshape(b, 2, H_GROUP * HEAD_DIM, H_GROUP * HEAD_DIM)


def kernel(x_prompt, x_sample, c, cache_na_k, cache_na_v, state_ret, c_ctx, w_mod, b_mod, w_in, w_out,
           na_rpb, ret_decay, sc_w, cf_w, cf_ln_g, cf_ln_b, ln1_g, ln1_b, ln2_g, ln2_b,
           peer_wq, peer_keys, peer_u, peer_v):
    n_req, seq, d = x_prompt.shape
    n_lat, lat_seq, _ = x_sample.shape
    past = cache_na_k.shape[2]
    rows = lat_seq // GRID_W

    cond8 = jnp.concatenate([c_ctx[None], c, jnp.zeros((8 - 1 - n_lat, d), F32)], axis=0)
    mod = _modulation(cond8, w_mod, b_mod)

    def mod_vec(l, j, lo, hi):
        return mod[l, lo:hi, j * d:(j + 1) * d][:, None, :]

    perm = _swap_halves_perm()
    w_in_b = w_in.astype(BF16)
    w_in_lat = jnp.concatenate([w_in_b, w_in_b[:, :, 3 * W_GROUP + perm], w_in_b[:, :, 4 * W_GROUP + perm]], -1)
    w_out_b = w_out.astype(BF16)
    wq_b = peer_wq.astype(BF16)
    keys_b = peer_keys.astype(BF16).reshape(DEPTH, PEER_HEADS * 2, PEER_NKEYS, PEER_NKEYS)
    u_b = peer_u.astype(BF16)
    v_b = peer_v.astype(BF16)

    lane_h = np.arange(W_GROUP) // HEAD_DIM
    avg = jnp.asarray((lane_h[:, None] == lane_h[None, :]).astype(np.float32) / HEAD_DIM).astype(BF16)
    cos_t, sin_t = _rope_tables(lat_seq)
    dec_tiles = jnp.broadcast_to(ret_decay.reshape(DEPTH, 2 * H_GROUP, 1), (DEPTH, 8, LANES))
    scw = jnp.pad(sc_w, ((0, 0), (0, 8 - SC_WIDTH), (0, 0)))
    cfw = jnp.pad(cf_w, ((0, 0), (0, 32 - CF_WIDTH), (0, 0)))
    kc_all = cache_na_k.reshape(n_lat, DEPTH, past, W_GROUP)
    vc_all = cache_na_v.reshape(n_lat, DEPTH, past, W_GROUP)

    xp = x_prompt.reshape(n_req * seq, d)
    xs = x_sample.reshape(n_lat * lat_seq, d)
    ks_out, vs_out, ss_out = [], [], []

    for l in range(DEPTH):
        row1 = lambda a: a[l][None, :]
        cfg, cfb = row1(cf_ln_g), row1(cf_ln_b)
        l1g, l1b, l2g, l2b = row1(ln1_g), row1(ln1_b), row1(ln2_g), row1(ln2_b)

        n_ctx = n_req * seq
        proj = _modmm(xp, mod_vec(l, 0, 0, 1), mod_vec(l, 1, 0, 1), w_in_b[l], n_ctx)
        mix, k_ctx, v_ctx, s_ctx = _ctx_mixers(proj, n_req, seq, dec_tiles[l], avg, scw[l], cfw[l], cfg, cfb)
        xp = _outproj_ln([mix], xp, mod_vec(l, 2, 0, 1), w_out_b[l], l1g, l1b, n_ctx)
        hm, a_t, b_t, tau = _peer_select(xp, mod_vec(l, 3, 0, 1), mod_vec(l, 4, 0, 1), wq_b[l], keys_b[l], n_ctx)
        xp = _peer_experts(hm, a_t, b_t, tau, u_b[l], v_b[l], xp, mod_vec(l, 5, 0, 1), l2g, l2b, n_ctx)
        ks_out.append(k_ctx.reshape(n_req, seq, H_GROUP, HEAD_DIM))
        vs_out.append(v_ctx.reshape(n_req, seq, H_GROUP, HEAD_DIM))
        ss_out.append(s_ctx)

        hi = 1 + n_lat
        proj = _modmm(xs, mod_vec(l, 0, 1, hi), mod_vec(l, 1, 1, hi), w_in_lat[l], lat_seq)
        bias = _attention_bias(na_rpb[l], rows)
        o_a = _lat_attention(proj, n_lat, lat_seq, kc_all[:, l], vc_all[:, l], bias)
        o_b = _lat_retention(proj, n_lat, lat_seq, cos_t, sin_t, _block_diag_states(state_ret[:, l]),
                             dec_tiles[l], avg)
        o_cd = _lat_convs(proj, n_lat, lat_seq, scw[l], cfw[l], cfg, cfb)
        xs = _outproj_ln([o_a, o_b, o_cd], xs, mod_vec(l, 2, 1, hi), w_out_b[l], l1g, l1b, lat_seq)
        hm, a_t, b_t, tau = _peer_select(xs, mod_vec(l, 3, 1, hi), mod_vec(l, 4, 1, hi), wq_b[l], keys_b[l], lat_seq)
        xs = _peer_experts(hm, a_t, b_t, tau, u_b[l], v_b[l], xs, mod_vec(l, 5, 1, hi), l2g, l2b, lat_seq)

    return (xp.reshape(n_req, seq, d), xs.reshape(n_lat, lat_seq, d),
            jnp.stack(ks_out, axis=1), jnp.stack(vs_out, axis=1), jnp.stack(ss_out, axis=1))
```

```python
import functools
import math

import jax
import jax.numpy as jnp
import numpy as np
from jax import lax
from jax.experimental import pallas as pl
from jax.experimental.pallas import tpu as pltpu

F32 = jnp.float32
BF16 = jnp.bfloat16

D_MODEL = 1024
DEPTH = 4
GRID_W = 64
HEAD_DIM = 64
W_GROUP = 256
H_GROUP = 4
N_IN_PARTS = 12
NA_WR = 8
NA_WC = 16
ROPE_BASE = 10000.0
SC_WIDTH = 3
CF_WIDTH = 31
PEER_HEADS = 8
PEER_NKEYS = 128
PEER_TOPK = 16
N_EXPERTS = PEER_NKEYS * PEER_NKEYS
LN_EPS = 1e-5
ALPHA = (2.0 * DEPTH) ** 0.25
NEG_BIG = -1e30

VMEM_LIMIT_BYTES = 56 * 1024 * 1024
LANES = 128

TOKEN_BLOCK = 512
PEER_TOKEN_BLOCK = 1024
PEER_TOKEN_GROUP = 256
PEER_EXPERT_CHUNK = 1024


def _params(*sem):
    return pltpu.CompilerParams(dimension_semantics=sem, vmem_limit_bytes=VMEM_LIMIT_BYTES)


def _sigmoid(x):
    return 1.0 / (1.0 + jnp.exp(-x))


def _silu(x):
    return x * _sigmoid(x)


def _layernorm_rows(x, g, b):
    mu = jnp.mean(x, axis=-1, keepdims=True)
    d = x - mu
    var = jnp.mean(d * d, axis=-1, keepdims=True)
    return d * lax.rsqrt(var + LN_EPS) * g + b


def _dot(a, b):
    return jnp.dot(a, b, preferred_element_type=F32)


def _dot_nt(a, b):
    return lax.dot_general(a, b, (((1,), (1,)), ((), ())), preferred_element_type=F32)


def _dot_f32(a, b):
    hi = a.astype(BF16)
    lo = (a - hi.astype(F32)).astype(BF16)
    return _dot(hi, b) + _dot(lo, b)


def _head_masks():
    lane = lax.broadcasted_iota(jnp.int32, (1, W_GROUP), 1)
    return [((lane >= h * HEAD_DIM) & (lane < (h + 1) * HEAD_DIM)).astype(F32) for h in range(H_GROUP)]


def _mod_kernel(c_ref, w_ref, b_ref, o_ref):
    c = c_ref[...]
    s = _silu(c)
    o_ref[0] = jnp.dot(s, w_ref[0], precision=lax.Precision.HIGHEST,
                       preferred_element_type=F32) + b_ref[0]


def _modulation(cond8, w_mod, b_mod):
    nc = 1024
    n_out = w_mod.shape[-1]
    return pl.pallas_call(
        _mod_kernel,
        grid=(DEPTH, n_out // nc),
        in_specs=[
            pl.BlockSpec((8, D_MODEL), lambda l, j: (0, 0)),
            pl.BlockSpec((1, D_MODEL, nc), lambda l, j: (l, 0, j)),
            pl.BlockSpec((1, 1, nc), lambda l, j: (l, 0, j)),
        ],
        out_specs=pl.BlockSpec((1, 8, nc), lambda l, j: (l, 0, j)),
        out_shape=jax.ShapeDtypeStruct((DEPTH, 8, n_out), F32),
        compiler_params=_params("arbitrary", "arbitrary"),
        name="modulation",
    )(cond8, w_mod, b_mod.reshape(DEPTH, 1, n_out))


def _modmm_kernel(x_ref, sh_ref, sc_ref, w_ref, o_ref):
    h = x_ref[...] * (1.0 + sc_ref[0]) + sh_ref[0]
    o_ref[...] = _dot(h.astype(BF16), w_ref[...])


def _modmm(x, sh, sc, w, rows_per_cond):
    n, d = x.shape
    n_out = w.shape[1]
    tb = TOKEN_BLOCK
    bpc = rows_per_cond // tb
    vec = pl.BlockSpec((1, 1, d), lambda b: (b // bpc, 0, 0))
    return pl.pallas_call(
        _modmm_kernel,
        grid=(n // tb,),
        in_specs=[
            pl.BlockSpec((tb, d), lambda b: (b, 0)),
            vec, vec,
            pl.BlockSpec((d, n_out), lambda b: (0, 0)),
        ],
        out_specs=pl.BlockSpec((tb, n_out), lambda b: (b, 0)),
        out_shape=jax.ShapeDtypeStruct((n, n_out), F32),
        compiler_params=_params("arbitrary"),
        name="modulated_projection",
    )(x, sh, sc, w)


def _log_gamma_tile(decay_tile):
    y = -decay_tile
    return -(jnp.maximum(y, 0.0) + jnp.log1p(jnp.exp(-jnp.abs(y))))


def _lane_vector(lg_tile, row0, masks):
    out = lg_tile[row0:row0 + 1, 0:1] * masks[0]
    for h in range(1, H_GROUP):
        out = out + lg_tile[row0 + h:row0 + h + 1, 0:1] * masks[h]
    return out


def _retention_pairs(rq, rk, rv, lg_tile, masks, seq_len, q_block):
    kb = rk.astype(BF16)
    vb = rv.astype(BF16)
    blocks = []
    for q0 in range(0, seq_len, q_block):
        rqb = rq[q0:q0 + q_block]
        ti = lax.broadcasted_iota(jnp.int32, (q_block, seq_len), 0) + q0
        si = lax.broadcasted_iota(jnp.int32, (q_block, seq_len), 1)
        dist = (ti - si).astype(F32)
        fwd = jnp.maximum(dist, 0.0)
        bwd = jnp.maximum(-dist, 0.0)
        acc = jnp.zeros((q_block, W_GROUP), F32)
        for h in range(H_GROUP):
            lgf = lg_tile[h:h + 1, 0:1]
            lgb = lg_tile[H_GROUP + h:H_GROUP + h + 1, 0:1]
            decay = (jnp.where(dist >= 0, jnp.exp(lgf * fwd), 0.0)
                     + jnp.where(dist <= 0, jnp.exp(lgb * bwd), 0.0))
            s = _dot_nt((rqb * masks[h]).astype(BF16), kb)
            acc = acc + _dot((s * decay).astype(BF16), vb) * masks[h]
        blocks.append(acc)
    return blocks[0] if len(blocks) == 1 else jnp.concatenate(blocks, axis=0)


def _retention_finish(o, r_g, avg):
    mu = _dot_f32(o, avg)
    d = o - mu
    var = _dot_f32(d * d, avg)
    return _silu(r_g) * (d * lax.rsqrt(var + LN_EPS))


def _shift_rows(z, off, seq_len):
    if off == 0:
        return z
    rolled = pltpu.roll(z, (-off) % seq_len, 0)
    t = lax.broadcasted_iota(jnp.int32, z.shape, 0)
    valid = (t + off >= 0) & (t + off < seq_len)
    return jnp.where(valid, rolled, 0.0)


def _dwconv(z, w_ref, width, seq_len):
    pad = (width - 1) // 2
    acc = None
    for k in range(width):
        term = _shift_rows(z, k - pad, seq_len) * w_ref[k:k + 1, :]
        acc = term if acc is None else acc + term
    return acc


def _conv_mixers(sc_b, sc_c, sc_x, cf_a, cf_gate, scw_ref, cfw_ref, cfg_ref, cfb_ref, seq_len):
    o_c = sc_b * _dwconv(sc_c * sc_x, scw_ref, SC_WIDTH, seq_len)
    u = _dwconv(cf_a * _sigmoid(cf_gate), cfw_ref, CF_WIDTH, seq_len)
    o_d = _silu(_layernorm_rows(u, cfg_ref[...], cfb_ref[...]))
    return o_c, o_d


def _ctx_mixer_kernel(p_ref, dec_ref, avg_ref, scw_ref, cfw_ref, cfg_ref, cfb_ref,
                      mix_ref, k_ref, v_ref, st_ref, *, seq_len):
    masks = _head_masks()
    part = lambda j: p_ref[:, j * W_GROUP:(j + 1) * W_GROUP]
    na_q, na_k, na_v = part(0), part(1), part(2)
    k_ref[...] = na_k
    v_ref[...] = na_v

    kb = na_k.astype(BF16)
    vb = na_v.astype(BF16)
    qs = na_q * (HEAD_DIM ** -0.5)
    o_a = jnp.zeros((seq_len, W_GROUP), F32)
    for h in range(H_GROUP):
        s = _dot_nt((qs * masks[h]).astype(BF16), kb)
        m = jnp.max(s, axis=-1, keepdims=True)
        e = jnp.exp(s - m)
        p = e / jnp.sum(e, axis=-1, keepdims=True)
        o_a = o_a + _dot(p.astype(BF16), vb) * masks[h]
    mix_ref[:, 0:W_GROUP] = o_a

    lg = _log_gamma_tile(dec_ref[...])
    rq, rv, r_g = part(3), part(5), part(6)
    rk = part(4) * (HEAD_DIM ** -0.5)
    o = _retention_pairs(rq, rk, rv, lg, masks, seq_len, seq_len)
    mix_ref[:, W_GROUP:2 * W_GROUP] = _retention_finish(o, r_g, avg_ref[...])

    pos = lax.broadcasted_iota(jnp.int32, (seq_len, W_GROUP), 0).astype(F32)
    vb_r = rv.astype(BF16)
    for d in range(2):
        lane_lg = _lane_vector(lg, d * H_GROUP, masks)
        expo = (seq_len - 1.0 - pos) if d == 0 else pos
        kd = rk * jnp.exp(lane_lg * expo)
        full = _dot(kd.T.astype(BF16), vb_r)
        for h in range(H_GROUP):
            st_ref[0, d, h] = full[h * HEAD_DIM:(h + 1) * HEAD_DIM, h * HEAD_DIM:(h + 1) * HEAD_DIM]

    o_c, o_d = _conv_mixers(part(7), part(8), part(9), part(10), part(11),
                            scw_ref, cfw_ref, cfg_ref, cfb_ref, seq_len)
    mix_ref[:, 2 * W_GROUP:3 * W_GROUP] = o_c
    mix_ref[:, 3 * W_GROUP:4 * W_GROUP] = o_d


def _ctx_mixers(proj, n_req, seq_len, dec_tile, avg, scw, cfw, cfg, cfb):
    n = n_req * seq_len
    const = lambda shape: pl.BlockSpec(shape, lambda b: (0,) * len(shape))
    return pl.pallas_call(
        functools.partial(_ctx_mixer_kernel, seq_len=seq_len),
        grid=(n_req,),
        in_specs=[
            pl.BlockSpec((seq_len, N_IN_PARTS * W_GROUP), lambda b: (b, 0)),
            const((8, LANES)), const((W_GROUP, W_GROUP)), const((8, W_GROUP)),
            const((32, W_GROUP)), const((1, W_GROUP)), const((1, W_GROUP)),
        ],
        out_specs=[
            pl.BlockSpec((seq_len, D_MODEL), lambda b: (b, 0)),
            pl.BlockSpec((seq_len, W_GROUP), lambda b: (b, 0)),
            pl.BlockSpec((seq_len, W_GROUP), lambda b: (b, 0)),
            pl.BlockSpec((1, 2, H_GROUP, HEAD_DIM, HEAD_DIM), lambda b: (b, 0, 0, 0, 0)),
        ],
        out_shape=[
            jax.ShapeDtypeStruct((n, D_MODEL), F32),
            jax.ShapeDtypeStruct((n, W_GROUP), F32),
            jax.ShapeDtypeStruct((n, W_GROUP), F32),
            jax.ShapeDtypeStruct((n_req, 2, H_GROUP, HEAD_DIM, HEAD_DIM), F32),
        ],
        compiler_params=_params("arbitrary"),
        name="context_mixers",
    )(proj, dec_tile, avg, scw, cfw, cfg, cfb)


def _lat_attn_kernel(q_ref, k_ref, v_ref, kc_ref, vc_ref, bias_ref, o_ref, *, rows):
    masks = _head_masks()
    kcb = kc_ref[0].astype(BF16)
    vcb = vc_ref[0].astype(BF16)
    wr = min(NA_WR, rows)

    def row_block(r, carry):
        rs = jnp.clip(r - wr // 2, 0, rows - wr)
        q0 = pl.multiple_of(r * GRID_W, GRID_W)
        k0 = pl.multiple_of(rs * GRID_W, GRID_W)
        qb = q_ref[pl.ds(q0, GRID_W), :] * (HEAD_DIM ** -0.5)
        qs = jnp.concatenate([qb * masks[h] for h in range(H_GROUP)], axis=0).astype(BF16)
        kl = k_ref[pl.ds(k0, wr * GRID_W), :].astype(BF16)
        vl = v_ref[pl.ds(k0, wr * GRID_W), :].astype(BF16)
        s_loc = _dot_nt(qs, kl) + bias_ref[r]
        s_ctx = _dot_nt(qs, kcb)
        m = jnp.maximum(jnp.max(s_loc, axis=-1, keepdims=True), jnp.max(s_ctx, axis=-1, keepdims=True))
        e_loc = jnp.exp(s_loc - m)
        e_ctx = jnp.exp(s_ctx - m)
        z = jnp.sum(e_loc, axis=-1, keepdims=True) + jnp.sum(e_ctx, axis=-1, keepdims=True)
        o = (_dot(e_loc.astype(BF16), vl) + _dot(e_ctx.astype(BF16), vcb)) / z
        acc = o[0:GRID_W] * masks[0]
        for h in range(1, H_GROUP):
            acc = acc + o[h * GRID_W:(h + 1) * GRID_W] * masks[h]
        o_ref[pl.ds(q0, GRID_W), :] = acc
        return carry

    lax.fori_loop(0, rows, row_block, 0)


def _lat_attention(proj, n_req, seq_len, kc, vc, bias):
    rows = seq_len // GRID_W
    part = lambda j: pl.BlockSpec((seq_len, W_GROUP), lambda b, j=j: (b, j))
    past = kc.shape[1]
    return pl.pallas_call(
        functools.partial(_lat_attn_kernel, rows=rows),
        grid=(n_req,),
        in_specs=[
            part(0), part(1), part(2),
            pl.BlockSpec((1, past, W_GROUP), lambda b: (b, 0, 0)),
            pl.BlockSpec((1, past, W_GROUP), lambda b: (b, 0, 0)),
            pl.BlockSpec(bias.shape, lambda b: (0, 0, 0)),
        ],
        out_specs=pl.BlockSpec((seq_len, W_GROUP), lambda b: (b, 0)),
        out_shape=jax.ShapeDtypeStruct((n_req * seq_len, W_GROUP), F32),
        compiler_params=_params("arbitrary"),
        name="latent_attention",
    )(proj, proj, proj, kc, vc, bias)


def _lat_ret_kernel(q_ref, k_ref, v_ref, g_ref, qs_ref, ks_ref, cos_ref, sin_ref, s0_ref,
                    dec_ref, avg_ref, o_ref, *, seq_len):
    masks = _head_masks()
    lg = _log_gamma_tile(dec_ref[...])
    cos = cos_ref[...]
    sin = sin_ref[...]
    rq = q_ref[...] * cos + qs_ref[...] * sin
    rk = (k_ref[...] * cos + ks_ref[...] * sin) * (HEAD_DIM ** -0.5)
    rv = v_ref[...]
    o = _retention_pairs(rq, rk, rv, lg, masks, seq_len, 256)
    pos = lax.broadcasted_iota(jnp.int32, (seq_len, W_GROUP), 0).astype(F32)
    rqb = rq.astype(BF16)
    lgf = _lane_vector(lg, 0, masks)
    lgb = _lane_vector(lg, H_GROUP, masks)
    o = o + _dot(rqb, s0_ref[0, 0].astype(BF16)) * jnp.exp(lgf * (pos + 1.0))
    o = o + _dot(rqb, s0_ref[0, 1].astype(BF16)) * jnp.exp(lgb * (seq_len - pos))
    o_ref[...] = _retention_finish(o, g_ref[...], avg_ref[...])


def _lat_retention(proj, n_req, seq_len, cos, sin, s0_bd, dec_tile, avg):
    part = lambda j: pl.BlockSpec((seq_len, W_GROUP), lambda b, j=j: (b, j))
    const = lambda shape: pl.BlockSpec(shape, lambda b: (0,) * len(shape))
    return pl.pallas_call(
        functools.partial(_lat_ret_kernel, seq_len=seq_len),
        grid=(n_req,),
        in_specs=[
            part(3), part(4), part(5), part(6), part(12), part(13),
            const((seq_len, W_GROUP)), const((seq_len, W_GROUP)),
            pl.BlockSpec((1, 2, W_GROUP, W_GROUP), lambda b: (b, 0, 0, 0)),
            const((8, LANES)), const((W_GROUP, W_GROUP)),
        ],
        out_specs=pl.BlockSpec((seq_len, W_GROUP), lambda b: (b, 0)),
        out_shape=jax.ShapeDtypeStruct((n_req * seq_len, W_GROUP), F32),
        compiler_params=_params("arbitrary"),
        name="latent_retention",
    )(proj, proj, proj, proj, proj, proj, cos, sin, s0_bd, dec_tile, avg)


def _lat_conv_kernel(b_ref, c_ref, x_ref, a_ref, gate_ref, scw_ref, cfw_ref, cfg_ref, cfb_ref,
                     o_ref, *, seq_len):
    o_c, o_d = _conv_mixers(b_ref[...], c_ref[...], x_ref[...], a_ref[...], gate_ref[...],
                            scw_ref, cfw_ref, cfg_ref, cfb_ref, seq_len)
    o_ref[:, 0:W_GROUP] = o_c
    o_ref[:, W_GROUP:2 * W_GROUP] = o_d


def _lat_convs(proj, n_req, seq_len, scw, cfw, cfg, cfb):
    part = lambda j: pl.BlockSpec((seq_len, W_GROUP), lambda b, j=j: (b, j))
    const = lambda shape: pl.BlockSpec(shape, lambda b: (0,) * len(shape))
    return pl.pallas_call(
        functools.partial(_lat_conv_kernel, seq_len=seq_len),
        grid=(n_req,),
        in_specs=[
            part(7), part(8), part(9), part(10), part(11),
            const((8, W_GROUP)), const((32, W_GROUP)), const((1, W_GROUP)), const((1, W_GROUP)),
        ],
        out_specs=pl.BlockSpec((seq_len, 2 * W_GROUP), lambda b: (b, 0)),
        out_shape=jax.ShapeDtypeStruct((n_req * seq_len, 2 * W_GROUP), F32),
        compiler_params=_params("arbitrary"),
        name="latent_convs",
    )(proj, proj, proj, proj, proj, scw, cfw, cfg, cfb)


def _outproj_kernel(*refs, widths):
    n_parts = len(widths)
    part_refs = refs[:n_parts]
    x_ref, g_ref, w_ref, lg_ref, lb_ref, o_ref = refs[n_parts:]
    mix = None
    row = 0
    for p_ref, wd in zip(part_refs, widths):
        term = _dot(p_ref[...].astype(BF16), w_ref[row:row + wd, :])
        mix = term if mix is None else mix + term
        row += wd
    y = ALPHA * x_ref[...] + g_ref[0] * mix
    o_ref[...] = _layernorm_rows(y, lg_ref[...], lb_ref[...])


def _outproj_ln(parts, x, gate, w_out, ln_g, ln_b, rows_per_cond):
    n, d = x.shape
    tb = TOKEN_BLOCK
    bpc = rows_per_cond // tb
    widths = tuple(p.shape[1] for p in parts)
    return pl.pallas_call(
        functools.partial(_outproj_kernel, widths=widths),
        grid=(n // tb,),
        in_specs=[pl.BlockSpec((tb, wd), lambda b: (b, 0)) for wd in widths] + [
            pl.BlockSpec((tb, d), lambda b: (b, 0)),
            pl.BlockSpec((1, 1, d), lambda b: (b // bpc, 0, 0)),
            pl.BlockSpec((d, d), lambda b: (0, 0)),
            pl.BlockSpec((1, d), lambda b: (0, 0)),
            pl.BlockSpec((1, d), lambda b: (0, 0)),
        ],
        out_specs=pl.BlockSpec((tb, d), lambda b: (b, 0)),
        out_shape=jax.ShapeDtypeStruct((n, d), F32),
        compiler_params=_params("arbitrary"),
        name="output_projection_ln",
    )(*parts, x, gate, w_out, ln_g, ln_b)


GATE_DTYPE = jnp.bfloat16


def _pair_words(x):
    bits = pltpu.bitcast(x.astype(F32), jnp.uint32)
    return bits | (bits >> 16)


def _packed_rows(word_row, rows):
    return pltpu.bitcast(jnp.broadcast_to(word_row, (rows // 2, word_row.shape[-1])), GATE_DTYPE)


def _extract_top(s, count):
    rows = []
    for _ in range(count):
        m = jnp.max(s, axis=0, keepdims=True)
        rows.append(m)
        s = jnp.where(s == m, -jnp.inf, s)
    return rows


def _peer_select_kernel(x_ref, sh_ref, sc_ref, wq_ref, keys_ref, hm_ref, a_ref, b_ref, tau_ref):
    hm = (x_ref[...] * (1.0 + sc_ref[0]) + sh_ref[0]).astype(BF16)
    hm_ref[...] = hm
    q = _dot(hm, wq_ref[...]).astype(BF16)
    dk = PEER_NKEYS
    as_gate = lambda v: v.astype(GATE_DTYPE).astype(F32)
    tops = [[None, None] for _ in range(PEER_HEADS)]
    for h in range(PEER_HEADS):
        for p in range(2):
            j = h * 2 + p
            s = _dot_nt(keys_ref[j], q[:, j * dk:(j + 1) * dk])
            top = _extract_top(s, PEER_TOPK)
            e = jnp.where(s >= top[-1], jnp.exp(s - top[0]), 0.0).astype(GATE_DTYPE)
            tops[h][p] = [as_gate(jnp.exp(t - top[0])) for t in top]
            if p == 0:
                a_ref[h] = _pair_words(e)
            else:
                b_ref[h] = e

    stack = lambda rows_per_head: jnp.concatenate(rows_per_head, axis=0)
    a_rank = [stack([tops[h][0][r] for h in range(PEER_HEADS)]) for r in range(PEER_TOPK)]
    b_rank = [stack([tops[h][1][r] for h in range(PEER_HEADS)]) for r in range(PEER_TOPK)]
    rest = [a_rank[r] * b_rank[c] for r in range(PEER_TOPK) for c in range(PEER_TOPK)
            if (r + 1) * (c + 1) <= PEER_TOPK]
    kth = jnp.zeros_like(rest[0])
    taken = jnp.zeros_like(rest[0])
    for _ in range(PEER_TOPK):
        m = functools.reduce(jnp.maximum, rest)
        kth = jnp.where(taken < PEER_TOPK, m, kth)
        taken = taken + functools.reduce(jnp.add, [(v == m).astype(F32) for v in rest])
        rest = [jnp.where(v == m, -1.0, v) for v in rest]
    a_all = jnp.stack(a_rank)[:, None]
    b_all = jnp.stack(b_rank)[None]
    cand = a_all * b_all
    chosen = cand >= kth
    z = jnp.sum(jnp.where(chosen, cand, 0.0), axis=(0, 1))
    rz = 1.0 / z
    scaled = (a_all.astype(GATE_DTYPE) * (b_all * rz).astype(GATE_DTYPE)).astype(F32)
    tau = jnp.min(jnp.where(chosen, scaled, jnp.inf), axis=(0, 1))
    tau_ref[...] = _pair_words(tau.astype(GATE_DTYPE))
    for h in range(PEER_HEADS):
        b_ref[h] = (b_ref[h].astype(F32) * rz[h:h + 1, :]).astype(GATE_DTYPE)


def _peer_select(x, sh, sc, wq, keys, rows_per_cond):
    n, d = x.shape
    tb = TOKEN_BLOCK
    bpc = rows_per_cond // tb
    vec = pl.BlockSpec((1, 1, d), lambda b: (b // bpc, 0, 0))
    grid_hn = lambda: pl.BlockSpec((PEER_HEADS, PEER_NKEYS, tb), lambda b: (0, 0, b))
    return pl.pallas_call(
        _peer_select_kernel,
        grid=(n // tb,),
        in_specs=[
            pl.BlockSpec((tb, d), lambda b: (b, 0)),
            vec, vec,
            pl.BlockSpec(wq.shape, lambda b: (0, 0)),
            pl.BlockSpec(keys.shape, lambda b: (0, 0, 0)),
        ],
        out_specs=[
            pl.BlockSpec((tb, d), lambda b: (b, 0)),
            grid_hn(), grid_hn(),
            pl.BlockSpec((PEER_HEADS, tb), lambda b: (0, b)),
        ],
        out_shape=[
            jax.ShapeDtypeStruct((n, d), BF16),
            jax.ShapeDtypeStruct((PEER_HEADS, PEER_NKEYS, n), jnp.uint32),
            jax.ShapeDtypeStruct((PEER_HEADS, PEER_NKEYS, n), GATE_DTYPE),
            jax.ShapeDtypeStruct((PEER_HEADS, n), jnp.uint32),
        ],
        compiler_params=_params("arbitrary"),
        name="peer_select",
    )(x, sh, sc, wq, keys)


def _gelu(x):
    return 0.5 * x * (1.0 + lax.erf(x * (2.0 ** -0.5)))


def _peer_expert_kernel(hm_ref, a_ref, b_ref, tau_ref, u_ref, v_ref, o_ref, acc_ref, ht_ref, gh_ref,
                        *, n_chunks, tok_block):
    c = pl.program_id(1)

    @pl.when(c == 0)
    def _():
        acc_ref[...] = jnp.zeros_like(acc_ref)

    rows_per_chunk = PEER_EXPERT_CHUNK // PEER_NKEYS
    group = PEER_TOKEN_GROUP
    n_groups = tok_block // group
    v_t = v_ref[...].T

    def pre_activations(g):
        tok = pl.ds(g * group, group)
        ht_ref[:, tok] = _dot_nt(u_ref[...], hm_ref[tok, :])

    def gated_activations(g):
        for sub in range(group // LANES):
            tok = pl.ds(g * group + sub * LANES, LANES)
            for i in range(rows_per_chunk):
                rows = pl.ds(i * PEER_NKEYS, PEER_NKEYS)
                gate = jnp.zeros((PEER_NKEYS, LANES), GATE_DTYPE)
                for h in range(PEER_HEADS):
                    e = _packed_rows(a_ref[h, i:i + 1, tok], PEER_NKEYS) * b_ref[h, :, tok]
                    tau = _packed_rows(tau_ref[h:h + 1, tok], PEER_NKEYS)
                    gate = jnp.where(e >= tau, gate + e, gate)
                act = _gelu(ht_ref[rows, tok]).astype(GATE_DTYPE)
                gh_ref[rows, tok] = (gate * act).astype(BF16)

    def accumulate(g):
        tok = pl.ds(g * group, group)
        acc_ref[:, tok] += _dot(v_t, gh_ref[:, tok])

    for k in range(n_groups + 2):
        if k < n_groups:
            pre_activations(k)
        if 0 <= k - 1 < n_groups:
            gated_activations(k - 1)
        if 0 <= k - 2 < n_groups:
            accumulate(k - 2)

    @pl.when(c == n_chunks - 1)
    def _():
        o_ref[...] = acc_ref[...].T


def _peer_experts(hm, a_t, b_t, tau, u, v):
    n, d = hm.shape
    tb = PEER_TOKEN_BLOCK
    ec = PEER_EXPERT_CHUNK
    n_chunks = N_EXPERTS // ec
    rpc = ec // PEER_NKEYS
    return pl.pallas_call(
        functools.partial(_peer_expert_kernel, n_chunks=n_chunks, tok_block=tb),
        grid=(n // tb, n_chunks),
        in_specs=[
            pl.BlockSpec((tb, d), lambda b, c: (b, 0)),
            pl.BlockSpec((PEER_HEADS, rpc, tb), lambda b, c: (0, c, b)),
            pl.BlockSpec((PEER_HEADS, PEER_NKEYS, tb), lambda b, c: (0, 0, b)),
            pl.BlockSpec((PEER_HEADS, tb), lambda b, c: (0, b)),
            pl.BlockSpec((ec, d), lambda b, c: (c, 0)),
            pl.BlockSpec((ec, d), lambda b, c: (c, 0)),
        ],
        out_specs=pl.BlockSpec((tb, d), lambda b, c: (b, 0)),
        out_shape=jax.ShapeDtypeStruct((n, d), F32),
        scratch_shapes=[
            pltpu.VMEM((d, tb), F32),
            pltpu.VMEM((ec, tb), F32),
            pltpu.VMEM((ec, tb), BF16),
        ],
        compiler_params=_params("arbitrary", "arbitrary"),
        name="peer_experts",
    )(hm, a_t, b_t, tau, u, v)


def _residual_ln_kernel(x_ref, f_ref, g_ref, lg_ref, lb_ref, o_ref):
    y = ALPHA * x_ref[...] + g_ref[0] * f_ref[...]
    o_ref[...] = _layernorm_rows(y, lg_ref[...], lb_ref[...])


def _residual_ln(x, f, gate, ln_g, ln_b, rows_per_cond):
    n, d = x.shape
    tb = TOKEN_BLOCK
    bpc = rows_per_cond // tb
    row = pl.BlockSpec((tb, d), lambda b: (b, 0))
    return pl.pallas_call(
        _residual_ln_kernel,
        grid=(n // tb,),
        in_specs=[row, row,
                  pl.BlockSpec((1, 1, d), lambda b: (b // bpc, 0, 0)),
                  pl.BlockSpec((1, d), lambda b: (0, 0)),
                  pl.BlockSpec((1, d), lambda b: (0, 0))],
        out_specs=row,
        out_shape=jax.ShapeDtypeStruct((n, d), F32),
        compiler_params=_params("arbitrary"),
        name="residual_ln",
    )(x, f, gate, ln_g, ln_b)


def _rope_tables(seq_len):
    t = np.arange(seq_len)
    row = (t // GRID_W).astype(np.float32)
    col = (t % GRID_W).astype(np.float32)
    n_freq = HEAD_DIM // 4
    inv = (ROPE_BASE ** (-np.arange(n_freq, dtype=np.float32) / n_freq)).astype(np.float32)
    ang = jnp.asarray(np.concatenate([row[:, None] * inv, col[:, None] * inv], -1).astype(np.float32))
    cos, sin = jnp.cos(ang), jnp.sin(ang)
    cos_t = jnp.tile(jnp.concatenate([cos, cos], -1), (1, H_GROUP))
    sin_t = jnp.tile(jnp.concatenate([-sin, sin], -1), (1, H_GROUP))
    return cos_t, sin_t


def _swap_halves_perm():
    idx = np.arange(W_GROUP)
    half = HEAD_DIM // 2
    return np.where((idx % HEAD_DIM) < half, idx + half, idx - half)


def _attention_bias(rpb, rows):
    wr = min(NA_WR, rows)
    cols = np.arange(GRID_W)
    c_start = np.clip(cols - NA_WC // 2, 0, GRID_W - NA_WC)
    col_in = (cols[None, :] >= c_start[:, None]) & (cols[None, :] < c_start[:, None] + NA_WC)
    col_idx = np.clip(cols[None, :] - cols[:, None], 1 - NA_WC, NA_WC - 1) + NA_WC - 1
    rpb_cols = rpb[:, :, col_idx]
    masked = jnp.where(jnp.asarray(col_in)[None, None], rpb_cols, NEG_BIG)
    tables = []
    for r in range(rows):
        rs = int(np.clip(r - wr // 2, 0, rows - wr))
        d0 = rs - r + NA_WR - 1
        blk = masked[:, d0:d0 + wr]
        tables.append(blk.transpose(0, 2, 1, 3).reshape(H_GROUP * GRID_W, wr * GRID_W))
    return jnp.stack(tables, axis=0)


def _block_diag_states(s0):
    b = s0.shape[0]
    eye = jnp.eye(H_GROUP, dtype=s0.dtype)
    full = jnp.einsum('bchde,hg->bchdge', s0, eye)
    return full.reshape(b, 2, H_GROUP * HEAD_DIM, H_GROUP * HEAD_DIM)


def kernel(x_prompt, x_sample, c, cache_na_k, cache_na_v, state_ret, c_ctx, w_mod, b_mod, w_in, w_out,
           na_rpb, ret_decay, sc_w, cf_w, cf_ln_g, cf_ln_b, ln1_g, ln1_b, ln2_g, ln2_b,
           peer_wq, peer_keys, peer_u, peer_v):
    n_req, seq, d = x_prompt.shape
    n_lat, lat_seq, _ = x_sample.shape
    past = cache_na_k.shape[2]
    rows = lat_seq // GRID_W

    cond8 = jnp.concatenate([c_ctx[None], c, jnp.zeros((8 - 1 - n_lat, d), F32)], axis=0)
    mod = _modulation(cond8, w_mod, b_mod)

    def mod_vec(l, j, lo, hi):
        return mod[l, lo:hi, j * d:(j + 1) * d][:, None, :]

    perm = _swap_halves_perm()
    w_in_b = w_in.astype(BF16)
    w_in_lat = jnp.concatenate([w_in_b, w_in_b[:, :, 3 * W_GROUP + perm], w_in_b[:, :, 4 * W_GROUP + perm]], -1)
    w_out_b = w_out.astype(BF16)
    wq_b = peer_wq.astype(BF16)
    keys_b = peer_keys.astype(BF16).reshape(DEPTH, PEER_HEADS * 2, PEER_NKEYS, PEER_NKEYS)
    u_b = peer_u.astype(BF16)
    v_b = peer_v.astype(BF16)

    lane_h = np.arange(W_GROUP) // HEAD_DIM
    avg = jnp.asarray((lane_h[:, None] == lane_h[None, :]).astype(np.float32) / HEAD_DIM).astype(BF16)
    cos_t, sin_t = _rope_tables(lat_seq)
    dec_tiles = jnp.broadcast_to(ret_decay.reshape(DEPTH, 2 * H_GROUP, 1), (DEPTH, 8, LANES))
    scw = jnp.pad(sc_w, ((0, 0), (0, 8 - SC_WIDTH), (0, 0)))
    cfw = jnp.pad(cf_w, ((0, 0), (0, 32 - CF_WIDTH), (0, 0)))
    kc_all = cache_na_k.reshape(n_lat, DEPTH, past, W_GROUP)
    vc_all = cache_na_v.reshape(n_lat, DEPTH, past, W_GROUP)

    xp = x_prompt.reshape(n_req * seq, d)
    xs = x_sample.reshape(n_lat * lat_seq, d)
    ks_out, vs_out, ss_out = [], [], []

    for l in range(DEPTH):
        row1 = lambda a: a[l][None, :]
        cfg, cfb = row1(cf_ln_g), row1(cf_ln_b)
        l1g, l1b, l2g, l2b = row1(ln1_g), row1(ln1_b), row1(ln2_g), row1(ln2_b)

        n_ctx = n_req * seq
        proj = _modmm(xp, mod_vec(l, 0, 0, 1), mod_vec(l, 1, 0, 1), w_in_b[l], n_ctx)
        mix, k_ctx, v_ctx, s_ctx = _ctx_mixers(proj, n_req, seq, dec_tiles[l], avg, scw[l], cfw[l], cfg, cfb)
        xp = _outproj_ln([mix], xp, mod_vec(l, 2, 0, 1), w_out_b[l], l1g, l1b, n_ctx)
        hm, a_t, b_t, tau = _peer_select(xp, mod_vec(l, 3, 0, 1), mod_vec(l, 4, 0, 1), wq_b[l], keys_b[l], n_ctx)
        ffn = _peer_experts(hm, a_t, b_t, tau, u_b[l], v_b[l])
        xp = _residual_ln(xp, ffn, mod_vec(l, 5, 0, 1), l2g, l2b, n_ctx)
        ks_out.append(k_ctx.reshape(n_req, seq, H_GROUP, HEAD_DIM))
        vs_out.append(v_ctx.reshape(n_req, seq, H_GROUP, HEAD_DIM))
        ss_out.append(s_ctx)

        hi = 1 + n_lat
        proj = _modmm(xs, mod_vec(l, 0, 1, hi), mod_vec(l, 1, 1, hi), w_in_lat[l], lat_seq)
        bias = _attention_bias(na_rpb[l], rows)
        o_a = _lat_attention(proj, n_lat, lat_seq, kc_all[:, l], vc_all[:, l], bias)
        o_b = _lat_retention(proj, n_lat, lat_seq, cos_t, sin_t, _block_diag_states(state_ret[:, l]),
                             dec_tiles[l], avg)
        o_cd = _lat_convs(proj, n_lat, lat_seq, scw[l], cfw[l], cfg, cfb)
        xs = _outproj_ln([o_a, o_b, o_cd], xs, mod_vec(l, 2, 1, hi), w_out_b[l], l1g, l1b, lat_seq)
        hm, a_t, b_t, tau = _peer_select(xs, mod_vec(l, 3, 1, hi), mod_vec(l, 4, 1, hi), wq_b[l], keys_b[l], lat_seq)
        ffn = _peer_experts(hm, a_t, b_t, tau, u_b[l], v_b[l])
        xs = _residual_ln(xs, ffn, mod_vec(l, 5, 1, hi), l2g, l2b, lat_seq)

    return (xp.reshape(n_req, seq, d), xs.reshape(n_lat, lat_seq, d),
            jnp.stack(ks_out, axis=1), jnp.stack(vs_out, axis=1), jnp.stack(ss_out, axis=1))
```

```python
import functools
import math

import jax
import jax.numpy as jnp
import numpy as np
from jax import lax
from jax.experimental import pallas as pl
from jax.experimental.pallas import tpu as pltpu

F32 = jnp.float32
BF16 = jnp.bfloat16

D_MODEL = 1024
DEPTH = 4
GRID_W = 64
HEAD_DIM = 64
W_GROUP = 256
H_GROUP = 4
N_IN_PARTS = 12
NA_WR = 8
NA_WC = 16
ROPE_BASE = 10000.0
SC_WIDTH = 3
CF_WIDTH = 31
PEER_HEADS = 8
PEER_NKEYS = 128
PEER_TOPK = 16
N_EXPERTS = PEER_NKEYS * PEER_NKEYS
LN_EPS = 1e-5
ALPHA = (2.0 * DEPTH) ** 0.25
NEG_BIG = -1e30

VMEM_LIMIT_BYTES = 56 * 1024 * 1024
LANES = 128

TOKEN_BLOCK = 512
PEER_TOKEN_BLOCK = 1024
PEER_TOKEN_GROUP = 256
PEER_EXPERT_CHUNK = 1024


def _params(*sem):
    return pltpu.CompilerParams(dimension_semantics=sem, vmem_limit_bytes=VMEM_LIMIT_BYTES)


def _sigmoid(x):
    return 1.0 / (1.0 + jnp.exp(-x))


def _silu(x):
    return x * _sigmoid(x)


def _layernorm_rows(x, g, b):
    mu = jnp.mean(x, axis=-1, keepdims=True)
    d = x - mu
    var = jnp.mean(d * d, axis=-1, keepdims=True)
    return d * lax.rsqrt(var + LN_EPS) * g + b


def _dot(a, b):
    return jnp.dot(a, b, preferred_element_type=F32)


def _dot_nt(a, b):
    return lax.dot_general(a, b, (((1,), (1,)), ((), ())), preferred_element_type=F32)


def _dot_f32(a, b):
    hi = a.astype(BF16)
    lo = (a - hi.astype(F32)).astype(BF16)
    return _dot(hi, b) + _dot(lo, b)


def _head_masks():
    lane = lax.broadcasted_iota(jnp.int32, (1, W_GROUP), 1)
    return [((lane >= h * HEAD_DIM) & (lane < (h + 1) * HEAD_DIM)).astype(F32) for h in range(H_GROUP)]


def _mod_kernel(c_ref, w_ref, b_ref, o_ref):
    c = c_ref[...]
    s = _silu(c)
    o_ref[0] = jnp.dot(s, w_ref[0], precision=lax.Precision.HIGHEST,
                       preferred_element_type=F32) + b_ref[0]


def _modulation(cond8, w_mod, b_mod):
    nc = 1024
    n_out = w_mod.shape[-1]
    return pl.pallas_call(
        _mod_kernel,
        grid=(DEPTH, n_out // nc),
        in_specs=[
            pl.BlockSpec((8, D_MODEL), lambda l, j: (0, 0)),
            pl.BlockSpec((1, D_MODEL, nc), lambda l, j: (l, 0, j)),
            pl.BlockSpec((1, 1, nc), lambda l, j: (l, 0, j)),
        ],
        out_specs=pl.BlockSpec((1, 8, nc), lambda l, j: (l, 0, j)),
        out_shape=jax.ShapeDtypeStruct((DEPTH, 8, n_out), F32),
        compiler_params=_params("arbitrary", "arbitrary"),
        name="modulation",
    )(cond8, w_mod, b_mod.reshape(DEPTH, 1, n_out))


def _modmm_kernel(x_ref, sh_ref, sc_ref, w_ref, o_ref):
    h = x_ref[...] * (1.0 + sc_ref[0]) + sh_ref[0]
    o_ref[...] = _dot(h.astype(BF16), w_ref[...])


def _modmm(x, sh, sc, w, rows_per_cond):
    n, d = x.shape
    n_out = w.shape[1]
    tb = TOKEN_BLOCK
    bpc = rows_per_cond // tb
    vec = pl.BlockSpec((1, 1, d), lambda b: (b // bpc, 0, 0))
    return pl.pallas_call(
        _modmm_kernel,
        grid=(n // tb,),
        in_specs=[
            pl.BlockSpec((tb, d), lambda b: (b, 0)),
            vec, vec,
            pl.BlockSpec((d, n_out), lambda b: (0, 0)),
        ],
        out_specs=pl.BlockSpec((tb, n_out), lambda b: (b, 0)),
        out_shape=jax.ShapeDtypeStruct((n, n_out), F32),
        compiler_params=_params("arbitrary"),
        name="modulated_projection",
    )(x, sh, sc, w)


def _log_gamma_tile(decay_tile):
    y = -decay_tile
    return -(jnp.maximum(y, 0.0) + jnp.log1p(jnp.exp(-jnp.abs(y))))


def _lane_vector(lg_tile, row0, masks):
    out = lg_tile[row0:row0 + 1, 0:1] * masks[0]
    for h in range(1, H_GROUP):
        out = out + lg_tile[row0 + h:row0 + h + 1, 0:1] * masks[h]
    return out


def _retention_pairs(rq, rk, rv, lg_tile, masks, seq_len, q_block):
    kb = rk.astype(BF16)
    vb = rv.astype(BF16)
    blocks = []
    for q0 in range(0, seq_len, q_block):
        rqb = rq[q0:q0 + q_block]
        ti = lax.broadcasted_iota(jnp.int32, (q_block, seq_len), 0) + q0
        si = lax.broadcasted_iota(jnp.int32, (q_block, seq_len), 1)
        dist = (ti - si).astype(F32)
        fwd = jnp.maximum(dist, 0.0)
        bwd = jnp.maximum(-dist, 0.0)
        acc = jnp.zeros((q_block, W_GROUP), F32)
        for h in range(H_GROUP):
            lgf = lg_tile[h:h + 1, 0:1]
            lgb = lg_tile[H_GROUP + h:H_GROUP + h + 1, 0:1]
            decay = (jnp.where(dist >= 0, jnp.exp(lgf * fwd), 0.0)
                     + jnp.where(dist <= 0, jnp.exp(lgb * bwd), 0.0))
            s = _dot_nt((rqb * masks[h]).astype(BF16), kb)
            acc = acc + _dot((s * decay).astype(BF16), vb) * masks[h]
        blocks.append(acc)
    return blocks[0] if len(blocks) == 1 else jnp.concatenate(blocks, axis=0)


def _retention_finish(o, r_g, avg):
    mu = _dot_f32(o, avg)
    d = o - mu
    var = _dot_f32(d * d, avg)
    return _silu(r_g) * (d * lax.rsqrt(var + LN_EPS))


def _shift_rows(z, off, seq_len):
    if off == 0:
        return z
    rolled = pltpu.roll(z, (-off) % seq_len, 0)
    t = lax.broadcasted_iota(jnp.int32, z.shape, 0)
    valid = (t + off >= 0) & (t + off < seq_len)
    return jnp.where(valid, rolled, 0.0)


def _dwconv(z, w_ref, width, seq_len):
    pad = (width - 1) // 2
    acc = None
    for k in range(width):
        term = _shift_rows(z, k - pad, seq_len) * w_ref[k:k + 1, :]
        acc = term if acc is None else acc + term
    return acc


def _conv_mixers(sc_b, sc_c, sc_x, cf_a, cf_gate, scw_ref, cfw_ref, cfg_ref, cfb_ref, seq_len):
    o_c = sc_b * _dwconv(sc_c * sc_x, scw_ref, SC_WIDTH, seq_len)
    u = _dwconv(cf_a * _sigmoid(cf_gate), cfw_ref, CF_WIDTH, seq_len)
    o_d = _silu(_layernorm_rows(u, cfg_ref[...], cfb_ref[...]))
    return o_c, o_d


def _ctx_mixer_kernel(p_ref, dec_ref, avg_ref, scw_ref, cfw_ref, cfg_ref, cfb_ref,
                      mix_ref, k_ref, v_ref, st_ref, *, seq_len):
    masks = _head_masks()
    part = lambda j: p_ref[:, j * W_GROUP:(j + 1) * W_GROUP]
    na_q, na_k, na_v = part(0), part(1), part(2)
    k_ref[...] = na_k
    v_ref[...] = na_v

    kb = na_k.astype(BF16)
    vb = na_v.astype(BF16)
    qs = na_q * (HEAD_DIM ** -0.5)
    o_a = jnp.zeros((seq_len, W_GROUP), F32)
    for h in range(H_GROUP):
        s = _dot_nt((qs * masks[h]).astype(BF16), kb)
        m = jnp.max(s, axis=-1, keepdims=True)
        e = jnp.exp(s - m)
        p = e / jnp.sum(e, axis=-1, keepdims=True)
        o_a = o_a + _dot(p.astype(BF16), vb) * masks[h]
    mix_ref[:, 0:W_GROUP] = o_a

    lg = _log_gamma_tile(dec_ref[...])
    rq, rv, r_g = part(3), part(5), part(6)
    rk = part(4) * (HEAD_DIM ** -0.5)
    o = _retention_pairs(rq, rk, rv, lg, masks, seq_len, seq_len)
    mix_ref[:, W_GROUP:2 * W_GROUP] = _retention_finish(o, r_g, avg_ref[...])

    pos = lax.broadcasted_iota(jnp.int32, (seq_len, W_GROUP), 0).astype(F32)
    vb_r = rv.astype(BF16)
    for d in range(2):
        lane_lg = _lane_vector(lg, d * H_GROUP, masks)
        expo = (seq_len - 1.0 - pos) if d == 0 else pos
        kd = rk * jnp.exp(lane_lg * expo)
        full = _dot(kd.T.astype(BF16), vb_r)
        for h in range(H_GROUP):
            st_ref[0, d, h] = full[h * HEAD_DIM:(h + 1) * HEAD_DIM, h * HEAD_DIM:(h + 1) * HEAD_DIM]

    o_c, o_d = _conv_mixers(part(7), part(8), part(9), part(10), part(11),
                            scw_ref, cfw_ref, cfg_ref, cfb_ref, seq_len)
    mix_ref[:, 2 * W_GROUP:3 * W_GROUP] = o_c
    mix_ref[:, 3 * W_GROUP:4 * W_GROUP] = o_d


def _ctx_mixers(proj, n_req, seq_len, dec_tile, avg, scw, cfw, cfg, cfb):
    n = n_req * seq_len
    const = lambda shape: pl.BlockSpec(shape, lambda b: (0,) * len(shape))
    return pl.pallas_call(
        functools.partial(_ctx_mixer_kernel, seq_len=seq_len),
        grid=(n_req,),
        in_specs=[
            pl.BlockSpec((seq_len, N_IN_PARTS * W_GROUP), lambda b: (b, 0)),
            const((8, LANES)), const((W_GROUP, W_GROUP)), const((8, W_GROUP)),
            const((32, W_GROUP)), const((1, W_GROUP)), const((1, W_GROUP)),
        ],
        out_specs=[
            pl.BlockSpec((seq_len, D_MODEL), lambda b: (b, 0)),
            pl.BlockSpec((seq_len, W_GROUP), lambda b: (b, 0)),
            pl.BlockSpec((seq_len, W_GROUP), lambda b: (b, 0)),
            pl.BlockSpec((1, 2, H_GROUP, HEAD_DIM, HEAD_DIM), lambda b: (b, 0, 0, 0, 0)),
        ],
        out_shape=[
            jax.ShapeDtypeStruct((n, D_MODEL), F32),
            jax.ShapeDtypeStruct((n, W_GROUP), F32),
            jax.ShapeDtypeStruct((n, W_GROUP), F32),
            jax.ShapeDtypeStruct((n_req, 2, H_GROUP, HEAD_DIM, HEAD_DIM), F32),
        ],
        compiler_params=_params("arbitrary"),
        name="context_mixers",
    )(proj, dec_tile, avg, scw, cfw, cfg, cfb)


def _lat_attn_kernel(q_ref, k_ref, v_ref, kc_ref, vc_ref, bias_ref, o_ref, *, rows):
    masks = _head_masks()
    kcb = kc_ref[0].astype(BF16)
    vcb = vc_ref[0].astype(BF16)
    wr = min(NA_WR, rows)

    def row_block(r, carry):
        rs = jnp.clip(r - wr // 2, 0, rows - wr)
        q0 = pl.multiple_of(r * GRID_W, GRID_W)
        k0 = pl.multiple_of(rs * GRID_W, GRID_W)
        qb = q_ref[pl.ds(q0, GRID_W), :] * (HEAD_DIM ** -0.5)
        qs = jnp.concatenate([qb * masks[h] for h in range(H_GROUP)], axis=0).astype(BF16)
        kl = k_ref[pl.ds(k0, wr * GRID_W), :].astype(BF16)
        vl = v_ref[pl.ds(k0, wr * GRID_W), :].astype(BF16)
        s_loc = _dot_nt(qs, kl) + bias_ref[r]
        s_ctx = _dot_nt(qs, kcb)
        m = jnp.maximum(jnp.max(s_loc, axis=-1, keepdims=True), jnp.max(s_ctx, axis=-1, keepdims=True))
        e_loc = jnp.exp(s_loc - m)
        e_ctx = jnp.exp(s_ctx - m)
        z = jnp.sum(e_loc, axis=-1, keepdims=True) + jnp.sum(e_ctx, axis=-1, keepdims=True)
        o = (_dot(e_loc.astype(BF16), vl) + _dot(e_ctx.astype(BF16), vcb)) / z
        acc = o[0:GRID_W] * masks[0]
        for h in range(1, H_GROUP):
            acc = acc + o[h * GRID_W:(h + 1) * GRID_W] * masks[h]
        o_ref[pl.ds(q0, GRID_W), :] = acc
        return carry

    lax.fori_loop(0, rows, row_block, 0)


def _lat_attention(proj, n_req, seq_len, kc, vc, bias):
    rows = seq_len // GRID_W
    part = lambda j: pl.BlockSpec((seq_len, W_GROUP), lambda b, j=j: (b, j))
    past = kc.shape[1]
    return pl.pallas_call(
        functools.partial(_lat_attn_kernel, rows=rows),
        grid=(n_req,),
        in_specs=[
            part(0), part(1), part(2),
            pl.BlockSpec((1, past, W_GROUP), lambda b: (b, 0, 0)),
            pl.BlockSpec((1, past, W_GROUP), lambda b: (b, 0, 0)),
            pl.BlockSpec(bias.shape, lambda b: (0, 0, 0)),
        ],
        out_specs=pl.BlockSpec((seq_len, W_GROUP), lambda b: (b, 0)),
        out_shape=jax.ShapeDtypeStruct((n_req * seq_len, W_GROUP), F32),
        compiler_params=_params("arbitrary"),
        name="latent_attention",
    )(proj, proj, proj, kc, vc, bias)


def _lat_ret_kernel(q_ref, k_ref, v_ref, g_ref, qs_ref, ks_ref, cos_ref, sin_ref, s0_ref,
                    dec_ref, avg_ref, o_ref, *, seq_len):
    masks = _head_masks()
    lg = _log_gamma_tile(dec_ref[...])
    cos = cos_ref[...]
    sin = sin_ref[...]
    rq = q_ref[...] * cos + qs_ref[...] * sin
    rk = (k_ref[...] * cos + ks_ref[...] * sin) * (HEAD_DIM ** -0.5)
    rv = v_ref[...]
    o = _retention_pairs(rq, rk, rv, lg, masks, seq_len, 256)
    pos = lax.broadcasted_iota(jnp.int32, (seq_len, W_GROUP), 0).astype(F32)
    rqb = rq.astype(BF16)
    lgf = _lane_vector(lg, 0, masks)
    lgb = _lane_vector(lg, H_GROUP, masks)
    o = o + _dot(rqb, s0_ref[0, 0].astype(BF16)) * jnp.exp(lgf * (pos + 1.0))
    o = o + _dot(rqb, s0_ref[0, 1].astype(BF16)) * jnp.exp(lgb * (seq_len - pos))
    o_ref[...] = _retention_finish(o, g_ref[...], avg_ref[...])


def _lat_retention(proj, n_req, seq_len, cos, sin, s0_bd, dec_tile, avg):
    part = lambda j: pl.BlockSpec((seq_len, W_GROUP), lambda b, j=j: (b, j))
    const = lambda shape: pl.BlockSpec(shape, lambda b: (0,) * len(shape))
    return pl.pallas_call(
        functools.partial(_lat_ret_kernel, seq_len=seq_len),
        grid=(n_req,),
        in_specs=[
            part(3), part(4), part(5), part(6), part(12), part(13),
            const((seq_len, W_GROUP)), const((seq_len, W_GROUP)),
            pl.BlockSpec((1, 2, W_GROUP, W_GROUP), lambda b: (b, 0, 0, 0)),
            const((8, LANES)), const((W_GROUP, W_GROUP)),
        ],
        out_specs=pl.BlockSpec((seq_len, W_GROUP), lambda b: (b, 0)),
        out_shape=jax.ShapeDtypeStruct((n_req * seq_len, W_GROUP), F32),
        compiler_params=_params("arbitrary"),
        name="latent_retention",
    )(proj, proj, proj, proj, proj, proj, cos, sin, s0_bd, dec_tile, avg)


def _lat_conv_kernel(b_ref, c_ref, x_ref, a_ref, gate_ref, scw_ref, cfw_ref, cfg_ref, cfb_ref,
                     o_ref, *, seq_len):
    o_c, o_d = _conv_mixers(b_ref[...], c_ref[...], x_ref[...], a_ref[...], gate_ref[...],
                            scw_ref, cfw_ref, cfg_ref, cfb_ref, seq_len)
    o_ref[:, 0:W_GROUP] = o_c
    o_ref[:, W_GROUP:2 * W_GROUP] = o_d


def _lat_convs(proj, n_req, seq_len, scw, cfw, cfg, cfb):
    part = lambda j: pl.BlockSpec((seq_len, W_GROUP), lambda b, j=j: (b, j))
    const = lambda shape: pl.BlockSpec(shape, lambda b: (0,) * len(shape))
    return pl.pallas_call(
        functools.partial(_lat_conv_kernel, seq_len=seq_len),
        grid=(n_req,),
        in_specs=[
            part(7), part(8), part(9), part(10), part(11),
            const((8, W_GROUP)), const((32, W_GROUP)), const((1, W_GROUP)), const((1, W_GROUP)),
        ],
        out_specs=pl.BlockSpec((seq_len, 2 * W_GROUP), lambda b: (b, 0)),
        out_shape=jax.ShapeDtypeStruct((n_req * seq_len, 2 * W_GROUP), F32),
        compiler_params=_params("arbitrary"),
        name="latent_convs",
    )(proj, proj, proj, proj, proj, scw, cfw, cfg, cfb)


def _outproj_kernel(*refs, widths):
    n_parts = len(widths)
    part_refs = refs[:n_parts]
    x_ref, g_ref, w_ref, lg_ref, lb_ref, o_ref = refs[n_parts:]
    mix = None
    row = 0
    for p_ref, wd in zip(part_refs, widths):
        term = _dot(p_ref[...].astype(BF16), w_ref[row:row + wd, :])
        mix = term if mix is None else mix + term
        row += wd
    y = ALPHA * x_ref[...] + g_ref[0] * mix
    o_ref[...] = _layernorm_rows(y, lg_ref[...], lb_ref[...])


def _outproj_ln(parts, x, gate, w_out, ln_g, ln_b, rows_per_cond):
    n, d = x.shape
    tb = TOKEN_BLOCK
    bpc = rows_per_cond // tb
    widths = tuple(p.shape[1] for p in parts)
    return pl.pallas_call(
        functools.partial(_outproj_kernel, widths=widths),
        grid=(n // tb,),
        in_specs=[pl.BlockSpec((tb, wd), lambda b: (b, 0)) for wd in widths] + [
            pl.BlockSpec((tb, d), lambda b: (b, 0)),
            pl.BlockSpec((1, 1, d), lambda b: (b // bpc, 0, 0)),
            pl.BlockSpec((d, d), lambda b: (0, 0)),
            pl.BlockSpec((1, d), lambda b: (0, 0)),
            pl.BlockSpec((1, d), lambda b: (0, 0)),
        ],
        out_specs=pl.BlockSpec((tb, d), lambda b: (b, 0)),
        out_shape=jax.ShapeDtypeStruct((n, d), F32),
        compiler_params=_params("arbitrary"),
        name="output_projection_ln",
    )(*parts, x, gate, w_out, ln_g, ln_b)


GATE_DTYPE = jnp.bfloat16


def _pair_words(x):
    bits = pltpu.bitcast(x.astype(F32), jnp.uint32)
    return bits | (bits >> 16)


def _packed_rows(word_row, rows):
    return pltpu.bitcast(jnp.broadcast_to(word_row, (rows // 2, word_row.shape[-1])), GATE_DTYPE)


def _extract_top(s, count):
    rows = []
    for _ in range(count):
        m = jnp.max(s, axis=0, keepdims=True)
        rows.append(m)
        s = jnp.where(s == m, -jnp.inf, s)
    return rows


def _peer_select_kernel(x_ref, sh_ref, sc_ref, wq_ref, keys_ref, hm_ref, a_ref, b_ref, tau_ref):
    hm = (x_ref[...] * (1.0 + sc_ref[0]) + sh_ref[0]).astype(BF16)
    hm_ref[...] = hm
    q = _dot(hm, wq_ref[...]).astype(BF16)
    dk = PEER_NKEYS
    as_gate = lambda v: v.astype(GATE_DTYPE).astype(F32)
    tops = [[None, None] for _ in range(PEER_HEADS)]
    for h in range(PEER_HEADS):
        for p in range(2):
            j = h * 2 + p
            s = _dot_nt(keys_ref[j], q[:, j * dk:(j + 1) * dk])
            top = _extract_top(s, PEER_TOPK)
            e = jnp.where(s >= top[-1], jnp.exp(s - top[0]), 0.0).astype(GATE_DTYPE)
            tops[h][p] = [as_gate(jnp.exp(t - top[0])) for t in top]
            if p == 0:
                a_ref[h] = _pair_words(e)
            else:
                b_ref[h] = e

    stack = lambda rows_per_head: jnp.concatenate(rows_per_head, axis=0)
    a_rank = [stack([tops[h][0][r] for h in range(PEER_HEADS)]) for r in range(PEER_TOPK)]
    b_rank = [stack([tops[h][1][r] for h in range(PEER_HEADS)]) for r in range(PEER_TOPK)]
    rest = [a_rank[r] * b_rank[c] for r in range(PEER_TOPK) for c in range(PEER_TOPK)
            if (r + 1) * (c + 1) <= PEER_TOPK]
    kth = jnp.zeros_like(rest[0])
    taken = jnp.zeros_like(rest[0])
    for _ in range(PEER_TOPK):
        m = functools.reduce(jnp.maximum, rest)
        kth = jnp.where(taken < PEER_TOPK, m, kth)
        taken = taken + functools.reduce(jnp.add, [(v == m).astype(F32) for v in rest])
        rest = [jnp.where(v == m, -1.0, v) for v in rest]
    a_all = jnp.stack(a_rank)[:, None]
    b_all = jnp.stack(b_rank)[None]
    cand = a_all * b_all
    chosen = cand >= kth
    z = jnp.sum(jnp.where(chosen, cand, 0.0), axis=(0, 1))
    rz = 1.0 / z
    scaled = (a_all.astype(GATE_DTYPE) * (b_all * rz).astype(GATE_DTYPE)).astype(F32)
    tau = jnp.min(jnp.where(chosen, scaled, jnp.inf), axis=(0, 1))
    tau_ref[...] = _pair_words(tau.astype(GATE_DTYPE))
    for h in range(PEER_HEADS):
        b_ref[h] = (b_ref[h].astype(F32) * rz[h:h + 1, :]).astype(GATE_DTYPE)


def _peer_select(x, sh, sc, wq, keys, rows_per_cond):
    n, d = x.shape
    tb = TOKEN_BLOCK
    bpc = rows_per_cond // tb
    vec = pl.BlockSpec((1, 1, d), lambda b: (b // bpc, 0, 0))
    grid_hn = lambda: pl.BlockSpec((PEER_HEADS, PEER_NKEYS, tb), lambda b: (0, 0, b))
    return pl.pallas_call(
        _peer_select_kernel,
        grid=(n // tb,),
        in_specs=[
            pl.BlockSpec((tb, d), lambda b: (b, 0)),
            vec, vec,
            pl.BlockSpec(wq.shape, lambda b: (0, 0)),
            pl.BlockSpec(keys.shape, lambda b: (0, 0, 0)),
        ],
        out_specs=[
            pl.BlockSpec((tb, d), lambda b: (b, 0)),
            grid_hn(), grid_hn(),
            pl.BlockSpec((PEER_HEADS, tb), lambda b: (0, b)),
        ],
        out_shape=[
            jax.ShapeDtypeStruct((n, d), BF16),
            jax.ShapeDtypeStruct((PEER_HEADS, PEER_NKEYS, n), jnp.uint32),
            jax.ShapeDtypeStruct((PEER_HEADS, PEER_NKEYS, n), GATE_DTYPE),
            jax.ShapeDtypeStruct((PEER_HEADS, n), jnp.uint32),
        ],
        compiler_params=_params("arbitrary"),
        name="peer_select",
    )(x, sh, sc, wq, keys)


def _gelu(x):
    return 0.5 * x * (1.0 + lax.erf(x * (2.0 ** -0.5)))


def _peer_expert_kernel(hm_ref, a_ref, b_ref, tau_ref, u_ref, vt_ref, o_ref, acc_ref, ht_ref, gh_ref,
                        *, n_chunks, tok_block):
    c = pl.program_id(1)

    @pl.when(c == 0)
    def _():
        acc_ref[...] = jnp.zeros_like(acc_ref)

    rows_per_chunk = PEER_EXPERT_CHUNK // PEER_NKEYS
    group = PEER_TOKEN_GROUP
    n_groups = tok_block // group
    v_t = vt_ref[0].T

    def pre_activations(g):
        tok = pl.ds(g * group, group)
        ht_ref[:, tok] = _dot_nt(u_ref[0], hm_ref[tok, :])

    def gated_activations(g):
        for sub in range(group // LANES):
            tok = pl.ds(g * group + sub * LANES, LANES)
            for i in range(rows_per_chunk):
                rows = pl.ds(i * PEER_NKEYS, PEER_NKEYS)
                gate = jnp.zeros((PEER_NKEYS, LANES), GATE_DTYPE)
                for h in range(PEER_HEADS):
                    e = _packed_rows(a_ref[h, i:i + 1, tok], PEER_NKEYS) * b_ref[h, :, tok]
                    tau = _packed_rows(tau_ref[h:h + 1, tok], PEER_NKEYS)
                    gate = jnp.where(e >= tau, gate + e, gate)
                act = _gelu(ht_ref[rows, tok]).astype(GATE_DTYPE)
                gh_ref[rows, tok] = (gate * act).astype(BF16)

    def accumulate(g):
        tok = pl.ds(g * group, group)
        acc_ref[:, tok] += _dot(v_t, gh_ref[:, tok])

    for k in range(n_groups + 2):
        if k < n_groups:
            pre_activations(k)
        if 0 <= k - 1 < n_groups:
            gated_activations(k - 1)
        if 0 <= k - 2 < n_groups:
            accumulate(k - 2)

    @pl.when(c == n_chunks - 1)
    def _():
        o_ref[...] = acc_ref[...].T


def _peer_experts(hm, a_t, b_t, tau, u_all, vt_all, layer):
    n, d = hm.shape
    tb = PEER_TOKEN_BLOCK
    ec = PEER_EXPERT_CHUNK
    n_chunks = N_EXPERTS // ec
    rpc = ec // PEER_NKEYS
    return pl.pallas_call(
        functools.partial(_peer_expert_kernel, n_chunks=n_chunks, tok_block=tb),
        grid=(n // tb, n_chunks),
        in_specs=[
            pl.BlockSpec((tb, d), lambda b, c: (b, 0)),
            pl.BlockSpec((PEER_HEADS, rpc, tb), lambda b, c: (0, c, b)),
            pl.BlockSpec((PEER_HEADS, PEER_NKEYS, tb), lambda b, c: (0, 0, b)),
            pl.BlockSpec((PEER_HEADS, tb), lambda b, c: (0, b)),
            pl.BlockSpec((1, ec, d), lambda b, c: (layer, c, 0)),
            pl.BlockSpec((1, ec, d), lambda b, c: (layer, c, 0)),
        ],
        out_specs=pl.BlockSpec((tb, d), lambda b, c: (b, 0)),
        out_shape=jax.ShapeDtypeStruct((n, d), F32),
        scratch_shapes=[
            pltpu.VMEM((d, tb), F32),
            pltpu.VMEM((ec, tb), F32),
            pltpu.VMEM((ec, tb), BF16),
        ],
        compiler_params=_params("arbitrary", "arbitrary"),
        name="peer_experts",
    )(hm, a_t, b_t, tau, u_all, vt_all)


def _residual_ln_kernel(x_ref, f_ref, g_ref, lg_ref, lb_ref, o_ref):
    y = ALPHA * x_ref[...] + g_ref[0] * f_ref[...]
    o_ref[...] = _layernorm_rows(y, lg_ref[...], lb_ref[...])


def _residual_ln(x, f, gate, ln_g, ln_b, rows_per_cond):
    n, d = x.shape
    tb = TOKEN_BLOCK
    bpc = rows_per_cond // tb
    row = pl.BlockSpec((tb, d), lambda b: (b, 0))
    return pl.pallas_call(
        _residual_ln_kernel,
        grid=(n // tb,),
        in_specs=[row, row,
                  pl.BlockSpec((1, 1, d), lambda b: (b // bpc, 0, 0)),
                  pl.BlockSpec((1, d), lambda b: (0, 0)),
                  pl.BlockSpec((1, d), lambda b: (0, 0))],
        out_specs=row,
        out_shape=jax.ShapeDtypeStruct((n, d), F32),
        compiler_params=_params("arbitrary"),
        name="residual_ln",
    )(x, f, gate, ln_g, ln_b)


def _rope_tables(seq_len):
    t = np.arange(seq_len)
    row = (t // GRID_W).astype(np.float32)
    col = (t % GRID_W).astype(np.float32)
    n_freq = HEAD_DIM // 4
    inv = (ROPE_BASE ** (-np.arange(n_freq, dtype=np.float32) / n_freq)).astype(np.float32)
    ang = jnp.asarray(np.concatenate([row[:, None] * inv, col[:, None] * inv], -1).astype(np.float32))
    cos, sin = jnp.cos(ang), jnp.sin(ang)
    cos_t = jnp.tile(jnp.concatenate([cos, cos], -1), (1, H_GROUP))
    sin_t = jnp.tile(jnp.concatenate([-sin, sin], -1), (1, H_GROUP))
    return cos_t, sin_t


def _swap_halves_perm():
    idx = np.arange(W_GROUP)
    half = HEAD_DIM // 2
    return np.where((idx % HEAD_DIM) < half, idx + half, idx - half)


def _attention_bias(rpb, rows):
    wr = min(NA_WR, rows)
    cols = np.arange(GRID_W)
    c_start = np.clip(cols - NA_WC // 2, 0, GRID_W - NA_WC)
    col_in = (cols[None, :] >= c_start[:, None]) & (cols[None, :] < c_start[:, None] + NA_WC)
    col_idx = np.clip(cols[None, :] - cols[:, None], 1 - NA_WC, NA_WC - 1) + NA_WC - 1
    pick = jnp.asarray((col_idx[None] == np.arange(2 * NA_WC - 1)[:, None, None]).astype(np.float32))
    rpb_cols = jnp.einsum('hdj,jqk->hdqk', rpb, pick, precision=lax.Precision.HIGHEST)
    masked = jnp.where(jnp.asarray(col_in)[None, None], rpb_cols, NEG_BIG)
    tables = []
    for r in range(rows):
        rs = int(np.clip(r - wr // 2, 0, rows - wr))
        d0 = rs - r + NA_WR - 1
        blk = masked[:, d0:d0 + wr]
        tables.append(blk.transpose(0, 2, 1, 3).reshape(H_GROUP * GRID_W, wr * GRID_W))
    return jnp.stack(tables, axis=0)


def _block_diag_states(s0):
    b = s0.shape[0]
    eye = jnp.eye(H_GROUP, dtype=s0.dtype)
    full = jnp.einsum('bchde,hg->bchdge', s0, eye)
    return full.reshape(b, 2, H_GROUP * HEAD_DIM, H_GROUP * HEAD_DIM)


def kernel(x_prompt, x_sample, c, cache_na_k, cache_na_v, state_ret, c_ctx, w_mod, b_mod, w_in, w_out,
           na_rpb, ret_decay, sc_w, cf_w, cf_ln_g, cf_ln_b, ln1_g, ln1_b, ln2_g, ln2_b,
           peer_wq, peer_keys, peer_u, peer_v):
    n_req, seq, d = x_prompt.shape
    n_lat, lat_seq, _ = x_sample.shape
    past = cache_na_k.shape[2]
    rows = lat_seq // GRID_W

    cond8 = jnp.concatenate([c_ctx[None], c, jnp.zeros((8 - 1 - n_lat, d), F32)], axis=0)
    mod = _modulation(cond8, w_mod, b_mod)

    def mod_vec(l, j, lo, hi):
        return mod[l, lo:hi, j * d:(j + 1) * d][:, None, :]

    perm = _swap_halves_perm()
    w_in_b = w_in.astype(BF16)
    w_in_lat = jnp.concatenate([w_in_b, w_in_b[:, :, 3 * W_GROUP + perm], w_in_b[:, :, 4 * W_GROUP + perm]], -1)
    w_out_b = w_out.astype(BF16)
    wq_b = peer_wq.astype(BF16)
    keys_b = peer_keys.astype(BF16).reshape(DEPTH, PEER_HEADS * 2, PEER_NKEYS, PEER_NKEYS)
    u_b = peer_u.astype(BF16)
    vt_b = peer_v.astype(BF16)

    lane_h = np.arange(W_GROUP) // HEAD_DIM
    avg = jnp.asarray((lane_h[:, None] == lane_h[None, :]).astype(np.float32) / HEAD_DIM).astype(BF16)
    cos_t, sin_t = _rope_tables(lat_seq)
    dec_tiles = jnp.broadcast_to(ret_decay.reshape(DEPTH, 2 * H_GROUP, 1), (DEPTH, 8, LANES))
    scw = jnp.pad(sc_w, ((0, 0), (0, 8 - SC_WIDTH), (0, 0)))
    cfw = jnp.pad(cf_w, ((0, 0), (0, 32 - CF_WIDTH), (0, 0)))
    kc_all = cache_na_k.reshape(n_lat, DEPTH, past, W_GROUP)
    vc_all = cache_na_v.reshape(n_lat, DEPTH, past, W_GROUP)

    xp = x_prompt.reshape(n_req * seq, d)
    xs = x_sample.reshape(n_lat * lat_seq, d)
    ks_out, vs_out, ss_out = [], [], []

    for l in range(DEPTH):
        row1 = lambda a: a[l][None, :]
        cfg, cfb = row1(cf_ln_g), row1(cf_ln_b)
        l1g, l1b, l2g, l2b = row1(ln1_g), row1(ln1_b), row1(ln2_g), row1(ln2_b)

        n_ctx = n_req * seq
        proj = _modmm(xp, mod_vec(l, 0, 0, 1), mod_vec(l, 1, 0, 1), w_in_b[l], n_ctx)
        mix, k_ctx, v_ctx, s_ctx = _ctx_mixers(proj, n_req, seq, dec_tiles[l], avg, scw[l], cfw[l], cfg, cfb)
        xp = _outproj_ln([mix], xp, mod_vec(l, 2, 0, 1), w_out_b[l], l1g, l1b, n_ctx)
        hm, a_t, b_t, tau = _peer_select(xp, mod_vec(l, 3, 0, 1), mod_vec(l, 4, 0, 1), wq_b[l], keys_b[l], n_ctx)
        ffn = _peer_experts(hm, a_t, b_t, tau, u_b, vt_b, l)
        xp = _residual_ln(xp, ffn, mod_vec(l, 5, 0, 1), l2g, l2b, n_ctx)
        ks_out.append(k_ctx.reshape(n_req, seq, H_GROUP, HEAD_DIM))
        vs_out.append(v_ctx.reshape(n_req, seq, H_GROUP, HEAD_DIM))
        ss_out.append(s_ctx)

        hi = 1 + n_lat
        proj = _modmm(xs, mod_vec(l, 0, 1, hi), mod_vec(l, 1, 1, hi), w_in_lat[l], lat_seq)
        bias = _attention_bias(na_rpb[l], rows)
        o_a = _lat_attention(proj, n_lat, lat_seq, kc_all[:, l], vc_all[:, l], bias)
        o_b = _lat_retention(proj, n_lat, lat_seq, cos_t, sin_t, _block_diag_states(state_ret[:, l]),
                             dec_tiles[l], avg)
        o_cd = _lat_convs(proj, n_lat, lat_seq, scw[l], cfw[l], cfg, cfb)
        xs = _outproj_ln([o_a, o_b, o_cd], xs, mod_vec(l, 2, 1, hi), w_out_b[l], l1g, l1b, lat_seq)
        hm, a_t, b_t, tau = _peer_select(xs, mod_vec(l, 3, 1, hi), mod_vec(l, 4, 1, hi), wq_b[l], keys_b[l], lat_seq)
        ffn = _peer_experts(hm, a_t, b_t, tau, u_b, vt_b, l)
        xs = _residual_ln(xs, ffn, mod_vec(l, 5, 1, hi), l2g, l2b, lat_seq)

    return (xp.reshape(n_req, seq, d), xs.reshape(n_lat, lat_seq, d),
            jnp.stack(ks_out, axis=1), jnp.stack(vs_out, axis=1), jnp.stack(ss_out, axis=1))
```

```python
import functools
import math

import jax
import jax.numpy as jnp
import numpy as np
from jax import lax
from jax.experimental import pallas as pl
from jax.experimental.pallas import tpu as pltpu

F32 = jnp.float32
BF16 = jnp.bfloat16

D_MODEL = 1024
DEPTH = 4
GRID_W = 64
HEAD_DIM = 64
W_GROUP = 256
H_GROUP = 4
N_IN_PARTS = 12
NA_WR = 8
NA_WC = 16
ROPE_BASE = 10000.0
SC_WIDTH = 3
CF_WIDTH = 31
PEER_HEADS = 8
PEER_NKEYS = 128
PEER_TOPK = 16
N_EXPERTS = PEER_NKEYS * PEER_NKEYS
LN_EPS = 1e-5
ALPHA = (2.0 * DEPTH) ** 0.25
NEG_BIG = -1e30

VMEM_LIMIT_BYTES = 56 * 1024 * 1024
LANES = 128

TOKEN_BLOCK = 512
PEER_TOKEN_BLOCK = 1024
PEER_TOKEN_GROUP = 256
PEER_EXPERT_CHUNK = 1024
PEER_EXPERT_SPLIT = 1


def _params(*sem):
    return pltpu.CompilerParams(dimension_semantics=sem, vmem_limit_bytes=VMEM_LIMIT_BYTES)


def _sigmoid(x):
    return 1.0 / (1.0 + jnp.exp(-x))


def _silu(x):
    return x * _sigmoid(x)


def _layernorm_rows(x, g, b):
    mu = jnp.mean(x, axis=-1, keepdims=True)
    d = x - mu
    var = jnp.mean(d * d, axis=-1, keepdims=True)
    return d * lax.rsqrt(var + LN_EPS) * g + b


def _dot(a, b):
    return jnp.dot(a, b, preferred_element_type=F32)


def _dot_nt(a, b):
    return lax.dot_general(a, b, (((1,), (1,)), ((), ())), preferred_element_type=F32)


def _dot_f32(a, b):
    hi = a.astype(BF16)
    lo = (a - hi.astype(F32)).astype(BF16)
    return _dot(hi, b) + _dot(lo, b)


def _head_masks():
    lane = lax.broadcasted_iota(jnp.int32, (1, W_GROUP), 1)
    return [((lane >= h * HEAD_DIM) & (lane < (h + 1) * HEAD_DIM)).astype(F32) for h in range(H_GROUP)]


def _mod_kernel(c_ref, w_ref, b_ref, o_ref):
    c = c_ref[...]
    s = _silu(c)
    o_ref[0] = jnp.dot(s, w_ref[0], precision=lax.Precision.HIGHEST,
                       preferred_element_type=F32) + b_ref[0]


def _modulation(cond8, w_mod, b_mod):
    nc = 1024
    n_out = w_mod.shape[-1]
    return pl.pallas_call(
        _mod_kernel,
        grid=(DEPTH, n_out // nc),
        in_specs=[
            pl.BlockSpec((8, D_MODEL), lambda l, j: (0, 0)),
            pl.BlockSpec((1, D_MODEL, nc), lambda l, j: (l, 0, j)),
            pl.BlockSpec((1, 1, nc), lambda l, j: (l, 0, j)),
        ],
        out_specs=pl.BlockSpec((1, 8, nc), lambda l, j: (l, 0, j)),
        out_shape=jax.ShapeDtypeStruct((DEPTH, 8, n_out), F32),
        compiler_params=_params("arbitrary", "arbitrary"),
        name="modulation",
    )(cond8, w_mod, b_mod.reshape(DEPTH, 1, n_out))


def _modmm_kernel(x_ref, sh_ref, sc_ref, w_ref, o_ref):
    h = x_ref[...] * (1.0 + sc_ref[0]) + sh_ref[0]
    o_ref[...] = _dot(h.astype(BF16), w_ref[...])


def _modmm(x, sh, sc, w, rows_per_cond):
    n, d = x.shape
    n_out = w.shape[1]
    tb = TOKEN_BLOCK
    bpc = rows_per_cond // tb
    vec = pl.BlockSpec((1, 1, d), lambda b: (b // bpc, 0, 0))
    return pl.pallas_call(
        _modmm_kernel,
        grid=(n // tb,),
        in_specs=[
            pl.BlockSpec((tb, d), lambda b: (b, 0)),
            vec, vec,
            pl.BlockSpec((d, n_out), lambda b: (0, 0)),
        ],
        out_specs=pl.BlockSpec((tb, n_out), lambda b: (b, 0)),
        out_shape=jax.ShapeDtypeStruct((n, n_out), F32),
        compiler_params=_params("arbitrary"),
        name="modulated_projection",
    )(x, sh, sc, w)


def _log_gamma_tile(decay_tile):
    y = -decay_tile
    return -(jnp.maximum(y, 0.0) + jnp.log1p(jnp.exp(-jnp.abs(y))))


def _lane_vector(lg_tile, row0, masks):
    out = lg_tile[row0:row0 + 1, 0:1] * masks[0]
    for h in range(1, H_GROUP):
        out = out + lg_tile[row0 + h:row0 + h + 1, 0:1] * masks[h]
    return out


def _retention_pairs(rq, rk, rv, lg_tile, masks, seq_len, q_block):
    kb = rk.astype(BF16)
    vb = rv.astype(BF16)
    blocks = []
    for q0 in range(0, seq_len, q_block):
        rqb = rq[q0:q0 + q_block]
        ti = lax.broadcasted_iota(jnp.int32, (q_block, seq_len), 0) + q0
        si = lax.broadcasted_iota(jnp.int32, (q_block, seq_len), 1)
        dist = (ti - si).astype(F32)
        fwd = jnp.maximum(dist, 0.0)
        bwd = jnp.maximum(-dist, 0.0)
        acc = jnp.zeros((q_block, W_GROUP), F32)
        for h in range(H_GROUP):
            lgf = lg_tile[h:h + 1, 0:1]
            lgb = lg_tile[H_GROUP + h:H_GROUP + h + 1, 0:1]
            decay = (jnp.where(dist >= 0, jnp.exp(lgf * fwd), 0.0)
                     + jnp.where(dist <= 0, jnp.exp(lgb * bwd), 0.0))
            s = _dot_nt((rqb * masks[h]).astype(BF16), kb)
            acc = acc + _dot((s * decay).astype(BF16), vb) * masks[h]
        blocks.append(acc)
    return blocks[0] if len(blocks) == 1 else jnp.concatenate(blocks, axis=0)


def _retention_finish(o, r_g, avg):
    mu = _dot_f32(o, avg)
    d = o - mu
    var = _dot_f32(d * d, avg)
    return _silu(r_g) * (d * lax.rsqrt(var + LN_EPS))


def _shift_rows(z, off, seq_len):
    if off == 0:
        return z
    rolled = pltpu.roll(z, (-off) % seq_len, 0)
    t = lax.broadcasted_iota(jnp.int32, z.shape, 0)
    valid = (t + off >= 0) & (t + off < seq_len)
    return jnp.where(valid, rolled, 0.0)


def _dwconv(z, w_ref, width, seq_len):
    pad = (width - 1) // 2
    acc = None
    for k in range(width):
        term = _shift_rows(z, k - pad, seq_len) * w_ref[k:k + 1, :]
        acc = term if acc is None else acc + term
    return acc


def _conv_mixers(sc_b, sc_c, sc_x, cf_a, cf_gate, scw_ref, cfw_ref, cfg_ref, cfb_ref, seq_len):
    o_c = sc_b * _dwconv(sc_c * sc_x, scw_ref, SC_WIDTH, seq_len)
    u = _dwconv(cf_a * _sigmoid(cf_gate), cfw_ref, CF_WIDTH, seq_len)
    o_d = _silu(_layernorm_rows(u, cfg_ref[...], cfb_ref[...]))
    return o_c, o_d


def _ctx_mixer_kernel(p_ref, dec_ref, avg_ref, scw_ref, cfw_ref, cfg_ref, cfb_ref,
                      mix_ref, k_ref, v_ref, st_ref, *, seq_len):
    masks = _head_masks()
    part = lambda j: p_ref[:, j * W_GROUP:(j + 1) * W_GROUP]
    na_q, na_k, na_v = part(0), part(1), part(2)
    k_ref[...] = na_k
    v_ref[...] = na_v

    kb = na_k.astype(BF16)
    vb = na_v.astype(BF16)
    qs = na_q * (HEAD_DIM ** -0.5)
    o_a = jnp.zeros((seq_len, W_GROUP), F32)
    for h in range(H_GROUP):
        s = _dot_nt((qs * masks[h]).astype(BF16), kb)
        m = jnp.max(s, axis=-1, keepdims=True)
        e = jnp.exp(s - m)
        p = e / jnp.sum(e, axis=-1, keepdims=True)
        o_a = o_a + _dot(p.astype(BF16), vb) * masks[h]
    mix_ref[:, 0:W_GROUP] = o_a

    lg = _log_gamma_tile(dec_ref[...])
    rq, rv, r_g = part(3), part(5), part(6)
    rk = part(4) * (HEAD_DIM ** -0.5)
    o = _retention_pairs(rq, rk, rv, lg, masks, seq_len, seq_len)
    mix_ref[:, W_GROUP:2 * W_GROUP] = _retention_finish(o, r_g, avg_ref[...])

    pos = lax.broadcasted_iota(jnp.int32, (seq_len, W_GROUP), 0).astype(F32)
    vb_r = rv.astype(BF16)
    for d in range(2):
        lane_lg = _lane_vector(lg, d * H_GROUP, masks)
        expo = (seq_len - 1.0 - pos) if d == 0 else pos
        kd = rk * jnp.exp(lane_lg * expo)
        full = _dot(kd.T.astype(BF16), vb_r)
        for h in range(H_GROUP):
            st_ref[0, d, h] = full[h * HEAD_DIM:(h + 1) * HEAD_DIM, h * HEAD_DIM:(h + 1) * HEAD_DIM]

    o_c, o_d = _conv_mixers(part(7), part(8), part(9), part(10), part(11),
                            scw_ref, cfw_ref, cfg_ref, cfb_ref, seq_len)
    mix_ref[:, 2 * W_GROUP:3 * W_GROUP] = o_c
    mix_ref[:, 3 * W_GROUP:4 * W_GROUP] = o_d


def _ctx_mixers(proj, n_req, seq_len, dec_tile, avg, scw, cfw, cfg, cfb):
    n = n_req * seq_len
    const = lambda shape: pl.BlockSpec(shape, lambda b: (0,) * len(shape))
    return pl.pallas_call(
        functools.partial(_ctx_mixer_kernel, seq_len=seq_len),
        grid=(n_req,),
        in_specs=[
            pl.BlockSpec((seq_len, N_IN_PARTS * W_GROUP), lambda b: (b, 0)),
            const((8, LANES)), const((W_GROUP, W_GROUP)), const((8, W_GROUP)),
            const((32, W_GROUP)), const((1, W_GROUP)), const((1, W_GROUP)),
        ],
        out_specs=[
            pl.BlockSpec((seq_len, D_MODEL), lambda b: (b, 0)),
            pl.BlockSpec((seq_len, W_GROUP), lambda b: (b, 0)),
            pl.BlockSpec((seq_len, W_GROUP), lambda b: (b, 0)),
            pl.BlockSpec((1, 2, H_GROUP, HEAD_DIM, HEAD_DIM), lambda b: (b, 0, 0, 0, 0)),
        ],
        out_shape=[
            jax.ShapeDtypeStruct((n, D_MODEL), F32),
            jax.ShapeDtypeStruct((n, W_GROUP), F32),
            jax.ShapeDtypeStruct((n, W_GROUP), F32),
            jax.ShapeDtypeStruct((n_req, 2, H_GROUP, HEAD_DIM, HEAD_DIM), F32),
        ],
        compiler_params=_params("arbitrary"),
        name="context_mixers",
    )(proj, dec_tile, avg, scw, cfw, cfg, cfb)


def _lat_attn_kernel(q_ref, k_ref, v_ref, kc_ref, vc_ref, bias_ref, o_ref, *, rows):
    masks = _head_masks()
    kcb = kc_ref[0].astype(BF16)
    vcb = vc_ref[0].astype(BF16)
    wr = min(NA_WR, rows)

    def row_block(r, carry):
        rs = jnp.clip(r - wr // 2, 0, rows - wr)
        q0 = pl.multiple_of(r * GRID_W, GRID_W)
        k0 = pl.multiple_of(rs * GRID_W, GRID_W)
        qb = q_ref[pl.ds(q0, GRID_W), :] * (HEAD_DIM ** -0.5)
        qs = jnp.concatenate([qb * masks[h] for h in range(H_GROUP)], axis=0).astype(BF16)
        kl = k_ref[pl.ds(k0, wr * GRID_W), :].astype(BF16)
        vl = v_ref[pl.ds(k0, wr * GRID_W), :].astype(BF16)
        s_loc = _dot_nt(qs, kl) + bias_ref[r]
        s_ctx = _dot_nt(qs, kcb)
        m = jnp.maximum(jnp.max(s_loc, axis=-1, keepdims=True), jnp.max(s_ctx, axis=-1, keepdims=True))
        e_loc = jnp.exp(s_loc - m)
        e_ctx = jnp.exp(s_ctx - m)
        z = jnp.sum(e_loc, axis=-1, keepdims=True) + jnp.sum(e_ctx, axis=-1, keepdims=True)
        o = (_dot(e_loc.astype(BF16), vl) + _dot(e_ctx.astype(BF16), vcb)) / z
        acc = o[0:GRID_W] * masks[0]
        for h in range(1, H_GROUP):
            acc = acc + o[h * GRID_W:(h + 1) * GRID_W] * masks[h]
        o_ref[pl.ds(q0, GRID_W), :] = acc
        return carry

    lax.fori_loop(0, rows, row_block, 0)


def _lat_attention(proj, n_req, seq_len, kc, vc, bias):
    rows = seq_len // GRID_W
    part = lambda j: pl.BlockSpec((seq_len, W_GROUP), lambda b, j=j: (b, j))
    past = kc.shape[1]
    return pl.pallas_call(
        functools.partial(_lat_attn_kernel, rows=rows),
        grid=(n_req,),
        in_specs=[
            part(0), part(1), part(2),
            pl.BlockSpec((1, past, W_GROUP), lambda b: (b, 0, 0)),
            pl.BlockSpec((1, past, W_GROUP), lambda b: (b, 0, 0)),
            pl.BlockSpec(bias.shape, lambda b: (0, 0, 0)),
        ],
        out_specs=pl.BlockSpec((seq_len, W_GROUP), lambda b: (b, 0)),
        out_shape=jax.ShapeDtypeStruct((n_req * seq_len, W_GROUP), F32),
        compiler_params=_params("arbitrary"),
        name="latent_attention",
    )(proj, proj, proj, kc, vc, bias)


def _lat_ret_kernel(q_ref, k_ref, v_ref, g_ref, qs_ref, ks_ref, cos_ref, sin_ref, s0_ref,
                    dec_ref, avg_ref, o_ref, *, seq_len):
    masks = _head_masks()
    lg = _log_gamma_tile(dec_ref[...])
    cos = cos_ref[...]
    sin = sin_ref[...]
    rq = q_ref[...] * cos + qs_ref[...] * sin
    rk = (k_ref[...] * cos + ks_ref[...] * sin) * (HEAD_DIM ** -0.5)
    rv = v_ref[...]
    o = _retention_pairs(rq, rk, rv, lg, masks, seq_len, 256)
    pos = lax.broadcasted_iota(jnp.int32, (seq_len, W_GROUP), 0).astype(F32)
    rqb = rq.astype(BF16)
    lgf = _lane_vector(lg, 0, masks)
    lgb = _lane_vector(lg, H_GROUP, masks)
    o = o + _dot(rqb, s0_ref[0, 0].astype(BF16)) * jnp.exp(lgf * (pos + 1.0))
    o = o + _dot(rqb, s0_ref[0, 1].astype(BF16)) * jnp.exp(lgb * (seq_len - pos))
    o_ref[...] = _retention_finish(o, g_ref[...], avg_ref[...])


def _lat_retention(proj, n_req, seq_len, cos, sin, s0_bd, dec_tile, avg):
    part = lambda j: pl.BlockSpec((seq_len, W_GROUP), lambda b, j=j: (b, j))
    const = lambda shape: pl.BlockSpec(shape, lambda b: (0,) * len(shape))
    return pl.pallas_call(
        functools.partial(_lat_ret_kernel, seq_len=seq_len),
        grid=(n_req,),
        in_specs=[
            part(3), part(4), part(5), part(6), part(12), part(13),
            const((seq_len, W_GROUP)), const((seq_len, W_GROUP)),
            pl.BlockSpec((1, 2, W_GROUP, W_GROUP), lambda b: (b, 0, 0, 0)),
            const((8, LANES)), const((W_GROUP, W_GROUP)),
        ],
        out_specs=pl.BlockSpec((seq_len, W_GROUP), lambda b: (b, 0)),
        out_shape=jax.ShapeDtypeStruct((n_req * seq_len, W_GROUP), F32),
        compiler_params=_params("arbitrary"),
        name="latent_retention",
    )(proj, proj, proj, proj, proj, proj, cos, sin, s0_bd, dec_tile, avg)


def _lat_conv_kernel(b_ref, c_ref, x_ref, a_ref, gate_ref, scw_ref, cfw_ref, cfg_ref, cfb_ref,
                     o_ref, *, seq_len):
    o_c, o_d = _conv_mixers(b_ref[...], c_ref[...], x_ref[...], a_ref[...], gate_ref[...],
                            scw_ref, cfw_ref, cfg_ref, cfb_ref, seq_len)
    o_ref[:, 0:W_GROUP] = o_c
    o_ref[:, W_GROUP:2 * W_GROUP] = o_d


def _lat_convs(proj, n_req, seq_len, scw, cfw, cfg, cfb):
    part = lambda j: pl.BlockSpec((seq_len, W_GROUP), lambda b, j=j: (b, j))
    const = lambda shape: pl.BlockSpec(shape, lambda b: (0,) * len(shape))
    return pl.pallas_call(
        functools.partial(_lat_conv_kernel, seq_len=seq_len),
        grid=(n_req,),
        in_specs=[
            part(7), part(8), part(9), part(10), part(11),
            const((8, W_GROUP)), const((32, W_GROUP)), const((1, W_GROUP)), const((1, W_GROUP)),
        ],
        out_specs=pl.BlockSpec((seq_len, 2 * W_GROUP), lambda b: (b, 0)),
        out_shape=jax.ShapeDtypeStruct((n_req * seq_len, 2 * W_GROUP), F32),
        compiler_params=_params("arbitrary"),
        name="latent_convs",
    )(proj, proj, proj, proj, proj, scw, cfw, cfg, cfb)


def _outproj_kernel(*refs, widths):
    n_parts = len(widths)
    part_refs = refs[:n_parts]
    x_ref, g_ref, w_ref, lg_ref, lb_ref, o_ref = refs[n_parts:]
    mix = None
    row = 0
    for p_ref, wd in zip(part_refs, widths):
        term = _dot(p_ref[...].astype(BF16), w_ref[row:row + wd, :])
        mix = term if mix is None else mix + term
        row += wd
    y = ALPHA * x_ref[...] + g_ref[0] * mix
    o_ref[...] = _layernorm_rows(y, lg_ref[...], lb_ref[...])


def _outproj_ln(parts, x, gate, w_out, ln_g, ln_b, rows_per_cond):
    n, d = x.shape
    tb = TOKEN_BLOCK
    bpc = rows_per_cond // tb
    widths = tuple(p.shape[1] for p in parts)
    return pl.pallas_call(
        functools.partial(_outproj_kernel, widths=widths),
        grid=(n // tb,),
        in_specs=[pl.BlockSpec((tb, wd), lambda b: (b, 0)) for wd in widths] + [
            pl.BlockSpec((tb, d), lambda b: (b, 0)),
            pl.BlockSpec((1, 1, d), lambda b: (b // bpc, 0, 0)),
            pl.BlockSpec((d, d), lambda b: (0, 0)),
            pl.BlockSpec((1, d), lambda b: (0, 0)),
            pl.BlockSpec((1, d), lambda b: (0, 0)),
        ],
        out_specs=pl.BlockSpec((tb, d), lambda b: (b, 0)),
        out_shape=jax.ShapeDtypeStruct((n, d), F32),
        compiler_params=_params("arbitrary"),
        name="output_projection_ln",
    )(*parts, x, gate, w_out, ln_g, ln_b)


GATE_DTYPE = jnp.bfloat16


def _pair_words(x):
    bits = pltpu.bitcast(x.astype(F32), jnp.uint32)
    return bits | (bits >> 16)


def _packed_rows(word_row, rows):
    return pltpu.bitcast(jnp.broadcast_to(word_row, (rows // 2, word_row.shape[-1])), GATE_DTYPE)


def _bitonic_merge_desc(vals):
    vals = list(vals)
    n = len(vals)
    j = n // 2
    while j >= 1:
        for i in range(n):
            l = i ^ j
            if l > i:
                vals[i], vals[l] = jnp.maximum(vals[i], vals[l]), jnp.minimum(vals[i], vals[l])
        j //= 2
    return vals


def _bitonic_sort_desc(vals):
    vals = list(vals)
    n = len(vals)
    k = 2
    while k <= n:
        j = k // 2
        while j >= 1:
            for i in range(n):
                l = i ^ j
                if l > i:
                    hi, lo = jnp.maximum(vals[i], vals[l]), jnp.minimum(vals[i], vals[l])
                    vals[i], vals[l] = (hi, lo) if (i & k) == 0 else (lo, hi)
            j //= 2
        k *= 2
    return vals


def _extract_top(s, count, roll=None):
    roll = roll or (lambda v, r: pltpu.roll(v, r, 0))
    sub = 8
    assert s.shape[0] == sub * count
    slabs = _bitonic_sort_desc([s[sub * k:sub * (k + 1)] for k in range(count)])
    r = sub // 2
    while r >= 1:
        other = [roll(v, r) for v in slabs]
        slabs = _bitonic_merge_desc([jnp.maximum(slabs[k], other[count - 1 - k]) for k in range(count)])
        r //= 2
    return [v[0:1] for v in slabs]


def _peer_select_kernel(x_ref, sh_ref, sc_ref, wq_ref, keys_ref, hm_ref, a_ref, b_ref, tau_ref):
    hm = (x_ref[...] * (1.0 + sc_ref[0]) + sh_ref[0]).astype(BF16)
    hm_ref[...] = hm
    q = _dot(hm, wq_ref[...]).astype(BF16)
    dk = PEER_NKEYS
    as_gate = lambda v: v.astype(GATE_DTYPE).astype(F32)
    tops = [[None, None] for _ in range(PEER_HEADS)]
    for h in range(PEER_HEADS):
        for p in range(2):
            j = h * 2 + p
            s = _dot_nt(keys_ref[j], q[:, j * dk:(j + 1) * dk])
            top = _extract_top(s, PEER_TOPK)
            e = jnp.where(s >= top[-1], jnp.exp(s - top[0]), 0.0).astype(GATE_DTYPE)
            tops[h][p] = [as_gate(jnp.exp(t - top[0])) for t in top]
            if p == 0:
                a_ref[h] = _pair_words(e)
            else:
                b_ref[h] = e

    stack = lambda rows_per_head: jnp.concatenate(rows_per_head, axis=0)
    a_rank = [stack([tops[h][0][r] for h in range(PEER_HEADS)]) for r in range(PEER_TOPK)]
    b_rank = [stack([tops[h][1][r] for h in range(PEER_HEADS)]) for r in range(PEER_TOPK)]
    likely = [a_rank[r] * b_rank[c] for r in range(PEER_TOPK) for c in range(PEER_TOPK)
              if (r + 1) * (c + 1) <= PEER_TOPK]
    padded = likely + [jnp.full_like(likely[0], -1.0)] * (64 - len(likely))
    kth = _bitonic_sort_desc(padded)[PEER_TOPK - 1]
    a_all = jnp.stack(a_rank)[:, None]
    b_all = jnp.stack(b_rank)[None]
    cand = a_all * b_all
    chosen = cand >= kth
    z = jnp.sum(jnp.where(chosen, cand, 0.0), axis=(0, 1))
    rz = 1.0 / z
    scaled = (a_all.astype(GATE_DTYPE) * (b_all * rz).astype(GATE_DTYPE)).astype(F32)
    tau = jnp.min(jnp.where(chosen, scaled, jnp.inf), axis=(0, 1))
    tau_ref[...] = _pair_words(tau.astype(GATE_DTYPE))
    for h in range(PEER_HEADS):
        b_ref[h] = (b_ref[h].astype(F32) * rz[h:h + 1, :]).astype(GATE_DTYPE)


def _peer_select(x, sh, sc, wq, keys, rows_per_cond):
    n, d = x.shape
    tb = TOKEN_BLOCK
    bpc = rows_per_cond // tb
    vec = pl.BlockSpec((1, 1, d), lambda b: (b // bpc, 0, 0))
    grid_hn = lambda: pl.BlockSpec((PEER_HEADS, PEER_NKEYS, tb), lambda b: (0, 0, b))
    return pl.pallas_call(
        _peer_select_kernel,
        grid=(n // tb,),
        in_specs=[
            pl.BlockSpec((tb, d), lambda b: (b, 0)),
            vec, vec,
            pl.BlockSpec(wq.shape, lambda b: (0, 0)),
            pl.BlockSpec(keys.shape, lambda b: (0, 0, 0)),
        ],
        out_specs=[
            pl.BlockSpec((tb, d), lambda b: (b, 0)),
            grid_hn(), grid_hn(),
            pl.BlockSpec((PEER_HEADS, tb), lambda b: (0, b)),
        ],
        out_shape=[
            jax.ShapeDtypeStruct((n, d), BF16),
            jax.ShapeDtypeStruct((PEER_HEADS, PEER_NKEYS, n), jnp.uint32),
            jax.ShapeDtypeStruct((PEER_HEADS, PEER_NKEYS, n), GATE_DTYPE),
            jax.ShapeDtypeStruct((PEER_HEADS, n), jnp.uint32),
        ],
        compiler_params=_params("arbitrary"),
        name="peer_select",
    )(x, sh, sc, wq, keys)


def _gelu(x):
    return 0.5 * x * (1.0 + lax.erf(x * (2.0 ** -0.5)))


def _peer_expert_kernel(hm_ref, a_ref, b_ref, tau_ref, u_ref, vt_ref, o_ref, acc_ref, ht_ref, gh_ref,
                        *, n_chunks, tok_block):
    c = pl.program_id(1)

    @pl.when(c == 0)
    def _():
        acc_ref[...] = jnp.zeros_like(acc_ref)

    rows_per_chunk = PEER_EXPERT_CHUNK // PEER_NKEYS
    group = PEER_TOKEN_GROUP
    n_groups = tok_block // group
    v_t = vt_ref[0].T

    halves = PEER_EXPERT_SPLIT
    half_rows = rows_per_chunk // halves
    half = PEER_EXPERT_CHUNK // halves

    def pre_activations(g, eh):
        tok = pl.ds(g * group, group)
        ex = pl.ds(eh * half, half)
        ht_ref[ex, tok] = _dot_nt(u_ref[0, ex, :], hm_ref[tok, :])

    def gated_activations(g, eh):
        for sub in range(group // LANES):
            tok = pl.ds(g * group + sub * LANES, LANES)
            for i in range(eh * half_rows, (eh + 1) * half_rows):
                rows = pl.ds(i * PEER_NKEYS, PEER_NKEYS)
                gate = jnp.zeros((PEER_NKEYS, LANES), GATE_DTYPE)
                for h in range(PEER_HEADS):
                    e = _packed_rows(a_ref[h, i:i + 1, tok], PEER_NKEYS) * b_ref[h, :, tok]
                    tau = _packed_rows(tau_ref[h:h + 1, tok], PEER_NKEYS)
                    gate = jnp.where(e >= tau, gate + e, gate)
                act = _gelu(ht_ref[rows, tok]).astype(GATE_DTYPE)
                gh_ref[rows, tok] = (gate * act).astype(BF16)

    def accumulate(g):
        tok = pl.ds(g * group, group)
        acc_ref[:, tok] += _dot(v_t, gh_ref[:, tok])

    units = [(g, eh) for g in range(n_groups) for eh in range(halves)]
    for k in range(len(units) + 1):
        if k < len(units):
            pre_activations(*units[k])
        if k >= 1:
            g, eh = units[k - 1]
            gated_activations(g, eh)
            if eh == halves - 1:
                accumulate(g)

    @pl.when(c == n_chunks - 1)
    def _():
        o_ref[...] = acc_ref[...].T


def _peer_experts(hm, a_t, b_t, tau, u_all, vt_all, layer):
    n, d = hm.shape
    tb = PEER_TOKEN_BLOCK
    ec = PEER_EXPERT_CHUNK
    n_chunks = N_EXPERTS // ec
    rpc = ec // PEER_NKEYS
    return pl.pallas_call(
        functools.partial(_peer_expert_kernel, n_chunks=n_chunks, tok_block=tb),
        grid=(n // tb, n_chunks),
        in_specs=[
            pl.BlockSpec((tb, d), lambda b, c: (b, 0)),
            pl.BlockSpec((PEER_HEADS, rpc, tb), lambda b, c: (0, c, b)),
            pl.BlockSpec((PEER_HEADS, PEER_NKEYS, tb), lambda b, c: (0, 0, b)),
            pl.BlockSpec((PEER_HEADS, tb), lambda b, c: (0, b)),
            pl.BlockSpec((1, ec, d), lambda b, c: (layer, c, 0)),
            pl.BlockSpec((1, ec, d), lambda b, c: (layer, c, 0)),
        ],
        out_specs=pl.BlockSpec((tb, d), lambda b, c: (b, 0)),
        out_shape=jax.ShapeDtypeStruct((n, d), F32),
        scratch_shapes=[
            pltpu.VMEM((d, tb), F32),
            pltpu.VMEM((ec, tb), F32),
            pltpu.VMEM((ec, tb), BF16),
        ],
        compiler_params=_params("arbitrary", "arbitrary"),
        name="peer_experts",
    )(hm, a_t, b_t, tau, u_all, vt_all)


def _residual_ln_kernel(x_ref, f_ref, g_ref, lg_ref, lb_ref, o_ref):
    y = ALPHA * x_ref[...] + g_ref[0] * f_ref[...]
    o_ref[...] = _layernorm_rows(y, lg_ref[...], lb_ref[...])


def _residual_ln(x, f, gate, ln_g, ln_b, rows_per_cond):
    n, d = x.shape
    tb = TOKEN_BLOCK
    bpc = rows_per_cond // tb
    row = pl.BlockSpec((tb, d), lambda b: (b, 0))
    return pl.pallas_call(
        _residual_ln_kernel,
        grid=(n // tb,),
        in_specs=[row, row,
                  pl.BlockSpec((1, 1, d), lambda b: (b // bpc, 0, 0)),
                  pl.BlockSpec((1, d), lambda b: (0, 0)),
                  pl.BlockSpec((1, d), lambda b: (0, 0))],
        out_specs=row,
        out_shape=jax.ShapeDtypeStruct((n, d), F32),
        compiler_params=_params("arbitrary"),
        name="residual_ln",
    )(x, f, gate, ln_g, ln_b)


def _rope_tables(seq_len):
    t = np.arange(seq_len)
    row = (t // GRID_W).astype(np.float32)
    col = (t % GRID_W).astype(np.float32)
    n_freq = HEAD_DIM // 4
    inv = (ROPE_BASE ** (-np.arange(n_freq, dtype=np.float32) / n_freq)).astype(np.float32)
    ang = jnp.asarray(np.concatenate([row[:, None] * inv, col[:, None] * inv], -1).astype(np.float32))
    cos, sin = jnp.cos(ang), jnp.sin(ang)
    cos_t = jnp.tile(jnp.concatenate([cos, cos], -1), (1, H_GROUP))
    sin_t = jnp.tile(jnp.concatenate([-sin, sin], -1), (1, H_GROUP))
    return cos_t, sin_t


def _swap_halves_perm():
    idx = np.arange(W_GROUP)
    half = HEAD_DIM // 2
    return np.where((idx % HEAD_DIM) < half, idx + half, idx - half)


def _attention_bias(rpb, rows):
    wr = min(NA_WR, rows)
    cols = np.arange(GRID_W)
    c_start = np.clip(cols - NA_WC // 2, 0, GRID_W - NA_WC)
    col_in = (cols[None, :] >= c_start[:, None]) & (cols[None, :] < c_start[:, None] + NA_WC)
    col_idx = np.clip(cols[None, :] - cols[:, None], 1 - NA_WC, NA_WC - 1) + NA_WC - 1
    pick = jnp.asarray((col_idx[None] == np.arange(2 * NA_WC - 1)[:, None, None]).astype(np.float32))
    rpb_cols = jnp.einsum('hdj,jqk->hdqk', rpb, pick, precision=lax.Precision.HIGHEST)
    masked = jnp.where(jnp.asarray(col_in)[None, None], rpb_cols, NEG_BIG)
    tables = []
    for r in range(rows):
        rs = int(np.clip(r - wr // 2, 0, rows - wr))
        d0 = rs - r + NA_WR - 1
        blk = masked[:, d0:d0 + wr]
        tables.append(blk.transpose(0, 2, 1, 3).reshape(H_GROUP * GRID_W, wr * GRID_W))
    return jnp.stack(tables, axis=0)


def _block_diag_states(s0):
    b = s0.shape[0]
    eye = jnp.eye(H_GROUP, dtype=s0.dtype)
    full = jnp.einsum('bchde,hg->bchdge', s0, eye)
    return full.reshape(b, 2, H_GROUP * HEAD_DIM, H_GROUP * HEAD_DIM)


def kernel(x_prompt, x_sample, c, cache_na_k, cache_na_v, state_ret, c_ctx, w_mod, b_mod, w_in, w_out,
           na_rpb, ret_decay, sc_w, cf_w, cf_ln_g, cf_ln_b, ln1_g, ln1_b, ln2_g, ln2_b,
           peer_wq, peer_keys, peer_u, peer_v):
    n_req, seq, d = x_prompt.shape
    n_lat, lat_seq, _ = x_sample.shape
    past = cache_na_k.shape[2]
    rows = lat_seq // GRID_W

    cond8 = jnp.concatenate([c_ctx[None], c, jnp.zeros((8 - 1 - n_lat, d), F32)], axis=0)
    mod = _modulation(cond8, w_mod, b_mod)

    def mod_vec(l, j, lo, hi):
        return mod[l, lo:hi, j * d:(j + 1) * d][:, None, :]

    perm = _swap_halves_perm()
    w_in_b = w_in.astype(BF16)
    w_in_lat = jnp.concatenate([w_in_b, w_in_b[:, :, 3 * W_GROUP + perm], w_in_b[:, :, 4 * W_GROUP + perm]], -1)
    w_out_b = w_out.astype(BF16)
    wq_b = peer_wq.astype(BF16)
    keys_b = peer_keys.astype(BF16).reshape(DEPTH, PEER_HEADS * 2, PEER_NKEYS, PEER_NKEYS)
    u_b = peer_u.astype(BF16)
    vt_b = peer_v.astype(BF16)

    lane_h = np.arange(W_GROUP) // HEAD_DIM
    avg = jnp.asarray((lane_h[:, None] == lane_h[None, :]).astype(np.float32) / HEAD_DIM).astype(BF16)
    cos_t, sin_t = _rope_tables(lat_seq)
    dec_tiles = jnp.broadcast_to(ret_decay.reshape(DEPTH, 2 * H_GROUP, 1), (DEPTH, 8, LANES))
    scw = jnp.pad(sc_w, ((0, 0), (0, 8 - SC_WIDTH), (0, 0)))
    cfw = jnp.pad(cf_w, ((0, 0), (0, 32 - CF_WIDTH), (0, 0)))
    kc_all = cache_na_k.reshape(n_lat, DEPTH, past, W_GROUP)
    vc_all = cache_na_v.reshape(n_lat, DEPTH, past, W_GROUP)

    xp = x_prompt.reshape(n_req * seq, d)
    xs = x_sample.reshape(n_lat * lat_seq, d)
    ks_out, vs_out, ss_out = [], [], []

    for l in range(DEPTH):
        row1 = lambda a: a[l][None, :]
        cfg, cfb = row1(cf_ln_g), row1(cf_ln_b)
        l1g, l1b, l2g, l2b = row1(ln1_g), row1(ln1_b), row1(ln2_g), row1(ln2_b)

        n_ctx = n_req * seq
        proj = _modmm(xp, mod_vec(l, 0, 0, 1), mod_vec(l, 1, 0, 1), w_in_b[l], n_ctx)
        mix, k_ctx, v_ctx, s_ctx = _ctx_mixers(proj, n_req, seq, dec_tiles[l], avg, scw[l], cfw[l], cfg, cfb)
        xp = _outproj_ln([mix], xp, mod_vec(l, 2, 0, 1), w_out_b[l], l1g, l1b, n_ctx)
        hm, a_t, b_t, tau = _peer_select(xp, mod_vec(l, 3, 0, 1), mod_vec(l, 4, 0, 1), wq_b[l], keys_b[l], n_ctx)
        ffn = _peer_experts(hm, a_t, b_t, tau, u_b, vt_b, l)
        xp = _residual_ln(xp, ffn, mod_vec(l, 5, 0, 1), l2g, l2b, n_ctx)
        ks_out.append(k_ctx.reshape(n_req, seq, H_GROUP, HEAD_DIM))
        vs_out.append(v_ctx.reshape(n_req, seq, H_GROUP, HEAD_DIM))
        ss_out.append(s_ctx)

        hi = 1 + n_lat
        proj = _modmm(xs, mod_vec(l, 0, 1, hi), mod_vec(l, 1, 1, hi), w_in_lat[l], lat_seq)
        bias = _attention_bias(na_rpb[l], rows)
        o_a = _lat_attention(proj, n_lat, lat_seq, kc_all[:, l], vc_all[:, l], bias)
        o_b = _lat_retention(proj, n_lat, lat_seq, cos_t, sin_t, _block_diag_states(state_ret[:, l]),
                             dec_tiles[l], avg)
        o_cd = _lat_convs(proj, n_lat, lat_seq, scw[l], cfw[l], cfg, cfb)
        xs = _outproj_ln([o_a, o_b, o_cd], xs, mod_vec(l, 2, 1, hi), w_out_b[l], l1g, l1b, lat_seq)
        hm, a_t, b_t, tau = _peer_select(xs, mod_vec(l, 3, 1, hi), mod_vec(l, 4, 1, hi), wq_b[l], keys_b[l], lat_seq)
        ffn = _peer_experts(hm, a_t, b_t, tau, u_b, vt_b, l)
        xs = _residual_ln(xs, ffn, mod_vec(l, 5, 1, hi), l2g, l2b, lat_seq)

    return (xp.reshape(n_req, seq, d), xs.reshape(n_lat, lat_seq, d),
            jnp.stack(ks_out, axis=1), jnp.stack(vs_out, axis=1), jnp.stack(ss_out, axis=1))
```

```python
import functools
import math

import jax
import jax.numpy as jnp
import numpy as np
from jax import lax
from jax.experimental import pallas as pl
from jax.experimental.pallas import tpu as pltpu

F32 = jnp.float32
BF16 = jnp.bfloat16

D_MODEL = 1024
DEPTH = 4
GRID_W = 64
HEAD_DIM = 64
W_GROUP = 256
H_GROUP = 4
N_IN_PARTS = 12
NA_WR = 8
NA_WC = 16
ROPE_BASE = 10000.0
SC_WIDTH = 3
CF_WIDTH = 31
PEER_HEADS = 8
PEER_NKEYS = 128
PEER_TOPK = 16
N_EXPERTS = PEER_NKEYS * PEER_NKEYS
LN_EPS = 1e-5
ALPHA = (2.0 * DEPTH) ** 0.25
NEG_BIG = -1e30

VMEM_LIMIT_BYTES = 56 * 1024 * 1024
LANES = 128

TOKEN_BLOCK = 512
PEER_TOKEN_BLOCK = 1024
PEER_TOKEN_GROUP = 256
PEER_EXPERT_CHUNK = 1024
PEER_EXPERT_SPLIT = 1


def _params(*sem):
    return pltpu.CompilerParams(dimension_semantics=sem, vmem_limit_bytes=VMEM_LIMIT_BYTES)


def _sigmoid(x):
    return 1.0 / (1.0 + jnp.exp(-x))


def _silu(x):
    return x * _sigmoid(x)


def _layernorm_rows(x, g, b):
    mu = jnp.mean(x, axis=-1, keepdims=True)
    d = x - mu
    var = jnp.mean(d * d, axis=-1, keepdims=True)
    return d * lax.rsqrt(var + LN_EPS) * g + b


def _dot(a, b):
    return jnp.dot(a, b, preferred_element_type=F32)


def _dot_nt(a, b):
    return lax.dot_general(a, b, (((1,), (1,)), ((), ())), preferred_element_type=F32)


def _dot_f32(a, b):
    hi = a.astype(BF16)
    lo = (a - hi.astype(F32)).astype(BF16)
    return _dot(hi, b) + _dot(lo, b)


def _head_masks():
    lane = lax.broadcasted_iota(jnp.int32, (1, W_GROUP), 1)
    return [((lane >= h * HEAD_DIM) & (lane < (h + 1) * HEAD_DIM)).astype(F32) for h in range(H_GROUP)]


def _mod_kernel(c_ref, w_ref, b_ref, o_ref):
    c = c_ref[...]
    s = _silu(c)
    o_ref[0] = jnp.dot(s, w_ref[0], precision=lax.Precision.HIGHEST,
                       preferred_element_type=F32) + b_ref[0]


def _modulation(cond8, w_mod, b_mod):
    nc = 1024
    n_out = w_mod.shape[-1]
    return pl.pallas_call(
        _mod_kernel,
        grid=(DEPTH, n_out // nc),
        in_specs=[
            pl.BlockSpec((8, D_MODEL), lambda l, j: (0, 0)),
            pl.BlockSpec((1, D_MODEL, nc), lambda l, j: (l, 0, j)),
            pl.BlockSpec((1, 1, nc), lambda l, j: (l, 0, j)),
        ],
        out_specs=pl.BlockSpec((1, 8, nc), lambda l, j: (l, 0, j)),
        out_shape=jax.ShapeDtypeStruct((DEPTH, 8, n_out), F32),
        compiler_params=_params("arbitrary", "arbitrary"),
        name="modulation",
    )(cond8, w_mod, b_mod.reshape(DEPTH, 1, n_out))


def _modmm_kernel(x_ref, sh_ref, sc_ref, w_ref, o_ref):
    h = x_ref[...] * (1.0 + sc_ref[0]) + sh_ref[0]
    o_ref[...] = _dot(h.astype(BF16), w_ref[...])


def _modmm(x, sh, sc, w, rows_per_cond):
    n, d = x.shape
    n_out = w.shape[1]
    tb = TOKEN_BLOCK
    bpc = rows_per_cond // tb
    vec = pl.BlockSpec((1, 1, d), lambda b: (b // bpc, 0, 0))
    return pl.pallas_call(
        _modmm_kernel,
        grid=(n // tb,),
        in_specs=[
            pl.BlockSpec((tb, d), lambda b: (b, 0)),
            vec, vec,
            pl.BlockSpec((d, n_out), lambda b: (0, 0)),
        ],
        out_specs=pl.BlockSpec((tb, n_out), lambda b: (b, 0)),
        out_shape=jax.ShapeDtypeStruct((n, n_out), F32),
        compiler_params=_params("arbitrary"),
        name="modulated_projection",
    )(x, sh, sc, w)


def _log_gamma_tile(decay_tile):
    y = -decay_tile
    return -(jnp.maximum(y, 0.0) + jnp.log1p(jnp.exp(-jnp.abs(y))))


def _lane_vector(lg_tile, row0, masks):
    out = lg_tile[row0:row0 + 1, 0:1] * masks[0]
    for h in range(1, H_GROUP):
        out = out + lg_tile[row0 + h:row0 + h + 1, 0:1] * masks[h]
    return out


def _retention_pairs(rq, rk, rv, lg_tile, masks, seq_len, q_block):
    kb = rk.astype(BF16)
    vb = rv.astype(BF16)
    blocks = []
    for q0 in range(0, seq_len, q_block):
        rqb = rq[q0:q0 + q_block]
        ti = lax.broadcasted_iota(jnp.int32, (q_block, seq_len), 0) + q0
        si = lax.broadcasted_iota(jnp.int32, (q_block, seq_len), 1)
        dist = (ti - si).astype(F32)
        fwd = jnp.maximum(dist, 0.0)
        bwd = jnp.maximum(-dist, 0.0)
        acc = jnp.zeros((q_block, W_GROUP), F32)
        for h in range(H_GROUP):
            lgf = lg_tile[h:h + 1, 0:1]
            lgb = lg_tile[H_GROUP + h:H_GROUP + h + 1, 0:1]
            decay = (jnp.where(dist >= 0, jnp.exp(lgf * fwd), 0.0)
                     + jnp.where(dist <= 0, jnp.exp(lgb * bwd), 0.0))
            s = _dot_nt((rqb * masks[h]).astype(BF16), kb)
            acc = acc + _dot((s * decay).astype(BF16), vb) * masks[h]
        blocks.append(acc)
    return blocks[0] if len(blocks) == 1 else jnp.concatenate(blocks, axis=0)


def _retention_finish(o, r_g, avg):
    mu = _dot_f32(o, avg)
    d = o - mu
    var = _dot_f32(d * d, avg)
    return _silu(r_g) * (d * lax.rsqrt(var + LN_EPS))


def _shift_rows(z, off, seq_len):
    if off == 0:
        return z
    rolled = pltpu.roll(z, (-off) % seq_len, 0)
    t = lax.broadcasted_iota(jnp.int32, z.shape, 0)
    valid = (t + off >= 0) & (t + off < seq_len)
    return jnp.where(valid, rolled, 0.0)


def _dwconv(z, w_ref, width, seq_len):
    pad = (width - 1) // 2
    acc = None
    for k in range(width):
        term = _shift_rows(z, k - pad, seq_len) * w_ref[k:k + 1, :]
        acc = term if acc is None else acc + term
    return acc


def _conv_mixers(sc_b, sc_c, sc_x, cf_a, cf_gate, scw_ref, cfw_ref, cfg_ref, cfb_ref, seq_len):
    o_c = sc_b * _dwconv(sc_c * sc_x, scw_ref, SC_WIDTH, seq_len)
    u = _dwconv(cf_a * _sigmoid(cf_gate), cfw_ref, CF_WIDTH, seq_len)
    o_d = _silu(_layernorm_rows(u, cfg_ref[...], cfb_ref[...]))
    return o_c, o_d


def _ctx_mixer_kernel(p_ref, dec_ref, avg_ref, scw_ref, cfw_ref, cfg_ref, cfb_ref,
                      mix_ref, k_ref, v_ref, st_ref, *, seq_len):
    masks = _head_masks()
    part = lambda j: p_ref[:, j * W_GROUP:(j + 1) * W_GROUP]
    na_q, na_k, na_v = part(0), part(1), part(2)
    k_ref[...] = na_k
    v_ref[...] = na_v

    kb = na_k.astype(BF16)
    vb = na_v.astype(BF16)
    qs = na_q * (HEAD_DIM ** -0.5)
    o_a = jnp.zeros((seq_len, W_GROUP), F32)
    for h in range(H_GROUP):
        s = _dot_nt((qs * masks[h]).astype(BF16), kb)
        m = jnp.max(s, axis=-1, keepdims=True)
        e = jnp.exp(s - m)
        p = e / jnp.sum(e, axis=-1, keepdims=True)
        o_a = o_a + _dot(p.astype(BF16), vb) * masks[h]
    mix_ref[:, 0:W_GROUP] = o_a

    lg = _log_gamma_tile(dec_ref[...])
    rq, rv, r_g = part(3), part(5), part(6)
    rk = part(4) * (HEAD_DIM ** -0.5)
    o = _retention_pairs(rq, rk, rv, lg, masks, seq_len, seq_len)
    mix_ref[:, W_GROUP:2 * W_GROUP] = _retention_finish(o, r_g, avg_ref[...])

    pos = lax.broadcasted_iota(jnp.int32, (seq_len, W_GROUP), 0).astype(F32)
    vb_r = rv.astype(BF16)
    for d in range(2):
        lane_lg = _lane_vector(lg, d * H_GROUP, masks)
        expo = (seq_len - 1.0 - pos) if d == 0 else pos
        kd = rk * jnp.exp(lane_lg * expo)
        full = _dot(kd.T.astype(BF16), vb_r)
        for h in range(H_GROUP):
            st_ref[0, d, h] = full[h * HEAD_DIM:(h + 1) * HEAD_DIM, h * HEAD_DIM:(h + 1) * HEAD_DIM]

    o_c, o_d = _conv_mixers(part(7), part(8), part(9), part(10), part(11),
                            scw_ref, cfw_ref, cfg_ref, cfb_ref, seq_len)
    mix_ref[:, 2 * W_GROUP:3 * W_GROUP] = o_c
    mix_ref[:, 3 * W_GROUP:4 * W_GROUP] = o_d


def _ctx_mixers(proj, n_req, seq_len, dec_tile, avg, scw, cfw, cfg, cfb):
    n = n_req * seq_len
    const = lambda shape: pl.BlockSpec(shape, lambda b: (0,) * len(shape))
    return pl.pallas_call(
        functools.partial(_ctx_mixer_kernel, seq_len=seq_len),
        grid=(n_req,),
        in_specs=[
            pl.BlockSpec((seq_len, N_IN_PARTS * W_GROUP), lambda b: (b, 0)),
            const((8, LANES)), const((W_GROUP, W_GROUP)), const((8, W_GROUP)),
            const((32, W_GROUP)), const((1, W_GROUP)), const((1, W_GROUP)),
        ],
        out_specs=[
            pl.BlockSpec((seq_len, D_MODEL), lambda b: (b, 0)),
            pl.BlockSpec((seq_len, W_GROUP), lambda b: (b, 0)),
            pl.BlockSpec((seq_len, W_GROUP), lambda b: (b, 0)),
            pl.BlockSpec((1, 2, H_GROUP, HEAD_DIM, HEAD_DIM), lambda b: (b, 0, 0, 0, 0)),
        ],
        out_shape=[
            jax.ShapeDtypeStruct((n, D_MODEL), F32),
            jax.ShapeDtypeStruct((n, W_GROUP), F32),
            jax.ShapeDtypeStruct((n, W_GROUP), F32),
            jax.ShapeDtypeStruct((n_req, 2, H_GROUP, HEAD_DIM, HEAD_DIM), F32),
        ],
        compiler_params=_params("arbitrary"),
        name="context_mixers",
    )(proj, dec_tile, avg, scw, cfw, cfg, cfb)


def _lat_attn_kernel(q_ref, k_ref, v_ref, kc_ref, vc_ref, bias_ref, o_ref, *, rows):
    masks = _head_masks()
    kcb = kc_ref[0].astype(BF16)
    vcb = vc_ref[0].astype(BF16)
    wr = min(NA_WR, rows)

    def row_block(r, carry):
        rs = jnp.clip(r - wr // 2, 0, rows - wr)
        q0 = pl.multiple_of(r * GRID_W, GRID_W)
        k0 = pl.multiple_of(rs * GRID_W, GRID_W)
        qb = q_ref[pl.ds(q0, GRID_W), :] * (HEAD_DIM ** -0.5)
        qs = jnp.concatenate([qb * masks[h] for h in range(H_GROUP)], axis=0).astype(BF16)
        kl = k_ref[pl.ds(k0, wr * GRID_W), :].astype(BF16)
        vl = v_ref[pl.ds(k0, wr * GRID_W), :].astype(BF16)
        s_loc = _dot_nt(qs, kl) + bias_ref[r]
        s_ctx = _dot_nt(qs, kcb)
        m = jnp.maximum(jnp.max(s_loc, axis=-1, keepdims=True), jnp.max(s_ctx, axis=-1, keepdims=True))
        e_loc = jnp.exp(s_loc - m)
        e_ctx = jnp.exp(s_ctx - m)
        z = jnp.sum(e_loc, axis=-1, keepdims=True) + jnp.sum(e_ctx, axis=-1, keepdims=True)
        o = (_dot(e_loc.astype(BF16), vl) + _dot(e_ctx.astype(BF16), vcb)) / z
        acc = o[0:GRID_W] * masks[0]
        for h in range(1, H_GROUP):
            acc = acc + o[h * GRID_W:(h + 1) * GRID_W] * masks[h]
        o_ref[pl.ds(q0, GRID_W), :] = acc
        return carry

    lax.fori_loop(0, rows, row_block, 0)


def _lat_attention(proj, n_req, seq_len, kc, vc, bias):
    rows = seq_len // GRID_W
    part = lambda j: pl.BlockSpec((seq_len, W_GROUP), lambda b, j=j: (b, j))
    past = kc.shape[1]
    return pl.pallas_call(
        functools.partial(_lat_attn_kernel, rows=rows),
        grid=(n_req,),
        in_specs=[
            part(0), part(1), part(2),
            pl.BlockSpec((1, past, W_GROUP), lambda b: (b, 0, 0)),
            pl.BlockSpec((1, past, W_GROUP), lambda b: (b, 0, 0)),
            pl.BlockSpec(bias.shape, lambda b: (0, 0, 0)),
        ],
        out_specs=pl.BlockSpec((seq_len, W_GROUP), lambda b: (b, 0)),
        out_shape=jax.ShapeDtypeStruct((n_req * seq_len, W_GROUP), F32),
        compiler_params=_params("arbitrary"),
        name="latent_attention",
    )(proj, proj, proj, kc, vc, bias)


def _lat_ret_kernel(q_ref, k_ref, v_ref, g_ref, qs_ref, ks_ref, cos_ref, sin_ref, s0_ref,
                    dec_ref, avg_ref, o_ref, *, seq_len):
    masks = _head_masks()
    lg = _log_gamma_tile(dec_ref[...])
    cos = cos_ref[...]
    sin = sin_ref[...]
    rq = q_ref[...] * cos + qs_ref[...] * sin
    rk = (k_ref[...] * cos + ks_ref[...] * sin) * (HEAD_DIM ** -0.5)
    rv = v_ref[...]
    o = _retention_pairs(rq, rk, rv, lg, masks, seq_len, 256)
    pos = lax.broadcasted_iota(jnp.int32, (seq_len, W_GROUP), 0).astype(F32)
    rqb = rq.astype(BF16)
    lgf = _lane_vector(lg, 0, masks)
    lgb = _lane_vector(lg, H_GROUP, masks)
    o = o + _dot(rqb, s0_ref[0, 0].astype(BF16)) * jnp.exp(lgf * (pos + 1.0))
    o = o + _dot(rqb, s0_ref[0, 1].astype(BF16)) * jnp.exp(lgb * (seq_len - pos))
    o_ref[...] = _retention_finish(o, g_ref[...], avg_ref[...])


def _lat_retention(proj, n_req, seq_len, cos, sin, s0_bd, dec_tile, avg):
    part = lambda j: pl.BlockSpec((seq_len, W_GROUP), lambda b, j=j: (b, j))
    const = lambda shape: pl.BlockSpec(shape, lambda b: (0,) * len(shape))
    return pl.pallas_call(
        functools.partial(_lat_ret_kernel, seq_len=seq_len),
        grid=(n_req,),
        in_specs=[
            part(3), part(4), part(5), part(6), part(12), part(13),
            const((seq_len, W_GROUP)), const((seq_len, W_GROUP)),
            pl.BlockSpec((1, 2, W_GROUP, W_GROUP), lambda b: (b, 0, 0, 0)),
            const((8, LANES)), const((W_GROUP, W_GROUP)),
        ],
        out_specs=pl.BlockSpec((seq_len, W_GROUP), lambda b: (b, 0)),
        out_shape=jax.ShapeDtypeStruct((n_req * seq_len, W_GROUP), F32),
        compiler_params=_params("arbitrary"),
        name="latent_retention",
    )(proj, proj, proj, proj, proj, proj, cos, sin, s0_bd, dec_tile, avg)


def _lat_conv_kernel(b_ref, c_ref, x_ref, a_ref, gate_ref, scw_ref, cfw_ref, cfg_ref, cfb_ref,
                     o_ref, *, seq_len):
    o_c, o_d = _conv_mixers(b_ref[...], c_ref[...], x_ref[...], a_ref[...], gate_ref[...],
                            scw_ref, cfw_ref, cfg_ref, cfb_ref, seq_len)
    o_ref[:, 0:W_GROUP] = o_c
    o_ref[:, W_GROUP:2 * W_GROUP] = o_d


def _lat_convs(proj, n_req, seq_len, scw, cfw, cfg, cfb):
    part = lambda j: pl.BlockSpec((seq_len, W_GROUP), lambda b, j=j: (b, j))
    const = lambda shape: pl.BlockSpec(shape, lambda b: (0,) * len(shape))
    return pl.pallas_call(
        functools.partial(_lat_conv_kernel, seq_len=seq_len),
        grid=(n_req,),
        in_specs=[
            part(7), part(8), part(9), part(10), part(11),
            const((8, W_GROUP)), const((32, W_GROUP)), const((1, W_GROUP)), const((1, W_GROUP)),
        ],
        out_specs=pl.BlockSpec((seq_len, 2 * W_GROUP), lambda b: (b, 0)),
        out_shape=jax.ShapeDtypeStruct((n_req * seq_len, 2 * W_GROUP), F32),
        compiler_params=_params("arbitrary"),
        name="latent_convs",
    )(proj, proj, proj, proj, proj, scw, cfw, cfg, cfb)


def _outproj_kernel(*refs, widths):
    n_parts = len(widths)
    part_refs = refs[:n_parts]
    x_ref, g_ref, w_ref, lg_ref, lb_ref, o_ref = refs[n_parts:]
    mix = None
    row = 0
    for p_ref, wd in zip(part_refs, widths):
        term = _dot(p_ref[...].astype(BF16), w_ref[row:row + wd, :])
        mix = term if mix is None else mix + term
        row += wd
    y = ALPHA * x_ref[...] + g_ref[0] * mix
    o_ref[...] = _layernorm_rows(y, lg_ref[...], lb_ref[...])


def _outproj_ln(parts, x, gate, w_out, ln_g, ln_b, rows_per_cond):
    n, d = x.shape
    tb = TOKEN_BLOCK
    bpc = rows_per_cond // tb
    widths = tuple(p.shape[1] for p in parts)
    return pl.pallas_call(
        functools.partial(_outproj_kernel, widths=widths),
        grid=(n // tb,),
        in_specs=[pl.BlockSpec((tb, wd), lambda b: (b, 0)) for wd in widths] + [
            pl.BlockSpec((tb, d), lambda b: (b, 0)),
            pl.BlockSpec((1, 1, d), lambda b: (b // bpc, 0, 0)),
            pl.BlockSpec((d, d), lambda b: (0, 0)),
            pl.BlockSpec((1, d), lambda b: (0, 0)),
            pl.BlockSpec((1, d), lambda b: (0, 0)),
        ],
        out_specs=pl.BlockSpec((tb, d), lambda b: (b, 0)),
        out_shape=jax.ShapeDtypeStruct((n, d), F32),
        compiler_params=_params("arbitrary"),
        name="output_projection_ln",
    )(*parts, x, gate, w_out, ln_g, ln_b)


GATE_DTYPE = jnp.bfloat16


def _pair_words(x):
    bits = pltpu.bitcast(x.astype(F32), jnp.uint32)
    return bits | (bits >> 16)


def _packed_rows(word_row, rows):
    return pltpu.bitcast(jnp.broadcast_to(word_row, (rows // 2, word_row.shape[-1])), GATE_DTYPE)


def _bitonic_merge_desc(vals):
    vals = list(vals)
    n = len(vals)
    j = n // 2
    while j >= 1:
        for i in range(n):
            l = i ^ j
            if l > i:
                vals[i], vals[l] = jnp.maximum(vals[i], vals[l]), jnp.minimum(vals[i], vals[l])
        j //= 2
    return vals


def _bitonic_sort_desc(vals):
    vals = list(vals)
    n = len(vals)
    k = 2
    while k <= n:
        j = k // 2
        while j >= 1:
            for i in range(n):
                l = i ^ j
                if l > i:
                    hi, lo = jnp.maximum(vals[i], vals[l]), jnp.minimum(vals[i], vals[l])
                    vals[i], vals[l] = (hi, lo) if (i & k) == 0 else (lo, hi)
            j //= 2
        k *= 2
    return vals


def _extract_top(s, count, roll=None):
    roll = roll or (lambda v, r: pltpu.roll(v, r, 0))
    sub = 8
    assert s.shape[0] == sub * count
    slabs = _bitonic_sort_desc([s[sub * k:sub * (k + 1)] for k in range(count)])
    r = sub // 2
    while r >= 1:
        other = [roll(v, r) for v in slabs]
        slabs = _bitonic_merge_desc([jnp.maximum(slabs[k], other[count - 1 - k]) for k in range(count)])
        r //= 2
    return [v[0:1] for v in slabs]


def _peer_select_kernel(x_ref, sh_ref, sc_ref, wq_ref, keys_ref, hm_ref, a_ref, b_ref, tau_ref):
    hm = (x_ref[...] * (1.0 + sc_ref[0]) + sh_ref[0]).astype(BF16)
    hm_ref[...] = hm.T
    q = _dot(hm, wq_ref[...]).astype(BF16)
    dk = PEER_NKEYS
    as_gate = lambda v: v.astype(GATE_DTYPE).astype(F32)
    tops = [[None, None] for _ in range(PEER_HEADS)]
    for h in range(PEER_HEADS):
        for p in range(2):
            j = h * 2 + p
            s = _dot_nt(keys_ref[j], q[:, j * dk:(j + 1) * dk])
            top = _extract_top(s, PEER_TOPK)
            e = jnp.where(s >= top[-1], jnp.exp(s - top[0]), 0.0).astype(GATE_DTYPE)
            tops[h][p] = [as_gate(jnp.exp(t - top[0])) for t in top]
            if p == 0:
                a_ref[h] = _pair_words(e)
            else:
                b_ref[h] = e

    stack = lambda rows_per_head: jnp.concatenate(rows_per_head, axis=0)
    a_rank = [stack([tops[h][0][r] for h in range(PEER_HEADS)]) for r in range(PEER_TOPK)]
    b_rank = [stack([tops[h][1][r] for h in range(PEER_HEADS)]) for r in range(PEER_TOPK)]
    likely = [a_rank[r] * b_rank[c] for r in range(PEER_TOPK) for c in range(PEER_TOPK)
              if (r + 1) * (c + 1) <= PEER_TOPK]
    padded = likely + [jnp.full_like(likely[0], -1.0)] * (64 - len(likely))
    kth = _bitonic_sort_desc(padded)[PEER_TOPK - 1]
    a_all = jnp.stack(a_rank)[:, None]
    b_all = jnp.stack(b_rank)[None]
    cand = a_all * b_all
    chosen = cand >= kth
    z = jnp.sum(jnp.where(chosen, cand, 0.0), axis=(0, 1))
    rz = 1.0 / z
    scaled = (a_all.astype(GATE_DTYPE) * (b_all * rz).astype(GATE_DTYPE)).astype(F32)
    tau = jnp.min(jnp.where(chosen, scaled, jnp.inf), axis=(0, 1))
    tau_ref[...] = _pair_words(tau.astype(GATE_DTYPE))
    for h in range(PEER_HEADS):
        b_ref[h] = (b_ref[h].astype(F32) * rz[h:h + 1, :]).astype(GATE_DTYPE)


def _peer_select(x, sh, sc, wq, keys, rows_per_cond):
    n, d = x.shape
    tb = TOKEN_BLOCK
    bpc = rows_per_cond // tb
    vec = pl.BlockSpec((1, 1, d), lambda b: (b // bpc, 0, 0))
    grid_hn = lambda: pl.BlockSpec((PEER_HEADS, PEER_NKEYS, tb), lambda b: (0, 0, b))
    return pl.pallas_call(
        _peer_select_kernel,
        grid=(n // tb,),
        in_specs=[
            pl.BlockSpec((tb, d), lambda b: (b, 0)),
            vec, vec,
            pl.BlockSpec(wq.shape, lambda b: (0, 0)),
            pl.BlockSpec(keys.shape, lambda b: (0, 0, 0)),
        ],
        out_specs=[
            pl.BlockSpec((d, tb), lambda b: (0, b)),
            grid_hn(), grid_hn(),
            pl.BlockSpec((PEER_HEADS, tb), lambda b: (0, b)),
        ],
        out_shape=[
            jax.ShapeDtypeStruct((d, n), BF16),
            jax.ShapeDtypeStruct((PEER_HEADS, PEER_NKEYS, n), jnp.uint32),
            jax.ShapeDtypeStruct((PEER_HEADS, PEER_NKEYS, n), GATE_DTYPE),
            jax.ShapeDtypeStruct((PEER_HEADS, n), jnp.uint32),
        ],
        compiler_params=_params("arbitrary"),
        name="peer_select",
    )(x, sh, sc, wq, keys)


def _gelu(x):
    return 0.5 * x * (1.0 + lax.erf(x * (2.0 ** -0.5)))


def _peer_expert_kernel(hm_ref, a_ref, b_ref, tau_ref, u_ref, vt_ref, o_ref, acc_ref, ht_ref, gh_ref,
                        *, n_chunks, tok_block):
    c = pl.program_id(1)

    @pl.when(c == 0)
    def _():
        acc_ref[...] = jnp.zeros_like(acc_ref)

    rows_per_chunk = PEER_EXPERT_CHUNK // PEER_NKEYS
    group = PEER_TOKEN_GROUP
    n_groups = tok_block // group
    v_t = vt_ref[0]

    halves = PEER_EXPERT_SPLIT
    half_rows = rows_per_chunk // halves
    half = PEER_EXPERT_CHUNK // halves

    def pre_activations(g, eh):
        tok = pl.ds(g * group, group)
        ex = pl.ds(eh * half, half)
        ht_ref[ex, tok] = _dot(u_ref[0, ex, :], hm_ref[:, tok])

    def gated_activations(g, eh):
        for sub in range(group // LANES):
            tok = pl.ds(g * group + sub * LANES, LANES)
            for i in range(eh * half_rows, (eh + 1) * half_rows):
                rows = pl.ds(i * PEER_NKEYS, PEER_NKEYS)
                gate = jnp.zeros((PEER_NKEYS, LANES), GATE_DTYPE)
                for h in range(PEER_HEADS):
                    e = _packed_rows(a_ref[h, i:i + 1, tok], PEER_NKEYS) * b_ref[h, :, tok]
                    tau = _packed_rows(tau_ref[h:h + 1, tok], PEER_NKEYS)
                    gate = jnp.where(e >= tau, gate + e, gate)
                act = _gelu(ht_ref[rows, tok]).astype(GATE_DTYPE)
                gh_ref[rows, tok] = (gate * act).astype(BF16)

    def accumulate(g):
        tok = pl.ds(g * group, group)
        acc_ref[:, tok] += _dot(v_t, gh_ref[:, tok])

    units = [(g, eh) for g in range(n_groups) for eh in range(halves)]
    for k in range(len(units) + 1):
        if k < len(units):
            pre_activations(*units[k])
        if k >= 1:
            g, eh = units[k - 1]
            gated_activations(g, eh)
            if eh == halves - 1:
                accumulate(g)

    @pl.when(c == n_chunks - 1)
    def _():
        o_ref[...] = acc_ref[...].T


def _peer_experts(hm, a_t, b_t, tau, u_all, vt_all, layer):
    d, n = hm.shape
    tb = PEER_TOKEN_BLOCK
    ec = PEER_EXPERT_CHUNK
    n_chunks = N_EXPERTS // ec
    rpc = ec // PEER_NKEYS
    return pl.pallas_call(
        functools.partial(_peer_expert_kernel, n_chunks=n_chunks, tok_block=tb),
        grid=(n // tb, n_chunks),
        in_specs=[
            pl.BlockSpec((d, tb), lambda b, c: (0, b)),
            pl.BlockSpec((PEER_HEADS, rpc, tb), lambda b, c: (0, c, b)),
            pl.BlockSpec((PEER_HEADS, PEER_NKEYS, tb), lambda b, c: (0, 0, b)),
            pl.BlockSpec((PEER_HEADS, tb), lambda b, c: (0, b)),
            pl.BlockSpec((1, ec, d), lambda b, c: (layer, c, 0)),
            pl.BlockSpec((1, d, ec), lambda b, c: (layer, 0, c)),
        ],
        out_specs=pl.BlockSpec((tb, d), lambda b, c: (b, 0)),
        out_shape=jax.ShapeDtypeStruct((n, d), F32),
        scratch_shapes=[
            pltpu.VMEM((d, tb), F32),
            pltpu.VMEM((ec, tb), F32),
            pltpu.VMEM((ec, tb), BF16),
        ],
        compiler_params=_params("arbitrary", "arbitrary"),
        name="peer_experts",
    )(hm, a_t, b_t, tau, u_all, vt_all)


def _residual_ln_kernel(x_ref, f_ref, g_ref, lg_ref, lb_ref, o_ref):
    y = ALPHA * x_ref[...] + g_ref[0] * f_ref[...]
    o_ref[...] = _layernorm_rows(y, lg_ref[...], lb_ref[...])


def _residual_ln(x, f, gate, ln_g, ln_b, rows_per_cond):
    n, d = x.shape
    tb = TOKEN_BLOCK
    bpc = rows_per_cond // tb
    row = pl.BlockSpec((tb, d), lambda b: (b, 0))
    return pl.pallas_call(
        _residual_ln_kernel,
        grid=(n // tb,),
        in_specs=[row, row,
                  pl.BlockSpec((1, 1, d), lambda b: (b // bpc, 0, 0)),
                  pl.BlockSpec((1, d), lambda b: (0, 0)),
                  pl.BlockSpec((1, d), lambda b: (0, 0))],
        out_specs=row,
        out_shape=jax.ShapeDtypeStruct((n, d), F32),
        compiler_params=_params("arbitrary"),
        name="residual_ln",
    )(x, f, gate, ln_g, ln_b)


def _rope_tables(seq_len):
    t = np.arange(seq_len)
    row = (t // GRID_W).astype(np.float32)
    col = (t % GRID_W).astype(np.float32)
    n_freq = HEAD_DIM // 4
    inv = (ROPE_BASE ** (-np.arange(n_freq, dtype=np.float32) / n_freq)).astype(np.float32)
    ang = jnp.asarray(np.concatenate([row[:, None] * inv, col[:, None] * inv], -1).astype(np.float32))
    cos, sin = jnp.cos(ang), jnp.sin(ang)
    cos_t = jnp.tile(jnp.concatenate([cos, cos], -1), (1, H_GROUP))
    sin_t = jnp.tile(jnp.concatenate([-sin, sin], -1), (1, H_GROUP))
    return cos_t, sin_t


def _swap_halves_perm():
    idx = np.arange(W_GROUP)
    half = HEAD_DIM // 2
    return np.where((idx % HEAD_DIM) < half, idx + half, idx - half)


def _attention_bias(rpb, rows):
    wr = min(NA_WR, rows)
    cols = np.arange(GRID_W)
    c_start = np.clip(cols - NA_WC // 2, 0, GRID_W - NA_WC)
    col_in = (cols[None, :] >= c_start[:, None]) & (cols[None, :] < c_start[:, None] + NA_WC)
    col_idx = np.clip(cols[None, :] - cols[:, None], 1 - NA_WC, NA_WC - 1) + NA_WC - 1
    pick = jnp.asarray((col_idx[None] == np.arange(2 * NA_WC - 1)[:, None, None]).astype(np.float32))
    rpb_cols = jnp.einsum('hdj,jqk->hdqk', rpb, pick, precision=lax.Precision.HIGHEST)
    masked = jnp.where(jnp.asarray(col_in)[None, None], rpb_cols, NEG_BIG)
    tables = []
    for r in range(rows):
        rs = int(np.clip(r - wr // 2, 0, rows - wr))
        d0 = rs - r + NA_WR - 1
        blk = masked[:, d0:d0 + wr]
        tables.append(blk.transpose(0, 2, 1, 3).reshape(H_GROUP * GRID_W, wr * GRID_W))
    return jnp.stack(tables, axis=0)


def _block_diag_states(s0):
    b = s0.shape[0]
    eye = jnp.eye(H_GROUP, dtype=s0.dtype)
    full = jnp.einsum('bchde,hg->bchdge', s0, eye)
    return full.reshape(b, 2, H_GROUP * HEAD_DIM, H_GROUP * HEAD_DIM)


def kernel(x_prompt, x_sample, c, cache_na_k, cache_na_v, state_ret, c_ctx, w_mod, b_mod, w_in, w_out,
           na_rpb, ret_decay, sc_w, cf_w, cf_ln_g, cf_ln_b, ln1_g, ln1_b, ln2_g, ln2_b,
           peer_wq, peer_keys, peer_u, peer_v):
    n_req, seq, d = x_prompt.shape
    n_lat, lat_seq, _ = x_sample.shape
    past = cache_na_k.shape[2]
    rows = lat_seq // GRID_W

    cond8 = jnp.concatenate([c_ctx[None], c, jnp.zeros((8 - 1 - n_lat, d), F32)], axis=0)
    mod = _modulation(cond8, w_mod, b_mod)

    def mod_vec(l, j, lo, hi):
        return mod[l, lo:hi, j * d:(j + 1) * d][:, None, :]

    perm = _swap_halves_perm()
    w_in_b = w_in.astype(BF16)
    w_in_lat = jnp.concatenate([w_in_b, w_in_b[:, :, 3 * W_GROUP + perm], w_in_b[:, :, 4 * W_GROUP + perm]], -1)
    w_out_b = w_out.astype(BF16)
    wq_b = peer_wq.astype(BF16)
    keys_b = peer_keys.astype(BF16).reshape(DEPTH, PEER_HEADS * 2, PEER_NKEYS, PEER_NKEYS)
    u_b = peer_u.astype(BF16)
    vt_b = jnp.swapaxes(peer_v, 1, 2).astype(BF16)

    lane_h = np.arange(W_GROUP) // HEAD_DIM
    avg = jnp.asarray((lane_h[:, None] == lane_h[None, :]).astype(np.float32) / HEAD_DIM).astype(BF16)
    cos_t, sin_t = _rope_tables(lat_seq)
    dec_tiles = jnp.broadcast_to(ret_decay.reshape(DEPTH, 2 * H_GROUP, 1), (DEPTH, 8, LANES))
    scw = jnp.pad(sc_w, ((0, 0), (0, 8 - SC_WIDTH), (0, 0)))
    cfw = jnp.pad(cf_w, ((0, 0), (0, 32 - CF_WIDTH), (0, 0)))
    kc_all = cache_na_k.reshape(n_lat, DEPTH, past, W_GROUP)
    vc_all = cache_na_v.reshape(n_lat, DEPTH, past, W_GROUP)

    xp = x_prompt.reshape(n_req * seq, d)
    xs = x_sample.reshape(n_lat * lat_seq, d)
    ks_out, vs_out, ss_out = [], [], []

    for l in range(DEPTH):
        row1 = lambda a: a[l][None, :]
        cfg, cfb = row1(cf_ln_g), row1(cf_ln_b)
        l1g, l1b, l2g, l2b = row1(ln1_g), row1(ln1_b), row1(ln2_g), row1(ln2_b)

        n_ctx = n_req * seq
        proj = _modmm(xp, mod_vec(l, 0, 0, 1), mod_vec(l, 1, 0, 1), w_in_b[l], n_ctx)
        mix, k_ctx, v_ctx, s_ctx = _ctx_mixers(proj, n_req, seq, dec_tiles[l], avg, scw[l], cfw[l], cfg, cfb)
        xp = _outproj_ln([mix], xp, mod_vec(l, 2, 0, 1), w_out_b[l], l1g, l1b, n_ctx)
        hm, a_t, b_t, tau = _peer_select(xp, mod_vec(l, 3, 0, 1), mod_vec(l, 4, 0, 1), wq_b[l], keys_b[l], n_ctx)
        ffn = _peer_experts(hm, a_t, b_t, tau, u_b, vt_b, l)
        xp = _residual_ln(xp, ffn, mod_vec(l, 5, 0, 1), l2g, l2b, n_ctx)
        ks_out.append(k_ctx.reshape(n_req, seq, H_GROUP, HEAD_DIM))
        vs_out.append(v_ctx.reshape(n_req, seq, H_GROUP, HEAD_DIM))
        ss_out.append(s_ctx)

        hi = 1 + n_lat
        proj = _modmm(xs, mod_vec(l, 0, 1, hi), mod_vec(l, 1, 1, hi), w_in_lat[l], lat_seq)
        bias = _attention_bias(na_rpb[l], rows)
        o_a = _lat_attention(proj, n_lat, lat_seq, kc_all[:, l], vc_all[:, l], bias)
        o_b = _lat_retention(proj, n_lat, lat_seq, cos_t, sin_t, _block_diag_states(state_ret[:, l]),
                             dec_tiles[l], avg)
        o_cd = _lat_convs(proj, n_lat, lat_seq, scw[l], cfw[l], cfg, cfb)
        xs = _outproj_ln([o_a, o_b, o_cd], xs, mod_vec(l, 2, 1, hi), w_out_b[l], l1g, l1b, lat_seq)
        hm, a_t, b_t, tau = _peer_select(xs, mod_vec(l, 3, 1, hi), mod_vec(l, 4, 1, hi), wq_b[l], keys_b[l], lat_seq)
        ffn = _peer_experts(hm, a_t, b_t, tau, u_b, vt_b, l)
        xs = _residual_ln(xs, ffn, mod_vec(l, 5, 1, hi), l2g, l2b, lat_seq)

    return (xp.reshape(n_req, seq, d), xs.reshape(n_lat, lat_seq, d),
            jnp.stack(ks_out, axis=1), jnp.stack(vs_out, axis=1), jnp.stack(ss_out, axis=1))
```

```python
import functools
import math

import jax
import jax.numpy as jnp
import numpy as np
from jax import lax
from jax.experimental import pallas as pl
from jax.experimental.pallas import tpu as pltpu

F32 = jnp.float32
BF16 = jnp.bfloat16

D_MODEL = 1024
DEPTH = 4
GRID_W = 64
HEAD_DIM = 64
W_GROUP = 256
H_GROUP = 4
N_IN_PARTS = 12
NA_WR = 8
NA_WC = 16
ROPE_BASE = 10000.0
SC_WIDTH = 3
CF_WIDTH = 31
PEER_HEADS = 8
PEER_NKEYS = 128
PEER_TOPK = 16
N_EXPERTS = PEER_NKEYS * PEER_NKEYS
LN_EPS = 1e-5
ALPHA = (2.0 * DEPTH) ** 0.25
NEG_BIG = -1e30

VMEM_LIMIT_BYTES = 56 * 1024 * 1024
LANES = 128

TOKEN_BLOCK = 512
PEER_TOKEN_BLOCK = 1024
PEER_TOKEN_GROUP = 256
PEER_EXPERT_CHUNK = 2048
PEER_EXPERT_SPLIT = 1


def _params(*sem):
    return pltpu.CompilerParams(dimension_semantics=sem, vmem_limit_bytes=VMEM_LIMIT_BYTES)


def _sigmoid(x):
    return 1.0 / (1.0 + jnp.exp(-x))


def _silu(x):
    return x * _sigmoid(x)


def _layernorm_rows(x, g, b):
    mu = jnp.mean(x, axis=-1, keepdims=True)
    d = x - mu
    var = jnp.mean(d * d, axis=-1, keepdims=True)
    return d * lax.rsqrt(var + LN_EPS) * g + b


def _dot(a, b):
    return jnp.dot(a, b, preferred_element_type=F32)


def _dot_nt(a, b):
    return lax.dot_general(a, b, (((1,), (1,)), ((), ())), preferred_element_type=F32)


def _dot_f32(a, b):
    hi = a.astype(BF16)
    lo = (a - hi.astype(F32)).astype(BF16)
    return _dot(hi, b) + _dot(lo, b)


def _head_masks():
    lane = lax.broadcasted_iota(jnp.int32, (1, W_GROUP), 1)
    return [((lane >= h * HEAD_DIM) & (lane < (h + 1) * HEAD_DIM)).astype(F32) for h in range(H_GROUP)]


def _mod_kernel(c_ref, w_ref, b_ref, o_ref):
    c = c_ref[...]
    s = _silu(c)
    o_ref[0] = jnp.dot(s, w_ref[0], precision=lax.Precision.HIGHEST,
                       preferred_element_type=F32) + b_ref[0]


def _modulation(cond8, w_mod, b_mod):
    nc = 1024
    n_out = w_mod.shape[-1]
    return pl.pallas_call(
        _mod_kernel,
        grid=(DEPTH, n_out // nc),
        in_specs=[
            pl.BlockSpec((8, D_MODEL), lambda l, j: (0, 0)),
            pl.BlockSpec((1, D_MODEL, nc), lambda l, j: (l, 0, j)),
            pl.BlockSpec((1, 1, nc), lambda l, j: (l, 0, j)),
        ],
        out_specs=pl.BlockSpec((1, 8, nc), lambda l, j: (l, 0, j)),
        out_shape=jax.ShapeDtypeStruct((DEPTH, 8, n_out), F32),
        compiler_params=_params("arbitrary", "arbitrary"),
        name="modulation",
    )(cond8, w_mod, b_mod.reshape(DEPTH, 1, n_out))


def _modmm_kernel(x_ref, sh_ref, sc_ref, w_ref, o_ref):
    h = x_ref[...] * (1.0 + sc_ref[0]) + sh_ref[0]
    o_ref[...] = _dot(h.astype(BF16), w_ref[...])


def _modmm(x, sh, sc, w, rows_per_cond):
    n, d = x.shape
    n_out = w.shape[1]
    tb = TOKEN_BLOCK
    bpc = rows_per_cond // tb
    vec = pl.BlockSpec((1, 1, d), lambda b: (b // bpc, 0, 0))
    return pl.pallas_call(
        _modmm_kernel,
        grid=(n // tb,),
        in_specs=[
            pl.BlockSpec((tb, d), lambda b: (b, 0)),
            vec, vec,
            pl.BlockSpec((d, n_out), lambda b: (0, 0)),
        ],
        out_specs=pl.BlockSpec((tb, n_out), lambda b: (b, 0)),
        out_shape=jax.ShapeDtypeStruct((n, n_out), F32),
        compiler_params=_params("arbitrary"),
        name="modulated_projection",
    )(x, sh, sc, w)


def _log_gamma_tile(decay_tile):
    y = -decay_tile
    return -(jnp.maximum(y, 0.0) + jnp.log1p(jnp.exp(-jnp.abs(y))))


def _lane_vector(lg_tile, row0, masks):
    out = lg_tile[row0:row0 + 1, 0:1] * masks[0]
    for h in range(1, H_GROUP):
        out = out + lg_tile[row0 + h:row0 + h + 1, 0:1] * masks[h]
    return out


def _retention_pairs(rq, rk, rv, lg_tile, masks, seq_len, q_block):
    kb = rk.astype(BF16)
    vb = rv.astype(BF16)
    blocks = []
    for q0 in range(0, seq_len, q_block):
        rqb = rq[q0:q0 + q_block]
        ti = lax.broadcasted_iota(jnp.int32, (q_block, seq_len), 0) + q0
        si = lax.broadcasted_iota(jnp.int32, (q_block, seq_len), 1)
        dist = (ti - si).astype(F32)
        fwd = jnp.maximum(dist, 0.0)
        bwd = jnp.maximum(-dist, 0.0)
        acc = jnp.zeros((q_block, W_GROUP), F32)
        for h in range(H_GROUP):
            lgf = lg_tile[h:h + 1, 0:1]
            lgb = lg_tile[H_GROUP + h:H_GROUP + h + 1, 0:1]
            decay = (jnp.where(dist >= 0, jnp.exp(lgf * fwd), 0.0)
                     + jnp.where(dist <= 0, jnp.exp(lgb * bwd), 0.0))
            s = _dot_nt((rqb * masks[h]).astype(BF16), kb)
            acc = acc + _dot((s * decay).astype(BF16), vb) * masks[h]
        blocks.append(acc)
    return blocks[0] if len(blocks) == 1 else jnp.concatenate(blocks, axis=0)


def _retention_finish(o, r_g, avg):
    mu = _dot_f32(o, avg)
    d = o - mu
    var = _dot_f32(d * d, avg)
    return _silu(r_g) * (d * lax.rsqrt(var + LN_EPS))


def _shift_rows(z, off, seq_len):
    if off == 0:
        return z
    rolled = pltpu.roll(z, (-off) % seq_len, 0)
    t = lax.broadcasted_iota(jnp.int32, z.shape, 0)
    valid = (t + off >= 0) & (t + off < seq_len)
    return jnp.where(valid, rolled, 0.0)


def _dwconv(z, w_ref, width, seq_len):
    pad = (width - 1) // 2
    acc = None
    for k in range(width):
        term = _shift_rows(z, k - pad, seq_len) * w_ref[k:k + 1, :]
        acc = term if acc is None else acc + term
    return acc


def _conv_mixers(sc_b, sc_c, sc_x, cf_a, cf_gate, scw_ref, cfw_ref, cfg_ref, cfb_ref, seq_len):
    o_c = sc_b * _dwconv(sc_c * sc_x, scw_ref, SC_WIDTH, seq_len)
    u = _dwconv(cf_a * _sigmoid(cf_gate), cfw_ref, CF_WIDTH, seq_len)
    o_d = _silu(_layernorm_rows(u, cfg_ref[...], cfb_ref[...]))
    return o_c, o_d


def _ctx_mixer_kernel(p_ref, dec_ref, avg_ref, scw_ref, cfw_ref, cfg_ref, cfb_ref,
                      mix_ref, k_ref, v_ref, st_ref, *, seq_len):
    masks = _head_masks()
    part = lambda j: p_ref[:, j * W_GROUP:(j + 1) * W_GROUP]
    na_q, na_k, na_v = part(0), part(1), part(2)
    k_ref[...] = na_k
    v_ref[...] = na_v

    kb = na_k.astype(BF16)
    vb = na_v.astype(BF16)
    qs = na_q * (HEAD_DIM ** -0.5)
    o_a = jnp.zeros((seq_len, W_GROUP), F32)
    for h in range(H_GROUP):
        s = _dot_nt((qs * masks[h]).astype(BF16), kb)
        m = jnp.max(s, axis=-1, keepdims=True)
        e = jnp.exp(s - m)
        p = e / jnp.sum(e, axis=-1, keepdims=True)
        o_a = o_a + _dot(p.astype(BF16), vb) * masks[h]
    mix_ref[:, 0:W_GROUP] = o_a

    lg = _log_gamma_tile(dec_ref[...])
    rq, rv, r_g = part(3), part(5), part(6)
    rk = part(4) * (HEAD_DIM ** -0.5)
    o = _retention_pairs(rq, rk, rv, lg, masks, seq_len, seq_len)
    mix_ref[:, W_GROUP:2 * W_GROUP] = _retention_finish(o, r_g, avg_ref[...])

    pos = lax.broadcasted_iota(jnp.int32, (seq_len, W_GROUP), 0).astype(F32)
    vb_r = rv.astype(BF16)
    for d in range(2):
        lane_lg = _lane_vector(lg, d * H_GROUP, masks)
        expo = (seq_len - 1.0 - pos) if d == 0 else pos
        kd = rk * jnp.exp(lane_lg * expo)
        full = _dot(kd.T.astype(BF16), vb_r)
        for h in range(H_GROUP):
            st_ref[0, d, h] = full[h * HEAD_DIM:(h + 1) * HEAD_DIM, h * HEAD_DIM:(h + 1) * HEAD_DIM]

    o_c, o_d = _conv_mixers(part(7), part(8), part(9), part(10), part(11),
                            scw_ref, cfw_ref, cfg_ref, cfb_ref, seq_len)
    mix_ref[:, 2 * W_GROUP:3 * W_GROUP] = o_c
    mix_ref[:, 3 * W_GROUP:4 * W_GROUP] = o_d


def _ctx_mixers(proj, n_req, seq_len, dec_tile, avg, scw, cfw, cfg, cfb):
    n = n_req * seq_len
    const = lambda shape: pl.BlockSpec(shape, lambda b: (0,) * len(shape))
    return pl.pallas_call(
        functools.partial(_ctx_mixer_kernel, seq_len=seq_len),
        grid=(n_req,),
        in_specs=[
            pl.BlockSpec((seq_len, N_IN_PARTS * W_GROUP), lambda b: (b, 0)),
            const((8, LANES)), const((W_GROUP, W_GROUP)), const((8, W_GROUP)),
            const((32, W_GROUP)), const((1, W_GROUP)), const((1, W_GROUP)),
        ],
        out_specs=[
            pl.BlockSpec((seq_len, D_MODEL), lambda b: (b, 0)),
            pl.BlockSpec((seq_len, W_GROUP), lambda b: (b, 0)),
            pl.BlockSpec((seq_len, W_GROUP), lambda b: (b, 0)),
            pl.BlockSpec((1, 2, H_GROUP, HEAD_DIM, HEAD_DIM), lambda b: (b, 0, 0, 0, 0)),
        ],
        out_shape=[
            jax.ShapeDtypeStruct((n, D_MODEL), F32),
            jax.ShapeDtypeStruct((n, W_GROUP), F32),
            jax.ShapeDtypeStruct((n, W_GROUP), F32),
            jax.ShapeDtypeStruct((n_req, 2, H_GROUP, HEAD_DIM, HEAD_DIM), F32),
        ],
        compiler_params=_params("arbitrary"),
        name="context_mixers",
    )(proj, dec_tile, avg, scw, cfw, cfg, cfb)


def _lat_attn_kernel(q_ref, k_ref, v_ref, kc_ref, vc_ref, bias_ref, o_ref, *, rows):
    masks = _head_masks()
    kcb = kc_ref[0].astype(BF16)
    vcb = vc_ref[0].astype(BF16)
    wr = min(NA_WR, rows)

    def row_block(r, carry):
        rs = jnp.clip(r - wr // 2, 0, rows - wr)
        q0 = pl.multiple_of(r * GRID_W, GRID_W)
        k0 = pl.multiple_of(rs * GRID_W, GRID_W)
        qb = q_ref[pl.ds(q0, GRID_W), :] * (HEAD_DIM ** -0.5)
        qs = jnp.concatenate([qb * masks[h] for h in range(H_GROUP)], axis=0).astype(BF16)
        kl = k_ref[pl.ds(k0, wr * GRID_W), :].astype(BF16)
        vl = v_ref[pl.ds(k0, wr * GRID_W), :].astype(BF16)
        d0 = rs - r + (NA_WR - 1)
        off = pl.multiple_of((d0 // 2) * (2 * GRID_W), 2 * GRID_W)
        width = wr * GRID_W
        bias = jnp.where(d0 % 2 == 0, bias_ref[0, :, pl.ds(off, width)], bias_ref[1, :, pl.ds(off, width)])
        s_loc = _dot_nt(qs, kl) + bias
        s_ctx = _dot_nt(qs, kcb)
        m = jnp.maximum(jnp.max(s_loc, axis=-1, keepdims=True), jnp.max(s_ctx, axis=-1, keepdims=True))
        e_loc = jnp.exp(s_loc - m)
        e_ctx = jnp.exp(s_ctx - m)
        z = jnp.sum(e_loc, axis=-1, keepdims=True) + jnp.sum(e_ctx, axis=-1, keepdims=True)
        o = (_dot(e_loc.astype(BF16), vl) + _dot(e_ctx.astype(BF16), vcb)) / z
        acc = o[0:GRID_W] * masks[0]
        for h in range(1, H_GROUP):
            acc = acc + o[h * GRID_W:(h + 1) * GRID_W] * masks[h]
        o_ref[pl.ds(q0, GRID_W), :] = acc
        return carry

    lax.fori_loop(0, rows, row_block, 0)


def _lat_attention(proj, n_req, seq_len, kc, vc, bias):
    rows = seq_len // GRID_W
    part = lambda j: pl.BlockSpec((seq_len, W_GROUP), lambda b, j=j: (b, j))
    past = kc.shape[1]
    return pl.pallas_call(
        functools.partial(_lat_attn_kernel, rows=rows),
        grid=(n_req,),
        in_specs=[
            part(0), part(1), part(2),
            pl.BlockSpec((1, past, W_GROUP), lambda b: (b, 0, 0)),
            pl.BlockSpec((1, past, W_GROUP), lambda b: (b, 0, 0)),
            pl.BlockSpec(bias.shape, lambda b: (0, 0, 0)),
        ],
        out_specs=pl.BlockSpec((seq_len, W_GROUP), lambda b: (b, 0)),
        out_shape=jax.ShapeDtypeStruct((n_req * seq_len, W_GROUP), F32),
        compiler_params=_params("arbitrary"),
        name="latent_attention",
    )(proj, proj, proj, kc, vc, bias)


def _lat_ret_kernel(q_ref, k_ref, v_ref, g_ref, qs_ref, ks_ref, cos_ref, sin_ref, s0_ref,
                    dec_ref, avg_ref, o_ref, *, seq_len):
    masks = _head_masks()
    lg = _log_gamma_tile(dec_ref[...])
    cos = cos_ref[...]
    sin = sin_ref[...]
    rq = q_ref[...] * cos + qs_ref[...] * sin
    rk = (k_ref[...] * cos + ks_ref[...] * sin) * (HEAD_DIM ** -0.5)
    rv = v_ref[...]
    o = _retention_pairs(rq, rk, rv, lg, masks, seq_len, 256)
    pos = lax.broadcasted_iota(jnp.int32, (seq_len, W_GROUP), 0).astype(F32)
    rqb = rq.astype(BF16)
    lgf = _lane_vector(lg, 0, masks)
    lgb = _lane_vector(lg, H_GROUP, masks)
    o = o + _dot(rqb, s0_ref[0, 0].astype(BF16)) * jnp.exp(lgf * (pos + 1.0))
    o = o + _dot(rqb, s0_ref[0, 1].astype(BF16)) * jnp.exp(lgb * (seq_len - pos))
    o_ref[...] = _retention_finish(o, g_ref[...], avg_ref[...])


def _lat_retention(proj, n_req, seq_len, cos, sin, s0_bd, dec_tile, avg):
    part = lambda j: pl.BlockSpec((seq_len, W_GROUP), lambda b, j=j: (b, j))
    const = lambda shape: pl.BlockSpec(shape, lambda b: (0,) * len(shape))
    return pl.pallas_call(
        functools.partial(_lat_ret_kernel, seq_len=seq_len),
        grid=(n_req,),
        in_specs=[
            part(3), part(4), part(5), part(6), part(12), part(13),
            const((seq_len, W_GROUP)), const((seq_len, W_GROUP)),
            pl.BlockSpec((1, 2, W_GROUP, W_GROUP), lambda b: (b, 0, 0, 0)),
            const((8, LANES)), const((W_GROUP, W_GROUP)),
        ],
        out_specs=pl.BlockSpec((seq_len, W_GROUP), lambda b: (b, 0)),
        out_shape=jax.ShapeDtypeStruct((n_req * seq_len, W_GROUP), F32),
        compiler_params=_params("arbitrary"),
        name="latent_retention",
    )(proj, proj, proj, proj, proj, proj, cos, sin, s0_bd, dec_tile, avg)


def _lat_conv_kernel(b_ref, c_ref, x_ref, a_ref, gate_ref, scw_ref, cfw_ref, cfg_ref, cfb_ref,
                     o_ref, *, seq_len):
    o_c, o_d = _conv_mixers(b_ref[...], c_ref[...], x_ref[...], a_ref[...], gate_ref[...],
                            scw_ref, cfw_ref, cfg_ref, cfb_ref, seq_len)
    o_ref[:, 0:W_GROUP] = o_c
    o_ref[:, W_GROUP:2 * W_GROUP] = o_d


def _lat_convs(proj, n_req, seq_len, scw, cfw, cfg, cfb):
    part = lambda j: pl.BlockSpec((seq_len, W_GROUP), lambda b, j=j: (b, j))
    const = lambda shape: pl.BlockSpec(shape, lambda b: (0,) * len(shape))
    return pl.pallas_call(
        functools.partial(_lat_conv_kernel, seq_len=seq_len),
        grid=(n_req,),
        in_specs=[
            part(7), part(8), part(9), part(10), part(11),
            const((8, W_GROUP)), const((32, W_GROUP)), const((1, W_GROUP)), const((1, W_GROUP)),
        ],
        out_specs=pl.BlockSpec((seq_len, 2 * W_GROUP), lambda b: (b, 0)),
        out_shape=jax.ShapeDtypeStruct((n_req * seq_len, 2 * W_GROUP), F32),
        compiler_params=_params("arbitrary"),
        name="latent_convs",
    )(proj, proj, proj, proj, proj, scw, cfw, cfg, cfb)


def _outproj_kernel(*refs, widths):
    n_parts = len(widths)
    part_refs = refs[:n_parts]
    x_ref, g_ref, w_ref, lg_ref, lb_ref, o_ref = refs[n_parts:]
    mix = None
    row = 0
    for p_ref, wd in zip(part_refs, widths):
        term = _dot(p_ref[...].astype(BF16), w_ref[row:row + wd, :])
        mix = term if mix is None else mix + term
        row += wd
    y = ALPHA * x_ref[...] + g_ref[0] * mix
    o_ref[...] = _layernorm_rows(y, lg_ref[...], lb_ref[...])


def _outproj_ln(parts, x, gate, w_out, ln_g, ln_b, rows_per_cond):
    n, d = x.shape
    tb = TOKEN_BLOCK
    bpc = rows_per_cond // tb
    widths = tuple(p.shape[1] for p in parts)
    return pl.pallas_call(
        functools.partial(_outproj_kernel, widths=widths),
        grid=(n // tb,),
        in_specs=[pl.BlockSpec((tb, wd), lambda b: (b, 0)) for wd in widths] + [
            pl.BlockSpec((tb, d), lambda b: (b, 0)),
            pl.BlockSpec((1, 1, d), lambda b: (b // bpc, 0, 0)),
            pl.BlockSpec((d, d), lambda b: (0, 0)),
            pl.BlockSpec((1, d), lambda b: (0, 0)),
            pl.BlockSpec((1, d), lambda b: (0, 0)),
        ],
        out_specs=pl.BlockSpec((tb, d), lambda b: (b, 0)),
        out_shape=jax.ShapeDtypeStruct((n, d), F32),
        compiler_params=_params("arbitrary"),
        name="output_projection_ln",
    )(*parts, x, gate, w_out, ln_g, ln_b)


GATE_DTYPE = jnp.bfloat16


def _pair_words(x):
    bits = pltpu.bitcast(x.astype(F32), jnp.uint32)
    return bits | (bits >> 16)


def _packed_rows(word_row, rows):
    return pltpu.bitcast(jnp.broadcast_to(word_row, (rows // 2, word_row.shape[-1])), GATE_DTYPE)


def _bitonic_merge_desc(vals):
    vals = list(vals)
    n = len(vals)
    j = n // 2
    while j >= 1:
        for i in range(n):
            l = i ^ j
            if l > i:
                vals[i], vals[l] = jnp.maximum(vals[i], vals[l]), jnp.minimum(vals[i], vals[l])
        j //= 2
    return vals


def _bitonic_sort_desc(vals):
    vals = list(vals)
    n = len(vals)
    k = 2
    while k <= n:
        j = k // 2
        while j >= 1:
            for i in range(n):
                l = i ^ j
                if l > i:
                    hi, lo = jnp.maximum(vals[i], vals[l]), jnp.minimum(vals[i], vals[l])
                    vals[i], vals[l] = (hi, lo) if (i & k) == 0 else (lo, hi)
            j //= 2
        k *= 2
    return vals


def _extract_top(s, count, roll=None):
    roll = roll or (lambda v, r: pltpu.roll(v, r, 0))
    sub = 8
    assert s.shape[0] == sub * count
    slabs = _bitonic_sort_desc([s[sub * k:sub * (k + 1)] for k in range(count)])
    r = sub // 2
    while r >= 1:
        other = [roll(v, r) for v in slabs]
        slabs = _bitonic_merge_desc([jnp.maximum(slabs[k], other[count - 1 - k]) for k in range(count)])
        r //= 2
    return [v[0:1] for v in slabs]


def _peer_select_kernel(x_ref, sh_ref, sc_ref, wq_ref, keys_ref, hm_ref, a_ref, b_ref, tau_ref):
    hm = (x_ref[...] * (1.0 + sc_ref[0]) + sh_ref[0]).astype(BF16)
    hm_ref[...] = hm.T
    q = _dot(hm, wq_ref[...]).astype(BF16)
    dk = PEER_NKEYS
    as_gate = lambda v: v.astype(GATE_DTYPE).astype(F32)
    tops = [[None, None] for _ in range(PEER_HEADS)]
    for h in range(PEER_HEADS):
        for p in range(2):
            j = h * 2 + p
            s = _dot_nt(keys_ref[j], q[:, j * dk:(j + 1) * dk])
            top = _extract_top(s, PEER_TOPK)
            e = jnp.where(s >= top[-1], jnp.exp(s - top[0]), 0.0).astype(GATE_DTYPE)
            tops[h][p] = [as_gate(jnp.exp(t - top[0])) for t in top]
            if p == 0:
                a_ref[h] = _pair_words(e)
            else:
                b_ref[h] = e

    stack = lambda rows_per_head: jnp.concatenate(rows_per_head, axis=0)
    a_rank = [stack([tops[h][0][r] for h in range(PEER_HEADS)]) for r in range(PEER_TOPK)]
    b_rank = [stack([tops[h][1][r] for h in range(PEER_HEADS)]) for r in range(PEER_TOPK)]
    likely = [a_rank[r] * b_rank[c] for r in range(PEER_TOPK) for c in range(PEER_TOPK)
              if (r + 1) * (c + 1) <= PEER_TOPK]
    padded = likely + [jnp.full_like(likely[0], -1.0)] * (64 - len(likely))
    kth = _bitonic_sort_desc(padded)[PEER_TOPK - 1]
    a_all = jnp.stack(a_rank)[:, None]
    b_all = jnp.stack(b_rank)[None]
    cand = a_all * b_all
    chosen = cand >= kth
    z = jnp.sum(jnp.where(chosen, cand, 0.0), axis=(0, 1))
    rz = 1.0 / z
    scaled = (a_all.astype(GATE_DTYPE) * (b_all * rz).astype(GATE_DTYPE)).astype(F32)
    tau = jnp.min(jnp.where(chosen, scaled, jnp.inf), axis=(0, 1))
    tau_ref[...] = _pair_words(tau.astype(GATE_DTYPE))
    for h in range(PEER_HEADS):
        b_ref[h] = (b_ref[h].astype(F32) * rz[h:h + 1, :]).astype(GATE_DTYPE)


def _peer_select(x, sh, sc, wq, keys, rows_per_cond):
    n, d = x.shape
    tb = TOKEN_BLOCK
    bpc = rows_per_cond // tb
    vec = pl.BlockSpec((1, 1, d), lambda b: (b // bpc, 0, 0))
    grid_hn = lambda: pl.BlockSpec((PEER_HEADS, PEER_NKEYS, tb), lambda b: (0, 0, b))
    return pl.pallas_call(
        _peer_select_kernel,
        grid=(n // tb,),
        in_specs=[
            pl.BlockSpec((tb, d), lambda b: (b, 0)),
            vec, vec,
            pl.BlockSpec(wq.shape, lambda b: (0, 0)),
            pl.BlockSpec(keys.shape, lambda b: (0, 0, 0)),
        ],
        out_specs=[
            pl.BlockSpec((d, tb), lambda b: (0, b)),
            grid_hn(), grid_hn(),
            pl.BlockSpec((PEER_HEADS, tb), lambda b: (0, b)),
        ],
        out_shape=[
            jax.ShapeDtypeStruct((d, n), BF16),
            jax.ShapeDtypeStruct((PEER_HEADS, PEER_NKEYS, n), jnp.uint32),
            jax.ShapeDtypeStruct((PEER_HEADS, PEER_NKEYS, n), GATE_DTYPE),
            jax.ShapeDtypeStruct((PEER_HEADS, n), jnp.uint32),
        ],
        compiler_params=_params("arbitrary"),
        name="peer_select",
    )(x, sh, sc, wq, keys)


def _gelu(x):
    return 0.5 * x * (1.0 + lax.erf(x * (2.0 ** -0.5)))


def _peer_expert_kernel(hm_ref, a_ref, b_ref, tau_ref, u_ref, v_ref, o_ref, acc_ref, ht_ref, gh_ref,
                        *, n_chunks, tok_block):
    c = pl.program_id(1)

    @pl.when(c == 0)
    def _():
        acc_ref[...] = jnp.zeros_like(acc_ref)

    rows_per_chunk = PEER_EXPERT_CHUNK // PEER_NKEYS
    group = PEER_TOKEN_GROUP
    n_groups = tok_block // group
    v_t = v_ref[0]

    halves = PEER_EXPERT_SPLIT
    half_rows = rows_per_chunk // halves
    half = PEER_EXPERT_CHUNK // halves

    def pre_activations(g, eh):
        tok = pl.ds(g * group, group)
        ex = pl.ds(eh * half, half)
        ht_ref[ex, tok] = _dot(u_ref[0, ex, :], hm_ref[:, tok])

    def gated_activations(g, eh):
        for sub in range(group // LANES):
            tok = pl.ds(g * group + sub * LANES, LANES)
            for i in range(eh * half_rows, (eh + 1) * half_rows):
                rows = pl.ds(i * PEER_NKEYS, PEER_NKEYS)
                gate = jnp.zeros((PEER_NKEYS, LANES), GATE_DTYPE)
                for h in range(PEER_HEADS):
                    e = _packed_rows(a_ref[h, i:i + 1, tok], PEER_NKEYS) * b_ref[h, :, tok]
                    tau = _packed_rows(tau_ref[h:h + 1, tok], PEER_NKEYS)
                    gate = jnp.where(e >= tau, gate + e, gate)
                act = _gelu(ht_ref[rows, tok]).astype(GATE_DTYPE)
                gh_ref[rows, tok] = (gate * act).astype(BF16)

    def accumulate(g):
        tok = pl.ds(g * group, group)
        acc_ref[:, tok] += _dot(v_t, gh_ref[:, tok])

    units = [(g, eh) for g in range(n_groups) for eh in range(halves)]
    for k in range(len(units) + 1):
        if k < len(units):
            pre_activations(*units[k])
        if k >= 1:
            g, eh = units[k - 1]
            gated_activations(g, eh)
            if eh == halves - 1:
                accumulate(g)

    @pl.when(c == n_chunks - 1)
    def _():
        o_ref[...] = acc_ref[...].T


def _peer_experts(hm, a_t, b_t, tau, u_all, v_all, layer):
    d, n = hm.shape
    tb = PEER_TOKEN_BLOCK
    ec = PEER_EXPERT_CHUNK
    n_chunks = N_EXPERTS // ec
    rpc = ec // PEER_NKEYS
    return pl.pallas_call(
        functools.partial(_peer_expert_kernel, n_chunks=n_chunks, tok_block=tb),
        grid=(n // tb, n_chunks),
        in_specs=[
            pl.BlockSpec((d, tb), lambda b, c: (0, b)),
            pl.BlockSpec((PEER_HEADS, rpc, tb), lambda b, c: (0, c, b)),
            pl.BlockSpec((PEER_HEADS, PEER_NKEYS, tb), lambda b, c: (0, 0, b)),
            pl.BlockSpec((PEER_HEADS, tb), lambda b, c: (0, b)),
            pl.BlockSpec((1, ec, d), lambda b, c: (layer, c, 0)),
            pl.BlockSpec((1, d, ec), lambda b, c: (layer, 0, c)),
        ],
        out_specs=pl.BlockSpec((tb, d), lambda b, c: (b, 0)),
        out_shape=jax.ShapeDtypeStruct((n, d), F32),
        scratch_shapes=[
            pltpu.VMEM((d, tb), F32),
            pltpu.VMEM((ec, tb), F32),
            pltpu.VMEM((ec, tb), BF16),
        ],
        compiler_params=_params("arbitrary", "arbitrary"),
        name="peer_experts",
    )(hm, a_t, b_t, tau, u_all, v_all)


def _residual_ln_kernel(x_ref, f_ref, g_ref, lg_ref, lb_ref, o_ref):
    y = ALPHA * x_ref[...] + g_ref[0] * f_ref[...]
    o_ref[...] = _layernorm_rows(y, lg_ref[...], lb_ref[...])


def _residual_ln(x, f, gate, ln_g, ln_b, rows_per_cond):
    n, d = x.shape
    tb = TOKEN_BLOCK
    bpc = rows_per_cond // tb
    row = pl.BlockSpec((tb, d), lambda b: (b, 0))
    return pl.pallas_call(
        _residual_ln_kernel,
        grid=(n // tb,),
        in_specs=[row, row,
                  pl.BlockSpec((1, 1, d), lambda b: (b // bpc, 0, 0)),
                  pl.BlockSpec((1, d), lambda b: (0, 0)),
                  pl.BlockSpec((1, d), lambda b: (0, 0))],
        out_specs=row,
        out_shape=jax.ShapeDtypeStruct((n, d), F32),
        compiler_params=_params("arbitrary"),
        name="residual_ln",
    )(x, f, gate, ln_g, ln_b)


def _rope_tables(seq_len):
    t = np.arange(seq_len)
    row = (t // GRID_W).astype(np.float32)
    col = (t % GRID_W).astype(np.float32)
    n_freq = HEAD_DIM // 4
    inv = (ROPE_BASE ** (-np.arange(n_freq, dtype=np.float32) / n_freq)).astype(np.float32)
    ang = jnp.asarray(np.concatenate([row[:, None] * inv, col[:, None] * inv], -1).astype(np.float32))
    cos, sin = jnp.cos(ang), jnp.sin(ang)
    cos_t = jnp.tile(jnp.concatenate([cos, cos], -1), (1, H_GROUP))
    sin_t = jnp.tile(jnp.concatenate([-sin, sin], -1), (1, H_GROUP))
    return cos_t, sin_t


def _swap_halves_perm():
    idx = np.arange(W_GROUP)
    half = HEAD_DIM // 2
    return np.where((idx % HEAD_DIM) < half, idx + half, idx - half)


def _attention_bias(rpb, rows):
    cols = np.arange(GRID_W)
    c_start = np.clip(cols - NA_WC // 2, 0, GRID_W - NA_WC)
    col_in = (cols[None, :] >= c_start[:, None]) & (cols[None, :] < c_start[:, None] + NA_WC)
    col_idx = np.clip(cols[None, :] - cols[:, None], 1 - NA_WC, NA_WC - 1) + NA_WC - 1
    pick = jnp.asarray((col_idx[None] == np.arange(2 * NA_WC - 1)[:, None, None]).astype(np.float32))
    rpb_cols = jnp.einsum('hdj,jqk->hdqk', rpb, pick, precision=lax.Precision.HIGHEST)
    masked = jnp.where(jnp.asarray(col_in)[None, None], rpb_cols, NEG_BIG)
    n_rel = 2 * NA_WR - 1
    flat = masked.transpose(0, 2, 1, 3).reshape(H_GROUP * GRID_W, n_rel * GRID_W)
    wide = jnp.pad(flat, ((0, 0), (0, (n_rel + 3) * GRID_W - flat.shape[1])))
    keep = (n_rel + 1) * GRID_W
    return jnp.stack([wide[:, :keep], wide[:, GRID_W:GRID_W + keep]], axis=0)


def _block_diag_states(s0):
    b = s0.shape[0]
    eye = jnp.eye(H_GROUP, dtype=s0.dtype)
    full = jnp.einsum('bchde,hg->bchdge', s0, eye)
    return full.reshape(b, 2, H_GROUP * HEAD_DIM, H_GROUP * HEAD_DIM)


def kernel(x_prompt, x_sample, c, cache_na_k, cache_na_v, state_ret, c_ctx, w_mod, b_mod, w_in, w_out,
           na_rpb, ret_decay, sc_w, cf_w, cf_ln_g, cf_ln_b, ln1_g, ln1_b, ln2_g, ln2_b,
           peer_wq, peer_keys, peer_u, peer_v):
    n_req, seq, d = x_prompt.shape
    n_lat, lat_seq, _ = x_sample.shape
    past = cache_na_k.shape[2]
    rows = lat_seq // GRID_W

    cond8 = jnp.concatenate([c_ctx[None], c, jnp.zeros((8 - 1 - n_lat, d), F32)], axis=0)
    mod = _modulation(cond8, w_mod, b_mod)

    def mod_vec(l, j, lo, hi):
        return mod[l, lo:hi, j * d:(j + 1) * d][:, None, :]

    perm = _swap_halves_perm()
    w_in_b = w_in.astype(BF16)
    w_in_lat = jnp.concatenate([w_in_b, w_in_b[:, :, 3 * W_GROUP + perm], w_in_b[:, :, 4 * W_GROUP + perm]], -1)
    w_out_b = w_out.astype(BF16)
    wq_b = peer_wq.astype(BF16)
    keys_b = peer_keys.astype(BF16).reshape(DEPTH, PEER_HEADS * 2, PEER_NKEYS, PEER_NKEYS)
    u_b = peer_u.astype(BF16)
    v_b = jnp.swapaxes(peer_v, 1, 2).astype(BF16)

    lane_h = np.arange(W_GROUP) // HEAD_DIM
    avg = jnp.asarray((lane_h[:, None] == lane_h[None, :]).astype(np.float32) / HEAD_DIM).astype(BF16)
    cos_t, sin_t = _rope_tables(lat_seq)
    dec_tiles = jnp.broadcast_to(ret_decay.reshape(DEPTH, 2 * H_GROUP, 1), (DEPTH, 8, LANES))
    scw = jnp.pad(sc_w, ((0, 0), (0, 8 - SC_WIDTH), (0, 0)))
    cfw = jnp.pad(cf_w, ((0, 0), (0, 32 - CF_WIDTH), (0, 0)))
    kc_all = cache_na_k.reshape(n_lat, DEPTH, past, W_GROUP)
    vc_all = cache_na_v.reshape(n_lat, DEPTH, past, W_GROUP)

    xp = x_prompt.reshape(n_req * seq, d)
    xs = x_sample.reshape(n_lat * lat_seq, d)
    ks_out, vs_out, ss_out = [], [], []

    for l in range(DEPTH):
        row1 = lambda a: a[l][None, :]
        cfg, cfb = row1(cf_ln_g), row1(cf_ln_b)
        l1g, l1b, l2g, l2b = row1(ln1_g), row1(ln1_b), row1(ln2_g), row1(ln2_b)

        n_ctx = n_req * seq
        proj = _modmm(xp, mod_vec(l, 0, 0, 1), mod_vec(l, 1, 0, 1), w_in_b[l], n_ctx)
        mix, k_ctx, v_ctx, s_ctx = _ctx_mixers(proj, n_req, seq, dec_tiles[l], avg, scw[l], cfw[l], cfg, cfb)
        xp = _outproj_ln([mix], xp, mod_vec(l, 2, 0, 1), w_out_b[l], l1g, l1b, n_ctx)
        hm, a_t, b_t, tau = _peer_select(xp, mod_vec(l, 3, 0, 1), mod_vec(l, 4, 0, 1), wq_b[l], keys_b[l], n_ctx)
        ffn = _peer_experts(hm, a_t, b_t, tau, u_b, v_b, l)
        xp = _residual_ln(xp, ffn, mod_vec(l, 5, 0, 1), l2g, l2b, n_ctx)
        ks_out.append(k_ctx.reshape(n_req, seq, H_GROUP, HEAD_DIM))
        vs_out.append(v_ctx.reshape(n_req, seq, H_GROUP, HEAD_DIM))
        ss_out.append(s_ctx)

        hi = 1 + n_lat
        proj = _modmm(xs, mod_vec(l, 0, 1, hi), mod_vec(l, 1, 1, hi), w_in_lat[l], lat_seq)
        bias = _attention_bias(na_rpb[l], rows)
        o_a = _lat_attention(proj, n_lat, lat_seq, kc_all[:, l], vc_all[:, l], bias)
        o_b = _lat_retention(proj, n_lat, lat_seq, cos_t, sin_t, _block_diag_states(state_ret[:, l]),
                             dec_tiles[l], avg)
        o_cd = _lat_convs(proj, n_lat, lat_seq, scw[l], cfw[l], cfg, cfb)
        xs = _outproj_ln([o_a, o_b, o_cd], xs, mod_vec(l, 2, 1, hi), w_out_b[l], l1g, l1b, lat_seq)
        hm, a_t, b_t, tau = _peer_select(xs, mod_vec(l, 3, 1, hi), mod_vec(l, 4, 1, hi), wq_b[l], keys_b[l], lat_seq)
        ffn = _peer_experts(hm, a_t, b_t, tau, u_b, v_b, l)
        xs = _residual_ln(xs, ffn, mod_vec(l, 5, 1, hi), l2g, l2b, lat_seq)

    return (xp.reshape(n_req, seq, d), xs.reshape(n_lat, lat_seq, d),
            jnp.stack(ks_out, axis=1), jnp.stack(vs_out, axis=1), jnp.stack(ss_out, axis=1))
```

```python
import functools
import math

import jax
import jax.numpy as jnp
import numpy as np
from jax import lax
from jax.experimental import pallas as pl
from jax.experimental.pallas import tpu as pltpu

F32 = jnp.float32
BF16 = jnp.bfloat16

D_MODEL = 1024
DEPTH = 4
GRID_W = 64
HEAD_DIM = 64
W_GROUP = 256
H_GROUP = 4
N_IN_PARTS = 12
NA_WR = 8
NA_WC = 16
ROPE_BASE = 10000.0
SC_WIDTH = 3
CF_WIDTH = 31
PEER_HEADS = 8
PEER_NKEYS = 128
PEER_TOPK = 16
N_EXPERTS = PEER_NKEYS * PEER_NKEYS
LN_EPS = 1e-5
ALPHA = (2.0 * DEPTH) ** 0.25
NEG_BIG = -1e30

VMEM_LIMIT_BYTES = 56 * 1024 * 1024
LANES = 128

TOKEN_BLOCK = 512
PEER_TOKEN_BLOCK = 1024
PEER_TOKEN_GROUP = 256
PEER_EXPERT_CHUNK = 2048
PEER_EXPERT_SPLIT = 1


def _params(*sem):
    return pltpu.CompilerParams(dimension_semantics=sem, vmem_limit_bytes=VMEM_LIMIT_BYTES)


def _sigmoid(x):
    return 1.0 / (1.0 + jnp.exp(-x))


def _silu(x):
    return x * _sigmoid(x)


def _layernorm_rows(x, g, b):
    mu = jnp.mean(x, axis=-1, keepdims=True)
    d = x - mu
    var = jnp.mean(d * d, axis=-1, keepdims=True)
    return d * lax.rsqrt(var + LN_EPS) * g + b


def _dot(a, b):
    return jnp.dot(a, b, preferred_element_type=F32)


def _dot_nt(a, b):
    return lax.dot_general(a, b, (((1,), (1,)), ((), ())), preferred_element_type=F32)


def _dot_f32(a, b):
    hi = a.astype(BF16)
    lo = (a - hi.astype(F32)).astype(BF16)
    return _dot(hi, b) + _dot(lo, b)


def _head_masks():
    lane = lax.broadcasted_iota(jnp.int32, (1, W_GROUP), 1)
    return [((lane >= h * HEAD_DIM) & (lane < (h + 1) * HEAD_DIM)).astype(F32) for h in range(H_GROUP)]


def _mod_kernel(c_ref, w_ref, b_ref, o_ref):
    c = c_ref[...]
    s = _silu(c)
    o_ref[0] = jnp.dot(s, w_ref[0], precision=lax.Precision.HIGHEST,
                       preferred_element_type=F32) + b_ref[0]


def _modulation(cond8, w_mod, b_mod):
    nc = 1024
    n_out = w_mod.shape[-1]
    return pl.pallas_call(
        _mod_kernel,
        grid=(DEPTH, n_out // nc),
        in_specs=[
            pl.BlockSpec((8, D_MODEL), lambda l, j: (0, 0)),
            pl.BlockSpec((1, D_MODEL, nc), lambda l, j: (l, 0, j)),
            pl.BlockSpec((1, 1, nc), lambda l, j: (l, 0, j)),
        ],
        out_specs=pl.BlockSpec((1, 8, nc), lambda l, j: (l, 0, j)),
        out_shape=jax.ShapeDtypeStruct((DEPTH, 8, n_out), F32),
        compiler_params=_params("arbitrary", "arbitrary"),
        name="modulation",
    )(cond8, w_mod, b_mod.reshape(DEPTH, 1, n_out))


def _modmm_kernel(x_ref, sh_ref, sc_ref, w_ref, o_ref):
    h = x_ref[...] * (1.0 + sc_ref[0]) + sh_ref[0]
    o_ref[...] = _dot(h.astype(BF16), w_ref[...])


def _modmm(x, sh, sc, w, rows_per_cond):
    n, d = x.shape
    n_out = w.shape[1]
    tb = TOKEN_BLOCK
    bpc = rows_per_cond // tb
    vec = pl.BlockSpec((1, 1, d), lambda b: (b // bpc, 0, 0))
    return pl.pallas_call(
        _modmm_kernel,
        grid=(n // tb,),
        in_specs=[
            pl.BlockSpec((tb, d), lambda b: (b, 0)),
            vec, vec,
            pl.BlockSpec((d, n_out), lambda b: (0, 0)),
        ],
        out_specs=pl.BlockSpec((tb, n_out), lambda b: (b, 0)),
        out_shape=jax.ShapeDtypeStruct((n, n_out), F32),
        compiler_params=_params("arbitrary"),
        name="modulated_projection",
    )(x, sh, sc, w)


def _log_gamma_tile(decay_tile):
    y = -decay_tile
    return -(jnp.maximum(y, 0.0) + jnp.log1p(jnp.exp(-jnp.abs(y))))


def _lane_vector(lg_tile, row0, masks):
    out = lg_tile[row0:row0 + 1, 0:1] * masks[0]
    for h in range(1, H_GROUP):
        out = out + lg_tile[row0 + h:row0 + h + 1, 0:1] * masks[h]
    return out


def _retention_pairs(rq, rk, rv, lg_tile, masks, seq_len, q_block):
    kb = rk.astype(BF16)
    vb = rv.astype(BF16)
    blocks = []
    for q0 in range(0, seq_len, q_block):
        rqb = rq[q0:q0 + q_block]
        ti = lax.broadcasted_iota(jnp.int32, (q_block, seq_len), 0) + q0
        si = lax.broadcasted_iota(jnp.int32, (q_block, seq_len), 1)
        dist = (ti - si).astype(F32)
        fwd = jnp.maximum(dist, 0.0)
        bwd = jnp.maximum(-dist, 0.0)
        acc = jnp.zeros((q_block, W_GROUP), F32)
        for h in range(H_GROUP):
            lgf = lg_tile[h:h + 1, 0:1]
            lgb = lg_tile[H_GROUP + h:H_GROUP + h + 1, 0:1]
            decay = (jnp.where(dist >= 0, jnp.exp(lgf * fwd), 0.0)
                     + jnp.where(dist <= 0, jnp.exp(lgb * bwd), 0.0))
            s = _dot_nt((rqb * masks[h]).astype(BF16), kb)
            acc = acc + _dot((s * decay).astype(BF16), vb) * masks[h]
        blocks.append(acc)
    return blocks[0] if len(blocks) == 1 else jnp.concatenate(blocks, axis=0)


def _retention_finish(o, r_g, avg):
    mu = _dot_f32(o, avg)
    d = o - mu
    var = _dot_f32(d * d, avg)
    return _silu(r_g) * (d * lax.rsqrt(var + LN_EPS))


def _shift_rows(z, off, seq_len):
    if off == 0:
        return z
    rolled = pltpu.roll(z, (-off) % seq_len, 0)
    t = lax.broadcasted_iota(jnp.int32, z.shape, 0)
    valid = (t + off >= 0) & (t + off < seq_len)
    return jnp.where(valid, rolled, 0.0)


def _dwconv(z, w_ref, width, seq_len):
    pad = (width - 1) // 2
    acc = None
    for k in range(width):
        term = _shift_rows(z, k - pad, seq_len) * w_ref[k:k + 1, :]
        acc = term if acc is None else acc + term
    return acc


def _conv_mixers(sc_b, sc_c, sc_x, cf_a, cf_gate, scw_ref, cfw_ref, cfg_ref, cfb_ref, seq_len):
    o_c = sc_b * _dwconv(sc_c * sc_x, scw_ref, SC_WIDTH, seq_len)
    u = _dwconv(cf_a * _sigmoid(cf_gate), cfw_ref, CF_WIDTH, seq_len)
    o_d = _silu(_layernorm_rows(u, cfg_ref[...], cfb_ref[...]))
    return o_c, o_d


def _ctx_mix_request(p_ref, dec_ref, avg_ref, scw_ref, cfw_ref, cfg_ref, cfb_ref,
                     mix_ref, k_ref, v_ref, st_ref, seq_len):
    masks = _head_masks()
    part = lambda j: p_ref[:, j * W_GROUP:(j + 1) * W_GROUP]
    na_q, na_k, na_v = part(0), part(1), part(2)
    k_ref[...] = na_k
    v_ref[...] = na_v

    kb = na_k.astype(BF16)
    vb = na_v.astype(BF16)
    qs = na_q * (HEAD_DIM ** -0.5)
    o_a = jnp.zeros((seq_len, W_GROUP), F32)
    for h in range(H_GROUP):
        s = _dot_nt((qs * masks[h]).astype(BF16), kb)
        m = jnp.max(s, axis=-1, keepdims=True)
        e = jnp.exp(s - m)
        p = e / jnp.sum(e, axis=-1, keepdims=True)
        o_a = o_a + _dot(p.astype(BF16), vb) * masks[h]
    mix_ref[:, 0:W_GROUP] = o_a

    lg = _log_gamma_tile(dec_ref[...])
    rq, rv, r_g = part(3), part(5), part(6)
    rk = part(4) * (HEAD_DIM ** -0.5)
    o = _retention_pairs(rq, rk, rv, lg, masks, seq_len, seq_len)
    mix_ref[:, W_GROUP:2 * W_GROUP] = _retention_finish(o, r_g, avg_ref[...])

    pos = lax.broadcasted_iota(jnp.int32, (seq_len, W_GROUP), 0).astype(F32)
    vb_r = rv.astype(BF16)
    for d in range(2):
        lane_lg = _lane_vector(lg, d * H_GROUP, masks)
        expo = (seq_len - 1.0 - pos) if d == 0 else pos
        kd = rk * jnp.exp(lane_lg * expo)
        full = _dot(kd.T.astype(BF16), vb_r)
        for h in range(H_GROUP):
            st_ref[d, h] = full[h * HEAD_DIM:(h + 1) * HEAD_DIM, h * HEAD_DIM:(h + 1) * HEAD_DIM]

    o_c, o_d = _conv_mixers(part(7), part(8), part(9), part(10), part(11),
                            scw_ref, cfw_ref, cfg_ref, cfb_ref, seq_len)
    mix_ref[:, 2 * W_GROUP:3 * W_GROUP] = o_c
    mix_ref[:, 3 * W_GROUP:4 * W_GROUP] = o_d


def _ctx_layer_kernel(x_ref, sh_ref, sc_ref, g_ref, win_ref, wout_ref, lg_ref, lb_ref,
                      dec_ref, avg_ref, scw_ref, cfw_ref, cfg_ref, cfb_ref,
                      o_ref, k_ref, v_ref, st_ref, proj_ref, mix_ref, *, seq_len, n_sub):
    x = x_ref[...]
    h = x * (1.0 + sc_ref[0]) + sh_ref[0]
    proj_ref[...] = _dot(h.astype(BF16), win_ref[...])
    for r in range(n_sub):
        rows = pl.ds(r * seq_len, seq_len)
        _ctx_mix_request(proj_ref.at[rows], dec_ref, avg_ref, scw_ref, cfw_ref, cfg_ref, cfb_ref,
                         mix_ref.at[rows], k_ref.at[rows], v_ref.at[rows], st_ref.at[r], seq_len)
    mix = _dot(mix_ref[...].astype(BF16), wout_ref[...])
    y = ALPHA * x + g_ref[0] * mix
    o_ref[...] = _layernorm_rows(y, lg_ref[...], lb_ref[...])


def _ctx_layer(x, sh, sc, gate, w_in, w_out, ln_g, ln_b, n_req, seq_len, dec_tile, avg, scw, cfw, cfg, cfb):
    n, d = x.shape
    n_sub = TOKEN_BLOCK // seq_len
    tb = n_sub * seq_len
    n_in = w_in.shape[1]
    const = lambda shape: pl.BlockSpec(shape, lambda b: (0,) * len(shape))
    rows = lambda width: pl.BlockSpec((tb, width), lambda b: (b, 0))
    return pl.pallas_call(
        functools.partial(_ctx_layer_kernel, seq_len=seq_len, n_sub=n_sub),
        grid=(n // tb,),
        in_specs=[
            rows(d), const((1, 1, d)), const((1, 1, d)), const((1, 1, d)),
            const((d, n_in)), const((d, d)), const((1, d)), const((1, d)),
            const((8, LANES)), const((W_GROUP, W_GROUP)), const((8, W_GROUP)),
            const((32, W_GROUP)), const((1, W_GROUP)), const((1, W_GROUP)),
        ],
        out_specs=[
            rows(d), rows(W_GROUP), rows(W_GROUP),
            pl.BlockSpec((n_sub, 2, H_GROUP, HEAD_DIM, HEAD_DIM), lambda b: (b, 0, 0, 0, 0)),
        ],
        out_shape=[
            jax.ShapeDtypeStruct((n, d), F32),
            jax.ShapeDtypeStruct((n, W_GROUP), F32),
            jax.ShapeDtypeStruct((n, W_GROUP), F32),
            jax.ShapeDtypeStruct((n_req, 2, H_GROUP, HEAD_DIM, HEAD_DIM), F32),
        ],
        scratch_shapes=[
            pltpu.VMEM((tb, n_in), F32),
            pltpu.VMEM((tb, d), F32),
        ],
        compiler_params=_params("arbitrary"),
        name="context_layer_mixing",
    )(x, sh, sc, gate, w_in, w_out, ln_g, ln_b, dec_tile, avg, scw, cfw, cfg, cfb)


def _lat_attn_kernel(q_ref, k_ref, v_ref, kc_ref, vc_ref, bias_ref, o_ref, *, rows):
    masks = _head_masks()
    kcb = kc_ref[0].astype(BF16)
    vcb = vc_ref[0].astype(BF16)
    wr = min(NA_WR, rows)

    def row_block(r, carry):
        rs = jnp.clip(r - wr // 2, 0, rows - wr)
        q0 = pl.multiple_of(r * GRID_W, GRID_W)
        k0 = pl.multiple_of(rs * GRID_W, GRID_W)
        qb = q_ref[pl.ds(q0, GRID_W), :] * (HEAD_DIM ** -0.5)
        qs = jnp.concatenate([qb * masks[h] for h in range(H_GROUP)], axis=0).astype(BF16)
        kl = k_ref[pl.ds(k0, wr * GRID_W), :].astype(BF16)
        vl = v_ref[pl.ds(k0, wr * GRID_W), :].astype(BF16)
        d0 = rs - r + (NA_WR - 1)
        off = pl.multiple_of((d0 // 2) * (2 * GRID_W), 2 * GRID_W)
        width = wr * GRID_W
        bias = jnp.where(d0 % 2 == 0, bias_ref[0, :, pl.ds(off, width)], bias_ref[1, :, pl.ds(off, width)])
        s_loc = _dot_nt(qs, kl) + bias
        s_ctx = _dot_nt(qs, kcb)
        m = jnp.maximum(jnp.max(s_loc, axis=-1, keepdims=True), jnp.max(s_ctx, axis=-1, keepdims=True))
        e_loc = jnp.exp(s_loc - m)
        e_ctx = jnp.exp(s_ctx - m)
        z = jnp.sum(e_loc, axis=-1, keepdims=True) + jnp.sum(e_ctx, axis=-1, keepdims=True)
        o = (_dot(e_loc.astype(BF16), vl) + _dot(e_ctx.astype(BF16), vcb)) / z
        acc = o[0:GRID_W] * masks[0]
        for h in range(1, H_GROUP):
            acc = acc + o[h * GRID_W:(h + 1) * GRID_W] * masks[h]
        o_ref[pl.ds(q0, GRID_W), :] = acc
        return carry

    lax.fori_loop(0, rows, row_block, 0)


def _lat_attention(proj, n_req, seq_len, kc, vc, bias):
    rows = seq_len // GRID_W
    part = lambda j: pl.BlockSpec((seq_len, W_GROUP), lambda b, j=j: (b, j))
    past = kc.shape[1]
    return pl.pallas_call(
        functools.partial(_lat_attn_kernel, rows=rows),
        grid=(n_req,),
        in_specs=[
            part(0), part(1), part(2),
            pl.BlockSpec((1, past, W_GROUP), lambda b: (b, 0, 0)),
            pl.BlockSpec((1, past, W_GROUP), lambda b: (b, 0, 0)),
            pl.BlockSpec(bias.shape, lambda b: (0, 0, 0)),
        ],
        out_specs=pl.BlockSpec((seq_len, W_GROUP), lambda b: (b, 0)),
        out_shape=jax.ShapeDtypeStruct((n_req * seq_len, W_GROUP), F32),
        compiler_params=_params("arbitrary"),
        name="latent_attention",
    )(proj, proj, proj, kc, vc, bias)


def _lat_ret_kernel(q_ref, k_ref, v_ref, g_ref, qs_ref, ks_ref, cos_ref, sin_ref, s0_ref,
                    dec_ref, avg_ref, o_ref, *, seq_len):
    masks = _head_masks()
    lg = _log_gamma_tile(dec_ref[...])
    cos = cos_ref[...]
    sin = sin_ref[...]
    rq = q_ref[...] * cos + qs_ref[...] * sin
    rk = (k_ref[...] * cos + ks_ref[...] * sin) * (HEAD_DIM ** -0.5)
    rv = v_ref[...]
    o = _retention_pairs(rq, rk, rv, lg, masks, seq_len, 256)
    pos = lax.broadcasted_iota(jnp.int32, (seq_len, W_GROUP), 0).astype(F32)
    rqb = rq.astype(BF16)
    lgf = _lane_vector(lg, 0, masks)
    lgb = _lane_vector(lg, H_GROUP, masks)
    o = o + _dot(rqb, s0_ref[0, 0].astype(BF16)) * jnp.exp(lgf * (pos + 1.0))
    o = o + _dot(rqb, s0_ref[0, 1].astype(BF16)) * jnp.exp(lgb * (seq_len - pos))
    o_ref[...] = _retention_finish(o, g_ref[...], avg_ref[...])


def _lat_retention(proj, n_req, seq_len, cos, sin, s0_bd, dec_tile, avg):
    part = lambda j: pl.BlockSpec((seq_len, W_GROUP), lambda b, j=j: (b, j))
    const = lambda shape: pl.BlockSpec(shape, lambda b: (0,) * len(shape))
    return pl.pallas_call(
        functools.partial(_lat_ret_kernel, seq_len=seq_len),
        grid=(n_req,),
        in_specs=[
            part(3), part(4), part(5), part(6), part(12), part(13),
            const((seq_len, W_GROUP)), const((seq_len, W_GROUP)),
            pl.BlockSpec((1, 2, W_GROUP, W_GROUP), lambda b: (b, 0, 0, 0)),
            const((8, LANES)), const((W_GROUP, W_GROUP)),
        ],
        out_specs=pl.BlockSpec((seq_len, W_GROUP), lambda b: (b, 0)),
        out_shape=jax.ShapeDtypeStruct((n_req * seq_len, W_GROUP), F32),
        compiler_params=_params("arbitrary"),
        name="latent_retention",
    )(proj, proj, proj, proj, proj, proj, cos, sin, s0_bd, dec_tile, avg)


def _lat_conv_kernel(b_ref, c_ref, x_ref, a_ref, gate_ref, scw_ref, cfw_ref, cfg_ref, cfb_ref,
                     o_ref, *, seq_len):
    o_c, o_d = _conv_mixers(b_ref[...], c_ref[...], x_ref[...], a_ref[...], gate_ref[...],
                            scw_ref, cfw_ref, cfg_ref, cfb_ref, seq_len)
    o_ref[:, 0:W_GROUP] = o_c
    o_ref[:, W_GROUP:2 * W_GROUP] = o_d


def _lat_convs(proj, n_req, seq_len, scw, cfw, cfg, cfb):
    part = lambda j: pl.BlockSpec((seq_len, W_GROUP), lambda b, j=j: (b, j))
    const = lambda shape: pl.BlockSpec(shape, lambda b: (0,) * len(shape))
    return pl.pallas_call(
        functools.partial(_lat_conv_kernel, seq_len=seq_len),
        grid=(n_req,),
        in_specs=[
            part(7), part(8), part(9), part(10), part(11),
            const((8, W_GROUP)), const((32, W_GROUP)), const((1, W_GROUP)), const((1, W_GROUP)),
        ],
        out_specs=pl.BlockSpec((seq_len, 2 * W_GROUP), lambda b: (b, 0)),
        out_shape=jax.ShapeDtypeStruct((n_req * seq_len, 2 * W_GROUP), F32),
        compiler_params=_params("arbitrary"),
        name="latent_convs",
    )(proj, proj, proj, proj, proj, scw, cfw, cfg, cfb)


def _outproj_kernel(*refs, widths):
    n_parts = len(widths)
    part_refs = refs[:n_parts]
    x_ref, g_ref, w_ref, lg_ref, lb_ref, o_ref = refs[n_parts:]
    mix = None
    row = 0
    for p_ref, wd in zip(part_refs, widths):
        term = _dot(p_ref[...].astype(BF16), w_ref[row:row + wd, :])
        mix = term if mix is None else mix + term
        row += wd
    y = ALPHA * x_ref[...] + g_ref[0] * mix
    o_ref[...] = _layernorm_rows(y, lg_ref[...], lb_ref[...])


def _outproj_ln(parts, x, gate, w_out, ln_g, ln_b, rows_per_cond):
    n, d = x.shape
    tb = TOKEN_BLOCK
    bpc = rows_per_cond // tb
    widths = tuple(p.shape[1] for p in parts)
    return pl.pallas_call(
        functools.partial(_outproj_kernel, widths=widths),
        grid=(n // tb,),
        in_specs=[pl.BlockSpec((tb, wd), lambda b: (b, 0)) for wd in widths] + [
            pl.BlockSpec((tb, d), lambda b: (b, 0)),
            pl.BlockSpec((1, 1, d), lambda b: (b // bpc, 0, 0)),
            pl.BlockSpec((d, d), lambda b: (0, 0)),
            pl.BlockSpec((1, d), lambda b: (0, 0)),
            pl.BlockSpec((1, d), lambda b: (0, 0)),
        ],
        out_specs=pl.BlockSpec((tb, d), lambda b: (b, 0)),
        out_shape=jax.ShapeDtypeStruct((n, d), F32),
        compiler_params=_params("arbitrary"),
        name="output_projection_ln",
    )(*parts, x, gate, w_out, ln_g, ln_b)


GATE_DTYPE = jnp.bfloat16


def _pair_words(x):
    bits = pltpu.bitcast(x.astype(F32), jnp.uint32)
    return bits | (bits >> 16)


def _packed_rows(word_row, rows):
    return pltpu.bitcast(jnp.broadcast_to(word_row, (rows // 2, word_row.shape[-1])), GATE_DTYPE)


def _bitonic_merge_desc(vals):
    vals = list(vals)
    n = len(vals)
    j = n // 2
    while j >= 1:
        for i in range(n):
            l = i ^ j
            if l > i:
                vals[i], vals[l] = jnp.maximum(vals[i], vals[l]), jnp.minimum(vals[i], vals[l])
        j //= 2
    return vals


def _bitonic_sort_desc(vals):
    vals = list(vals)
    n = len(vals)
    k = 2
    while k <= n:
        j = k // 2
        while j >= 1:
            for i in range(n):
                l = i ^ j
                if l > i:
                    hi, lo = jnp.maximum(vals[i], vals[l]), jnp.minimum(vals[i], vals[l])
                    vals[i], vals[l] = (hi, lo) if (i & k) == 0 else (lo, hi)
            j //= 2
        k *= 2
    return vals


def _extract_top(s, count, roll=None):
    roll = roll or (lambda v, r: pltpu.roll(v, r, 0))
    sub = 8
    assert s.shape[0] == sub * count
    slabs = _bitonic_sort_desc([s[sub * k:sub * (k + 1)] for k in range(count)])
    r = sub // 2
    while r >= 1:
        other = [roll(v, r) for v in slabs]
        slabs = _bitonic_merge_desc([jnp.maximum(slabs[k], other[count - 1 - k]) for k in range(count)])
        r //= 2
    return [v[0:1] for v in slabs]


def _peer_select_kernel(x_ref, sh_ref, sc_ref, wq_ref, keys_ref, hm_ref, a_ref, b_ref, tau_ref):
    hm = (x_ref[...] * (1.0 + sc_ref[0]) + sh_ref[0]).astype(BF16)
    hm_ref[...] = hm.T
    q = _dot(hm, wq_ref[...]).astype(BF16)
    dk = PEER_NKEYS
    as_gate = lambda v: v.astype(GATE_DTYPE).astype(F32)
    tops = [[None, None] for _ in range(PEER_HEADS)]
    for h in range(PEER_HEADS):
        for p in range(2):
            j = h * 2 + p
            s = _dot_nt(keys_ref[j], q[:, j * dk:(j + 1) * dk])
            top = _extract_top(s, PEER_TOPK)
            e = jnp.where(s >= top[-1], jnp.exp(s - top[0]), 0.0).astype(GATE_DTYPE)
            tops[h][p] = [as_gate(jnp.exp(t - top[0])) for t in top]
            if p == 0:
                a_ref[h] = _pair_words(e)
            else:
                b_ref[h] = e

    stack = lambda rows_per_head: jnp.concatenate(rows_per_head, axis=0)
    a_rank = [stack([tops[h][0][r] for h in range(PEER_HEADS)]) for r in range(PEER_TOPK)]
    b_rank = [stack([tops[h][1][r] for h in range(PEER_HEADS)]) for r in range(PEER_TOPK)]
    likely = [a_rank[r] * b_rank[c] for r in range(PEER_TOPK) for c in range(PEER_TOPK)
              if (r + 1) * (c + 1) <= PEER_TOPK]
    padded = likely + [jnp.full_like(likely[0], -1.0)] * (64 - len(likely))
    kth = _bitonic_sort_desc(padded)[PEER_TOPK - 1]
    a_all = jnp.stack(a_rank)[:, None]
    b_all = jnp.stack(b_rank)[None]
    cand = a_all * b_all
    chosen = cand >= kth
    z = jnp.sum(jnp.where(chosen, cand, 0.0), axis=(0, 1))
    rz = 1.0 / z
    scaled = (a_all.astype(GATE_DTYPE) * (b_all * rz).astype(GATE_DTYPE)).astype(F32)
    tau = jnp.min(jnp.where(chosen, scaled, jnp.inf), axis=(0, 1))
    tau_ref[...] = _pair_words(tau.astype(GATE_DTYPE))
    for h in range(PEER_HEADS):
        b_ref[h] = (b_ref[h].astype(F32) * rz[h:h + 1, :]).astype(GATE_DTYPE)


def _peer_select(x, sh, sc, wq, keys, rows_per_cond):
    n, d = x.shape
    tb = TOKEN_BLOCK
    bpc = rows_per_cond // tb
    vec = pl.BlockSpec((1, 1, d), lambda b: (b // bpc, 0, 0))
    grid_hn = lambda: pl.BlockSpec((PEER_HEADS, PEER_NKEYS, tb), lambda b: (0, 0, b))
    return pl.pallas_call(
        _peer_select_kernel,
        grid=(n // tb,),
        in_specs=[
            pl.BlockSpec((tb, d), lambda b: (b, 0)),
            vec, vec,
            pl.BlockSpec(wq.shape, lambda b: (0, 0)),
            pl.BlockSpec(keys.shape, lambda b: (0, 0, 0)),
        ],
        out_specs=[
            pl.BlockSpec((d, tb), lambda b: (0, b)),
            grid_hn(), grid_hn(),
            pl.BlockSpec((PEER_HEADS, tb), lambda b: (0, b)),
        ],
        out_shape=[
            jax.ShapeDtypeStruct((d, n), BF16),
            jax.ShapeDtypeStruct((PEER_HEADS, PEER_NKEYS, n), jnp.uint32),
            jax.ShapeDtypeStruct((PEER_HEADS, PEER_NKEYS, n), GATE_DTYPE),
            jax.ShapeDtypeStruct((PEER_HEADS, n), jnp.uint32),
        ],
        compiler_params=_params("arbitrary"),
        name="peer_select",
    )(x, sh, sc, wq, keys)


def _gelu(x):
    return 0.5 * x * (1.0 + lax.erf(x * (2.0 ** -0.5)))


def _peer_expert_kernel(hm_ref, a_ref, b_ref, tau_ref, u_ref, v_ref, o_ref, acc_ref, ht_ref, gh_ref,
                        *, n_chunks, tok_block):
    c = pl.program_id(1)

    @pl.when(c == 0)
    def _():
        acc_ref[...] = jnp.zeros_like(acc_ref)

    rows_per_chunk = PEER_EXPERT_CHUNK // PEER_NKEYS
    group = PEER_TOKEN_GROUP
    n_groups = tok_block // group
    v_t = v_ref[0]

    halves = PEER_EXPERT_SPLIT
    half_rows = rows_per_chunk // halves
    half = PEER_EXPERT_CHUNK // halves

    def pre_activations(g, eh):
        tok = pl.ds(g * group, group)
        ex = pl.ds(eh * half, half)
        ht_ref[ex, tok] = _dot(u_ref[0, ex, :], hm_ref[:, tok])

    def gated_activations(g, eh):
        for sub in range(group // LANES):
            tok = pl.ds(g * group + sub * LANES, LANES)
            for i in range(eh * half_rows, (eh + 1) * half_rows):
                rows = pl.ds(i * PEER_NKEYS, PEER_NKEYS)
                gate = jnp.zeros((PEER_NKEYS, LANES), GATE_DTYPE)
                for h in range(PEER_HEADS):
                    e = _packed_rows(a_ref[h, i:i + 1, tok], PEER_NKEYS) * b_ref[h, :, tok]
                    tau = _packed_rows(tau_ref[h:h + 1, tok], PEER_NKEYS)
                    gate = jnp.where(e >= tau, gate + e, gate)
                act = _gelu(ht_ref[rows, tok]).astype(GATE_DTYPE)
                gh_ref[rows, tok] = (gate * act).astype(BF16)

    def accumulate(g):
        tok = pl.ds(g * group, group)
        acc_ref[:, tok] += _dot(v_t, gh_ref[:, tok])

    units = [(g, eh) for g in range(n_groups) for eh in range(halves)]
    for k in range(len(units) + 1):
        if k < len(units):
            pre_activations(*units[k])
        if k >= 1:
            g, eh = units[k - 1]
            gated_activations(g, eh)
            if eh == halves - 1:
                accumulate(g)

    @pl.when(c == n_chunks - 1)
    def _():
        o_ref[...] = acc_ref[...].T


def _peer_experts(hm, a_t, b_t, tau, u_all, v_all, layer):
    d, n = hm.shape
    tb = PEER_TOKEN_BLOCK
    ec = PEER_EXPERT_CHUNK
    n_chunks = N_EXPERTS // ec
    rpc = ec // PEER_NKEYS
    return pl.pallas_call(
        functools.partial(_peer_expert_kernel, n_chunks=n_chunks, tok_block=tb),
        grid=(n // tb, n_chunks),
        in_specs=[
            pl.BlockSpec((d, tb), lambda b, c: (0, b)),
            pl.BlockSpec((PEER_HEADS, rpc, tb), lambda b, c: (0, c, b)),
            pl.BlockSpec((PEER_HEADS, PEER_NKEYS, tb), lambda b, c: (0, 0, b)),
            pl.BlockSpec((PEER_HEADS, tb), lambda b, c: (0, b)),
            pl.BlockSpec((1, ec, d), lambda b, c: (layer, c, 0)),
            pl.BlockSpec((1, d, ec), lambda b, c: (layer, 0, c)),
        ],
        out_specs=pl.BlockSpec((tb, d), lambda b, c: (b, 0)),
        out_shape=jax.ShapeDtypeStruct((n, d), F32),
        scratch_shapes=[
            pltpu.VMEM((d, tb), F32),
            pltpu.VMEM((ec, tb), F32),
            pltpu.VMEM((ec, tb), BF16),
        ],
        compiler_params=_params("arbitrary", "arbitrary"),
        name="peer_experts",
    )(hm, a_t, b_t, tau, u_all, v_all)


def _residual_ln_kernel(x_ref, f_ref, g_ref, lg_ref, lb_ref, o_ref):
    y = ALPHA * x_ref[...] + g_ref[0] * f_ref[...]
    o_ref[...] = _layernorm_rows(y, lg_ref[...], lb_ref[...])


def _residual_ln(x, f, gate, ln_g, ln_b, rows_per_cond):
    n, d = x.shape
    tb = TOKEN_BLOCK
    bpc = rows_per_cond // tb
    row = pl.BlockSpec((tb, d), lambda b: (b, 0))
    return pl.pallas_call(
        _residual_ln_kernel,
        grid=(n // tb,),
        in_specs=[row, row,
                  pl.BlockSpec((1, 1, d), lambda b: (b // bpc, 0, 0)),
                  pl.BlockSpec((1, d), lambda b: (0, 0)),
                  pl.BlockSpec((1, d), lambda b: (0, 0))],
        out_specs=row,
        out_shape=jax.ShapeDtypeStruct((n, d), F32),
        compiler_params=_params("arbitrary"),
        name="residual_ln",
    )(x, f, gate, ln_g, ln_b)


def _rope_tables(seq_len):
    t = np.arange(seq_len)
    row = (t // GRID_W).astype(np.float32)
    col = (t % GRID_W).astype(np.float32)
    n_freq = HEAD_DIM // 4
    inv = (ROPE_BASE ** (-np.arange(n_freq, dtype=np.float32) / n_freq)).astype(np.float32)
    ang = jnp.asarray(np.concatenate([row[:, None] * inv, col[:, None] * inv], -1).astype(np.float32))
    cos, sin = jnp.cos(ang), jnp.sin(ang)
    cos_t = jnp.tile(jnp.concatenate([cos, cos], -1), (1, H_GROUP))
    sin_t = jnp.tile(jnp.concatenate([-sin, sin], -1), (1, H_GROUP))
    return cos_t, sin_t


def _swap_halves_perm():
    idx = np.arange(W_GROUP)
    half = HEAD_DIM // 2
    return np.where((idx % HEAD_DIM) < half, idx + half, idx - half)


def _attention_bias(rpb, rows):
    cols = np.arange(GRID_W)
    c_start = np.clip(cols - NA_WC // 2, 0, GRID_W - NA_WC)
    col_in = (cols[None, :] >= c_start[:, None]) & (cols[None, :] < c_start[:, None] + NA_WC)
    col_idx = np.clip(cols[None, :] - cols[:, None], 1 - NA_WC, NA_WC - 1) + NA_WC - 1
    pick = jnp.asarray((col_idx[None] == np.arange(2 * NA_WC - 1)[:, None, None]).astype(np.float32))
    rpb_cols = jnp.einsum('hdj,jqk->hdqk', rpb, pick, precision=lax.Precision.HIGHEST)
    masked = jnp.where(jnp.asarray(col_in)[None, None], rpb_cols, NEG_BIG)
    n_rel = 2 * NA_WR - 1
    flat = masked.transpose(0, 2, 1, 3).reshape(H_GROUP * GRID_W, n_rel * GRID_W)
    wide = jnp.pad(flat, ((0, 0), (0, (n_rel + 3) * GRID_W - flat.shape[1])))
    keep = (n_rel + 1) * GRID_W
    return jnp.stack([wide[:, :keep], wide[:, GRID_W:GRID_W + keep]], axis=0)


def _block_diag_states(s0):
    b = s0.shape[0]
    eye = jnp.eye(H_GROUP, dtype=s0.dtype)
    full = jnp.einsum('bchde,hg->bchdge', s0, eye)
    return full.reshape(b, 2, H_GROUP * HEAD_DIM, H_GROUP * HEAD_DIM)


def kernel(x_prompt, x_sample, c, cache_na_k, cache_na_v, state_ret, c_ctx, w_mod, b_mod, w_in, w_out,
           na_rpb, ret_decay, sc_w, cf_w, cf_ln_g, cf_ln_b, ln1_g, ln1_b, ln2_g, ln2_b,
           peer_wq, peer_keys, peer_u, peer_v):
    n_req, seq, d = x_prompt.shape
    n_lat, lat_seq, _ = x_sample.shape
    past = cache_na_k.shape[2]
    rows = lat_seq // GRID_W

    cond8 = jnp.concatenate([c_ctx[None], c, jnp.zeros((8 - 1 - n_lat, d), F32)], axis=0)
    mod = _modulation(cond8, w_mod, b_mod)

    def mod_vec(l, j, lo, hi):
        return mod[l, lo:hi, j * d:(j + 1) * d][:, None, :]

    perm = _swap_halves_perm()
    w_in_b = w_in.astype(BF16)
    w_in_lat = jnp.concatenate([w_in_b, w_in_b[:, :, 3 * W_GROUP + perm], w_in_b[:, :, 4 * W_GROUP + perm]], -1)
    w_out_b = w_out.astype(BF16)
    wq_b = peer_wq.astype(BF16)
    keys_b = peer_keys.astype(BF16).reshape(DEPTH, PEER_HEADS * 2, PEER_NKEYS, PEER_NKEYS)
    u_b = peer_u.astype(BF16)
    v_b = jnp.swapaxes(peer_v, 1, 2).astype(BF16)

    lane_h = np.arange(W_GROUP) // HEAD_DIM
    avg = jnp.asarray((lane_h[:, None] == lane_h[None, :]).astype(np.float32) / HEAD_DIM).astype(BF16)
    cos_t, sin_t = _rope_tables(lat_seq)
    dec_tiles = jnp.broadcast_to(ret_decay.reshape(DEPTH, 2 * H_GROUP, 1), (DEPTH, 8, LANES))
    scw = jnp.pad(sc_w, ((0, 0), (0, 8 - SC_WIDTH), (0, 0)))
    cfw = jnp.pad(cf_w, ((0, 0), (0, 32 - CF_WIDTH), (0, 0)))
    kc_all = cache_na_k.reshape(n_lat, DEPTH, past, W_GROUP)
    vc_all = cache_na_v.reshape(n_lat, DEPTH, past, W_GROUP)

    xp = x_prompt.reshape(n_req * seq, d)
    xs = x_sample.reshape(n_lat * lat_seq, d)
    ks_out, vs_out, ss_out = [], [], []

    for l in range(DEPTH):
        row1 = lambda a: a[l][None, :]
        cfg, cfb = row1(cf_ln_g), row1(cf_ln_b)
        l1g, l1b, l2g, l2b = row1(ln1_g), row1(ln1_b), row1(ln2_g), row1(ln2_b)

        n_ctx = n_req * seq
        xp, k_ctx, v_ctx, s_ctx = _ctx_layer(
            xp, mod_vec(l, 0, 0, 1), mod_vec(l, 1, 0, 1), mod_vec(l, 2, 0, 1), w_in_b[l], w_out_b[l], l1g, l1b,
            n_req, seq, dec_tiles[l], avg, scw[l], cfw[l], cfg, cfb)
        hm, a_t, b_t, tau = _peer_select(xp, mod_vec(l, 3, 0, 1), mod_vec(l, 4, 0, 1), wq_b[l], keys_b[l], n_ctx)
        ffn = _peer_experts(hm, a_t, b_t, tau, u_b, v_b, l)
        xp = _residual_ln(xp, ffn, mod_vec(l, 5, 0, 1), l2g, l2b, n_ctx)
        ks_out.append(k_ctx.reshape(n_req, seq, H_GROUP, HEAD_DIM))
        vs_out.append(v_ctx.reshape(n_req, seq, H_GROUP, HEAD_DIM))
        ss_out.append(s_ctx)

        hi = 1 + n_lat
        proj = _modmm(xs, mod_vec(l, 0, 1, hi), mod_vec(l, 1, 1, hi), w_in_lat[l], lat_seq)
        bias = _attention_bias(na_rpb[l], rows)
        o_a = _lat_attention(proj, n_lat, lat_seq, kc_all[:, l], vc_all[:, l], bias)
        o_b = _lat_retention(proj, n_lat, lat_seq, cos_t, sin_t, _block_diag_states(state_ret[:, l]),
                             dec_tiles[l], avg)
        o_cd = _lat_convs(proj, n_lat, lat_seq, scw[l], cfw[l], cfg, cfb)
        xs = _outproj_ln([o_a, o_b, o_cd], xs, mod_vec(l, 2, 1, hi), w_out_b[l], l1g, l1b, lat_seq)
        hm, a_t, b_t, tau = _peer_select(xs, mod_vec(l, 3, 1, hi), mod_vec(l, 4, 1, hi), wq_b[l], keys_b[l], lat_seq)
        ffn = _peer_experts(hm, a_t, b_t, tau, u_b, v_b, l)
        xs = _residual_ln(xs, ffn, mod_vec(l, 5, 1, hi), l2g, l2b, lat_seq)

    return (xp.reshape(n_req, seq, d), xs.reshape(n_lat, lat_seq, d),
            jnp.stack(ks_out, axis=1), jnp.stack(vs_out, axis=1), jnp.stack(ss_out, axis=1))
```

```python
import functools
import math

import jax
import jax.numpy as jnp
import numpy as np
from jax import lax
from jax.experimental import pallas as pl
from jax.experimental.pallas import tpu as pltpu

F32 = jnp.float32
BF16 = jnp.bfloat16

D_MODEL = 1024
DEPTH = 4
GRID_W = 64
HEAD_DIM = 64
W_GROUP = 256
H_GROUP = 4
N_IN_PARTS = 12
NA_WR = 8
NA_WC = 16
ROPE_BASE = 10000.0
SC_WIDTH = 3
CF_WIDTH = 31
PEER_HEADS = 8
PEER_NKEYS = 128
PEER_TOPK = 16
N_EXPERTS = PEER_NKEYS * PEER_NKEYS
LN_EPS = 1e-5
ALPHA = (2.0 * DEPTH) ** 0.25
NEG_BIG = -1e30

VMEM_LIMIT_BYTES = 56 * 1024 * 1024
LANES = 128

TOKEN_BLOCK = 512
PEER_TOKEN_BLOCK = 1024
PEER_TOKEN_GROUP = 256
PEER_EXPERT_CHUNK = 2048
PEER_EXPERT_SPLIT = 1


def _params(*sem):
    return pltpu.CompilerParams(dimension_semantics=sem, vmem_limit_bytes=VMEM_LIMIT_BYTES)


def _sigmoid(x):
    return 1.0 / (1.0 + jnp.exp(-x))


def _silu(x):
    return x * _sigmoid(x)


def _layernorm_rows(x, g, b):
    mu = jnp.mean(x, axis=-1, keepdims=True)
    d = x - mu
    var = jnp.mean(d * d, axis=-1, keepdims=True)
    return d * lax.rsqrt(var + LN_EPS) * g + b


def _dot(a, b):
    return jnp.dot(a, b, preferred_element_type=F32)


def _dot_nt(a, b):
    return lax.dot_general(a, b, (((1,), (1,)), ((), ())), preferred_element_type=F32)


def _dot_f32(a, b):
    hi = a.astype(BF16)
    lo = (a - hi.astype(F32)).astype(BF16)
    return _dot(hi, b) + _dot(lo, b)


def _head_masks():
    lane = lax.broadcasted_iota(jnp.int32, (1, W_GROUP), 1)
    return [((lane >= h * HEAD_DIM) & (lane < (h + 1) * HEAD_DIM)).astype(F32) for h in range(H_GROUP)]


def _mod_kernel(c_ref, w_ref, b_ref, o_ref):
    c = c_ref[...]
    s = _silu(c)
    o_ref[0] = jnp.dot(s, w_ref[0], precision=lax.Precision.HIGHEST,
                       preferred_element_type=F32) + b_ref[0]


def _modulation(cond8, w_mod, b_mod):
    nc = 1024
    n_out = w_mod.shape[-1]
    return pl.pallas_call(
        _mod_kernel,
        grid=(DEPTH, n_out // nc),
        in_specs=[
            pl.BlockSpec((8, D_MODEL), lambda l, j: (0, 0)),
            pl.BlockSpec((1, D_MODEL, nc), lambda l, j: (l, 0, j)),
            pl.BlockSpec((1, 1, nc), lambda l, j: (l, 0, j)),
        ],
        out_specs=pl.BlockSpec((1, 8, nc), lambda l, j: (l, 0, j)),
        out_shape=jax.ShapeDtypeStruct((DEPTH, 8, n_out), F32),
        compiler_params=_params("arbitrary", "arbitrary"),
        name="modulation",
    )(cond8, w_mod, b_mod.reshape(DEPTH, 1, n_out))


def _modmm_kernel(x_ref, sh_ref, sc_ref, w_ref, o_ref):
    h = x_ref[...] * (1.0 + sc_ref[0]) + sh_ref[0]
    o_ref[...] = _dot(h.astype(BF16), w_ref[...])


def _modmm(x, sh, sc, w, rows_per_cond):
    n, d = x.shape
    n_out = w.shape[1]
    tb = TOKEN_BLOCK
    bpc = rows_per_cond // tb
    vec = pl.BlockSpec((1, 1, d), lambda b: (b // bpc, 0, 0))
    return pl.pallas_call(
        _modmm_kernel,
        grid=(n // tb,),
        in_specs=[
            pl.BlockSpec((tb, d), lambda b: (b, 0)),
            vec, vec,
            pl.BlockSpec((d, n_out), lambda b: (0, 0)),
        ],
        out_specs=pl.BlockSpec((tb, n_out), lambda b: (b, 0)),
        out_shape=jax.ShapeDtypeStruct((n, n_out), F32),
        compiler_params=_params("arbitrary"),
        name="modulated_projection",
    )(x, sh, sc, w)


def _log_gamma_tile(decay_tile):
    y = -decay_tile
    return -(jnp.maximum(y, 0.0) + jnp.log1p(jnp.exp(-jnp.abs(y))))


def _lane_vector(lg_tile, row0, masks):
    out = lg_tile[row0:row0 + 1, 0:1] * masks[0]
    for h in range(1, H_GROUP):
        out = out + lg_tile[row0 + h:row0 + h + 1, 0:1] * masks[h]
    return out


def _retention_pairs(rq, rk, rv, lg_tile, masks, seq_len, q_block):
    kb = rk.astype(BF16)
    vb = rv.astype(BF16)
    blocks = []
    for q0 in range(0, seq_len, q_block):
        rqb = rq[q0:q0 + q_block]
        ti = lax.broadcasted_iota(jnp.int32, (q_block, seq_len), 0) + q0
        si = lax.broadcasted_iota(jnp.int32, (q_block, seq_len), 1)
        dist = (ti - si).astype(F32)
        fwd = jnp.maximum(dist, 0.0)
        bwd = jnp.maximum(-dist, 0.0)
        acc = jnp.zeros((q_block, W_GROUP), F32)
        for h in range(H_GROUP):
            lgf = lg_tile[h:h + 1, 0:1]
            lgb = lg_tile[H_GROUP + h:H_GROUP + h + 1, 0:1]
            decay = (jnp.where(dist >= 0, jnp.exp(lgf * fwd), 0.0)
                     + jnp.where(dist <= 0, jnp.exp(lgb * bwd), 0.0))
            s = _dot_nt((rqb * masks[h]).astype(BF16), kb)
            acc = acc + _dot((s * decay).astype(BF16), vb) * masks[h]
        blocks.append(acc)
    return blocks[0] if len(blocks) == 1 else jnp.concatenate(blocks, axis=0)


def _retention_finish(o, r_g, avg):
    mu = _dot_f32(o, avg)
    d = o - mu
    var = _dot_f32(d * d, avg)
    return _silu(r_g) * (d * lax.rsqrt(var + LN_EPS))


def _shift_rows(z, off, seq_len):
    if off == 0:
        return z
    rolled = pltpu.roll(z, (-off) % seq_len, 0)
    t = lax.broadcasted_iota(jnp.int32, z.shape, 0)
    valid = (t + off >= 0) & (t + off < seq_len)
    return jnp.where(valid, rolled, 0.0)


def _dwconv(z, w_ref, width, seq_len):
    pad = (width - 1) // 2
    acc = None
    for k in range(width):
        term = _shift_rows(z, k - pad, seq_len) * w_ref[k:k + 1, :]
        acc = term if acc is None else acc + term
    return acc


def _conv_mixers(sc_b, sc_c, sc_x, cf_a, cf_gate, scw_ref, cfw_ref, cfg_ref, cfb_ref, seq_len):
    o_c = sc_b * _dwconv(sc_c * sc_x, scw_ref, SC_WIDTH, seq_len)
    u = _dwconv(cf_a * _sigmoid(cf_gate), cfw_ref, CF_WIDTH, seq_len)
    o_d = _silu(_layernorm_rows(u, cfg_ref[...], cfb_ref[...]))
    return o_c, o_d


def _ctx_mix_request(p_ref, dec_ref, avg_ref, scw_ref, cfw_ref, cfg_ref, cfb_ref,
                     mix_ref, k_ref, v_ref, st_ref, seq_len):
    masks = _head_masks()
    part = lambda j: p_ref[:, j * W_GROUP:(j + 1) * W_GROUP]
    na_q, na_k, na_v = part(0), part(1), part(2)
    k_ref[...] = na_k
    v_ref[...] = na_v

    kb = na_k.astype(BF16)
    vb = na_v.astype(BF16)
    qs = na_q * (HEAD_DIM ** -0.5)
    o_a = jnp.zeros((seq_len, W_GROUP), F32)
    for h in range(H_GROUP):
        s = _dot_nt((qs * masks[h]).astype(BF16), kb)
        m = jnp.max(s, axis=-1, keepdims=True)
        e = jnp.exp(s - m)
        p = e / jnp.sum(e, axis=-1, keepdims=True)
        o_a = o_a + _dot(p.astype(BF16), vb) * masks[h]
    mix_ref[:, 0:W_GROUP] = o_a

    lg = _log_gamma_tile(dec_ref[...])
    rq, rv, r_g = part(3), part(5), part(6)
    rk = part(4) * (HEAD_DIM ** -0.5)
    o = _retention_pairs(rq, rk, rv, lg, masks, seq_len, seq_len)
    mix_ref[:, W_GROUP:2 * W_GROUP] = _retention_finish(o, r_g, avg_ref[...])

    pos = lax.broadcasted_iota(jnp.int32, (seq_len, W_GROUP), 0).astype(F32)
    vb_r = rv.astype(BF16)
    for d in range(2):
        lane_lg = _lane_vector(lg, d * H_GROUP, masks)
        expo = (seq_len - 1.0 - pos) if d == 0 else pos
        kd = rk * jnp.exp(lane_lg * expo)
        full = _dot(kd.T.astype(BF16), vb_r)
        for h in range(H_GROUP):
            st_ref[d, h] = full[h * HEAD_DIM:(h + 1) * HEAD_DIM, h * HEAD_DIM:(h + 1) * HEAD_DIM]

    o_c, o_d = _conv_mixers(part(7), part(8), part(9), part(10), part(11),
                            scw_ref, cfw_ref, cfg_ref, cfb_ref, seq_len)
    mix_ref[:, 2 * W_GROUP:3 * W_GROUP] = o_c
    mix_ref[:, 3 * W_GROUP:4 * W_GROUP] = o_d


def _ctx_layer_kernel(*refs, seq_len, n_sub, has_prev):
    if has_prev:
        f_ref, pg_ref, plg_ref, plb_ref = refs[:4]
        refs = refs[4:]
    (x_ref, sh_ref, sc_ref, g_ref, win_ref, wout_ref, lg_ref, lb_ref,
     dec_ref, avg_ref, scw_ref, cfw_ref, cfg_ref, cfb_ref,
     o_ref, k_ref, v_ref, st_ref, proj_ref, mix_ref) = refs
    x = x_ref[...]
    if has_prev:
        x = _layernorm_rows(ALPHA * x + pg_ref[0] * f_ref[...].T, plg_ref[...], plb_ref[...])
    h = x * (1.0 + sc_ref[0]) + sh_ref[0]
    proj_ref[...] = _dot(h.astype(BF16), win_ref[...])
    for r in range(n_sub):
        rows = pl.ds(r * seq_len, seq_len)
        _ctx_mix_request(proj_ref.at[rows], dec_ref, avg_ref, scw_ref, cfw_ref, cfg_ref, cfb_ref,
                         mix_ref.at[rows], k_ref.at[rows], v_ref.at[rows], st_ref.at[r], seq_len)
    mix = _dot(mix_ref[...].astype(BF16), wout_ref[...])
    y = ALPHA * x + g_ref[0] * mix
    o_ref[...] = _layernorm_rows(y, lg_ref[...], lb_ref[...])


def _ctx_layer(x, sh, sc, gate, w_in, w_out, ln_g, ln_b, n_req, seq_len, dec_tile, avg, scw, cfw, cfg, cfb,
               prev=None):
    n, d = x.shape
    n_sub = TOKEN_BLOCK // seq_len
    tb = n_sub * seq_len
    n_in = w_in.shape[1]
    const = lambda shape: pl.BlockSpec(shape, lambda b: (0,) * len(shape))
    rows = lambda width: pl.BlockSpec((tb, width), lambda b: (b, 0))
    prev_specs = [] if prev is None else [pl.BlockSpec((d, tb), lambda b: (0, b)),
                                          const((1, 1, d)), const((1, d)), const((1, d))]
    prev_args = () if prev is None else tuple(prev)
    return pl.pallas_call(
        functools.partial(_ctx_layer_kernel, seq_len=seq_len, n_sub=n_sub, has_prev=prev is not None),
        grid=(n // tb,),
        in_specs=prev_specs + [
            rows(d), const((1, 1, d)), const((1, 1, d)), const((1, 1, d)),
            const((d, n_in)), const((d, d)), const((1, d)), const((1, d)),
            const((8, LANES)), const((W_GROUP, W_GROUP)), const((8, W_GROUP)),
            const((32, W_GROUP)), const((1, W_GROUP)), const((1, W_GROUP)),
        ],
        out_specs=[
            rows(d), rows(W_GROUP), rows(W_GROUP),
            pl.BlockSpec((n_sub, 2, H_GROUP, HEAD_DIM, HEAD_DIM), lambda b: (b, 0, 0, 0, 0)),
        ],
        out_shape=[
            jax.ShapeDtypeStruct((n, d), F32),
            jax.ShapeDtypeStruct((n, W_GROUP), F32),
            jax.ShapeDtypeStruct((n, W_GROUP), F32),
            jax.ShapeDtypeStruct((n_req, 2, H_GROUP, HEAD_DIM, HEAD_DIM), F32),
        ],
        scratch_shapes=[
            pltpu.VMEM((tb, n_in), F32),
            pltpu.VMEM((tb, d), F32),
        ],
        compiler_params=_params("arbitrary"),
        name="context_layer_mixing",
    )(*prev_args, x, sh, sc, gate, w_in, w_out, ln_g, ln_b, dec_tile, avg, scw, cfw, cfg, cfb)


def _lat_attn_kernel(q_ref, k_ref, v_ref, kc_ref, vc_ref, bias_ref, o_ref, *, rows):
    masks = _head_masks()
    kcb = kc_ref[0].astype(BF16)
    vcb = vc_ref[0].astype(BF16)
    wr = min(NA_WR, rows)

    def row_block(r, carry):
        rs = jnp.clip(r - wr // 2, 0, rows - wr)
        q0 = pl.multiple_of(r * GRID_W, GRID_W)
        k0 = pl.multiple_of(rs * GRID_W, GRID_W)
        qb = q_ref[pl.ds(q0, GRID_W), :] * (HEAD_DIM ** -0.5)
        qs = jnp.concatenate([qb * masks[h] for h in range(H_GROUP)], axis=0).astype(BF16)
        kl = k_ref[pl.ds(k0, wr * GRID_W), :].astype(BF16)
        vl = v_ref[pl.ds(k0, wr * GRID_W), :].astype(BF16)
        d0 = rs - r + (NA_WR - 1)
        off = pl.multiple_of((d0 // 2) * (2 * GRID_W), 2 * GRID_W)
        width = wr * GRID_W
        bias = jnp.where(d0 % 2 == 0, bias_ref[0, :, pl.ds(off, width)], bias_ref[1, :, pl.ds(off, width)])
        s_loc = _dot_nt(qs, kl) + bias
        s_ctx = _dot_nt(qs, kcb)
        m = jnp.maximum(jnp.max(s_loc, axis=-1, keepdims=True), jnp.max(s_ctx, axis=-1, keepdims=True))
        e_loc = jnp.exp(s_loc - m)
        e_ctx = jnp.exp(s_ctx - m)
        z = jnp.sum(e_loc, axis=-1, keepdims=True) + jnp.sum(e_ctx, axis=-1, keepdims=True)
        o = (_dot(e_loc.astype(BF16), vl) + _dot(e_ctx.astype(BF16), vcb)) / z
        acc = o[0:GRID_W] * masks[0]
        for h in range(1, H_GROUP):
            acc = acc + o[h * GRID_W:(h + 1) * GRID_W] * masks[h]
        o_ref[pl.ds(q0, GRID_W), :] = acc
        return carry

    lax.fori_loop(0, rows, row_block, 0)


def _lat_attention(proj, n_req, seq_len, kc, vc, bias):
    rows = seq_len // GRID_W
    part = lambda j: pl.BlockSpec((seq_len, W_GROUP), lambda b, j=j: (b, j))
    past = kc.shape[1]
    return pl.pallas_call(
        functools.partial(_lat_attn_kernel, rows=rows),
        grid=(n_req,),
        in_specs=[
            part(0), part(1), part(2),
            pl.BlockSpec((1, past, W_GROUP), lambda b: (b, 0, 0)),
            pl.BlockSpec((1, past, W_GROUP), lambda b: (b, 0, 0)),
            pl.BlockSpec(bias.shape, lambda b: (0, 0, 0)),
        ],
        out_specs=pl.BlockSpec((seq_len, W_GROUP), lambda b: (b, 0)),
        out_shape=jax.ShapeDtypeStruct((n_req * seq_len, W_GROUP), F32),
        compiler_params=_params("arbitrary"),
        name="latent_attention",
    )(proj, proj, proj, kc, vc, bias)


def _lat_ret_kernel(q_ref, k_ref, v_ref, g_ref, qs_ref, ks_ref, cos_ref, sin_ref, s0_ref,
                    dec_ref, avg_ref, o_ref, *, seq_len):
    masks = _head_masks()
    lg = _log_gamma_tile(dec_ref[...])
    cos = cos_ref[...]
    sin = sin_ref[...]
    rq = q_ref[...] * cos + qs_ref[...] * sin
    rk = (k_ref[...] * cos + ks_ref[...] * sin) * (HEAD_DIM ** -0.5)
    rv = v_ref[...]
    o = _retention_pairs(rq, rk, rv, lg, masks, seq_len, 256)
    pos = lax.broadcasted_iota(jnp.int32, (seq_len, W_GROUP), 0).astype(F32)
    rqb = rq.astype(BF16)
    lgf = _lane_vector(lg, 0, masks)
    lgb = _lane_vector(lg, H_GROUP, masks)
    o = o + _dot(rqb, s0_ref[0, 0].astype(BF16)) * jnp.exp(lgf * (pos + 1.0))
    o = o + _dot(rqb, s0_ref[0, 1].astype(BF16)) * jnp.exp(lgb * (seq_len - pos))
    o_ref[...] = _retention_finish(o, g_ref[...], avg_ref[...])


def _lat_retention(proj, n_req, seq_len, cos, sin, s0_bd, dec_tile, avg):
    part = lambda j: pl.BlockSpec((seq_len, W_GROUP), lambda b, j=j: (b, j))
    const = lambda shape: pl.BlockSpec(shape, lambda b: (0,) * len(shape))
    return pl.pallas_call(
        functools.partial(_lat_ret_kernel, seq_len=seq_len),
        grid=(n_req,),
        in_specs=[
            part(3), part(4), part(5), part(6), part(12), part(13),
            const((seq_len, W_GROUP)), const((seq_len, W_GROUP)),
            pl.BlockSpec((1, 2, W_GROUP, W_GROUP), lambda b: (b, 0, 0, 0)),
            const((8, LANES)), const((W_GROUP, W_GROUP)),
        ],
        out_specs=pl.BlockSpec((seq_len, W_GROUP), lambda b: (b, 0)),
        out_shape=jax.ShapeDtypeStruct((n_req * seq_len, W_GROUP), F32),
        compiler_params=_params("arbitrary"),
        name="latent_retention",
    )(proj, proj, proj, proj, proj, proj, cos, sin, s0_bd, dec_tile, avg)


def _lat_conv_kernel(b_ref, c_ref, x_ref, a_ref, gate_ref, scw_ref, cfw_ref, cfg_ref, cfb_ref,
                     o_ref, *, seq_len):
    o_c, o_d = _conv_mixers(b_ref[...], c_ref[...], x_ref[...], a_ref[...], gate_ref[...],
                            scw_ref, cfw_ref, cfg_ref, cfb_ref, seq_len)
    o_ref[:, 0:W_GROUP] = o_c
    o_ref[:, W_GROUP:2 * W_GROUP] = o_d


def _lat_convs(proj, n_req, seq_len, scw, cfw, cfg, cfb):
    part = lambda j: pl.BlockSpec((seq_len, W_GROUP), lambda b, j=j: (b, j))
    const = lambda shape: pl.BlockSpec(shape, lambda b: (0,) * len(shape))
    return pl.pallas_call(
        functools.partial(_lat_conv_kernel, seq_len=seq_len),
        grid=(n_req,),
        in_specs=[
            part(7), part(8), part(9), part(10), part(11),
            const((8, W_GROUP)), const((32, W_GROUP)), const((1, W_GROUP)), const((1, W_GROUP)),
        ],
        out_specs=pl.BlockSpec((seq_len, 2 * W_GROUP), lambda b: (b, 0)),
        out_shape=jax.ShapeDtypeStruct((n_req * seq_len, 2 * W_GROUP), F32),
        compiler_params=_params("arbitrary"),
        name="latent_convs",
    )(proj, proj, proj, proj, proj, scw, cfw, cfg, cfb)


def _outproj_kernel(*refs, widths):
    n_parts = len(widths)
    part_refs = refs[:n_parts]
    x_ref, g_ref, w_ref, lg_ref, lb_ref, o_ref = refs[n_parts:]
    mix = None
    row = 0
    for p_ref, wd in zip(part_refs, widths):
        term = _dot(p_ref[...].astype(BF16), w_ref[row:row + wd, :])
        mix = term if mix is None else mix + term
        row += wd
    y = ALPHA * x_ref[...] + g_ref[0] * mix
    o_ref[...] = _layernorm_rows(y, lg_ref[...], lb_ref[...])


def _outproj_ln(parts, x, gate, w_out, ln_g, ln_b, rows_per_cond):
    n, d = x.shape
    tb = TOKEN_BLOCK
    bpc = rows_per_cond // tb
    widths = tuple(p.shape[1] for p in parts)
    return pl.pallas_call(
        functools.partial(_outproj_kernel, widths=widths),
        grid=(n // tb,),
        in_specs=[pl.BlockSpec((tb, wd), lambda b: (b, 0)) for wd in widths] + [
            pl.BlockSpec((tb, d), lambda b: (b, 0)),
            pl.BlockSpec((1, 1, d), lambda b: (b // bpc, 0, 0)),
            pl.BlockSpec((d, d), lambda b: (0, 0)),
            pl.BlockSpec((1, d), lambda b: (0, 0)),
            pl.BlockSpec((1, d), lambda b: (0, 0)),
        ],
        out_specs=pl.BlockSpec((tb, d), lambda b: (b, 0)),
        out_shape=jax.ShapeDtypeStruct((n, d), F32),
        compiler_params=_params("arbitrary"),
        name="output_projection_ln",
    )(*parts, x, gate, w_out, ln_g, ln_b)


GATE_DTYPE = jnp.bfloat16


def _pair_words(x):
    bits = pltpu.bitcast(x.astype(F32), jnp.uint32)
    return bits | (bits >> 16)


def _packed_rows(word_row, rows):
    return pltpu.bitcast(jnp.broadcast_to(word_row, (rows // 2, word_row.shape[-1])), GATE_DTYPE)


def _bitonic_merge_desc(vals):
    vals = list(vals)
    n = len(vals)
    j = n // 2
    while j >= 1:
        for i in range(n):
            l = i ^ j
            if l > i:
                vals[i], vals[l] = jnp.maximum(vals[i], vals[l]), jnp.minimum(vals[i], vals[l])
        j //= 2
    return vals


def _bitonic_sort_desc(vals):
    vals = list(vals)
    n = len(vals)
    k = 2
    while k <= n:
        j = k // 2
        while j >= 1:
            for i in range(n):
                l = i ^ j
                if l > i:
                    hi, lo = jnp.maximum(vals[i], vals[l]), jnp.minimum(vals[i], vals[l])
                    vals[i], vals[l] = (hi, lo) if (i & k) == 0 else (lo, hi)
            j //= 2
        k *= 2
    return vals


def _extract_top(s, count, roll=None):
    roll = roll or (lambda v, r: pltpu.roll(v, r, 0))
    sub = 8
    assert s.shape[0] == sub * count
    slabs = _bitonic_sort_desc([s[sub * k:sub * (k + 1)] for k in range(count)])
    r = sub // 2
    while r >= 1:
        other = [roll(v, r) for v in slabs]
        slabs = _bitonic_merge_desc([jnp.maximum(slabs[k], other[count - 1 - k]) for k in range(count)])
        r //= 2
    return [v[0:1] for v in slabs]


def _peer_select_kernel(x_ref, sh_ref, sc_ref, wq_ref, keys_ref, hm_ref, a_ref, b_ref, tau_ref):
    hm = (x_ref[...] * (1.0 + sc_ref[0]) + sh_ref[0]).astype(BF16)
    hm_ref[...] = hm.T
    q = _dot(hm, wq_ref[...]).astype(BF16)
    dk = PEER_NKEYS
    as_gate = lambda v: v.astype(GATE_DTYPE).astype(F32)
    tops = [[None, None] for _ in range(PEER_HEADS)]
    for h in range(PEER_HEADS):
        for p in range(2):
            j = h * 2 + p
            s = _dot_nt(keys_ref[j], q[:, j * dk:(j + 1) * dk])
            top = _extract_top(s, PEER_TOPK)
            e = jnp.where(s >= top[-1], jnp.exp(s - top[0]), 0.0).astype(GATE_DTYPE)
            tops[h][p] = [as_gate(jnp.exp(t - top[0])) for t in top]
            if p == 0:
                a_ref[h] = _pair_words(e)
            else:
                b_ref[h] = e

    stack = lambda rows_per_head: jnp.concatenate(rows_per_head, axis=0)
    a_rank = [stack([tops[h][0][r] for h in range(PEER_HEADS)]) for r in range(PEER_TOPK)]
    b_rank = [stack([tops[h][1][r] for h in range(PEER_HEADS)]) for r in range(PEER_TOPK)]
    likely = [a_rank[r] * b_rank[c] for r in range(PEER_TOPK) for c in range(PEER_TOPK)
              if (r + 1) * (c + 1) <= PEER_TOPK]
    padded = likely + [jnp.full_like(likely[0], -1.0)] * (64 - len(likely))
    kth = _bitonic_sort_desc(padded)[PEER_TOPK - 1]
    a_all = jnp.stack(a_rank)[:, None]
    b_all = jnp.stack(b_rank)[None]
    cand = a_all * b_all
    chosen = cand >= kth
    z = jnp.sum(jnp.where(chosen, cand, 0.0), axis=(0, 1))
    rz = 1.0 / z
    scaled = (a_all.astype(GATE_DTYPE) * (b_all * rz).astype(GATE_DTYPE)).astype(F32)
    tau = jnp.min(jnp.where(chosen, scaled, jnp.inf), axis=(0, 1))
    tau_ref[...] = _pair_words(tau.astype(GATE_DTYPE))
    for h in range(PEER_HEADS):
        b_ref[h] = (b_ref[h].astype(F32) * rz[h:h + 1, :]).astype(GATE_DTYPE)


def _peer_select(x, sh, sc, wq, keys, rows_per_cond):
    n, d = x.shape
    tb = TOKEN_BLOCK
    bpc = rows_per_cond // tb
    vec = pl.BlockSpec((1, 1, d), lambda b: (b // bpc, 0, 0))
    grid_hn = lambda: pl.BlockSpec((PEER_HEADS, PEER_NKEYS, tb), lambda b: (0, 0, b))
    return pl.pallas_call(
        _peer_select_kernel,
        grid=(n // tb,),
        in_specs=[
            pl.BlockSpec((tb, d), lambda b: (b, 0)),
            vec, vec,
            pl.BlockSpec(wq.shape, lambda b: (0, 0)),
            pl.BlockSpec(keys.shape, lambda b: (0, 0, 0)),
        ],
        out_specs=[
            pl.BlockSpec((d, tb), lambda b: (0, b)),
            grid_hn(), grid_hn(),
            pl.BlockSpec((PEER_HEADS, tb), lambda b: (0, b)),
        ],
        out_shape=[
            jax.ShapeDtypeStruct((d, n), BF16),
            jax.ShapeDtypeStruct((PEER_HEADS, PEER_NKEYS, n), jnp.uint32),
            jax.ShapeDtypeStruct((PEER_HEADS, PEER_NKEYS, n), GATE_DTYPE),
            jax.ShapeDtypeStruct((PEER_HEADS, n), jnp.uint32),
        ],
        compiler_params=_params("arbitrary"),
        name="peer_select",
    )(x, sh, sc, wq, keys)


def _gelu(x):
    return 0.5 * x * (1.0 + lax.erf(x * (2.0 ** -0.5)))


def _peer_expert_kernel(hm_ref, a_ref, b_ref, tau_ref, u_ref, v_ref, o_ref, ht_ref, gh_ref, *, tok_block):
    c = pl.program_id(1)

    @pl.when(c == 0)
    def _():
        o_ref[...] = jnp.zeros_like(o_ref)

    rows_per_chunk = PEER_EXPERT_CHUNK // PEER_NKEYS
    group = PEER_TOKEN_GROUP
    n_groups = tok_block // group
    v_t = v_ref[0]

    halves = PEER_EXPERT_SPLIT
    half_rows = rows_per_chunk // halves
    half = PEER_EXPERT_CHUNK // halves

    def pre_activations(g, eh):
        tok = pl.ds(g * group, group)
        ex = pl.ds(eh * half, half)
        ht_ref[ex, tok] = _dot(u_ref[0, ex, :], hm_ref[:, tok])

    def gated_activations(g, eh):
        for sub in range(group // LANES):
            tok = pl.ds(g * group + sub * LANES, LANES)
            for i in range(eh * half_rows, (eh + 1) * half_rows):
                rows = pl.ds(i * PEER_NKEYS, PEER_NKEYS)
                gate = jnp.zeros((PEER_NKEYS, LANES), GATE_DTYPE)
                for h in range(PEER_HEADS):
                    e = _packed_rows(a_ref[h, i:i + 1, tok], PEER_NKEYS) * b_ref[h, :, tok]
                    tau = _packed_rows(tau_ref[h:h + 1, tok], PEER_NKEYS)
                    gate = jnp.where(e >= tau, gate + e, gate)
                act = _gelu(ht_ref[rows, tok]).astype(GATE_DTYPE)
                gh_ref[rows, tok] = (gate * act).astype(BF16)

    def accumulate(g):
        tok = pl.ds(g * group, group)
        o_ref[:, tok] += _dot(v_t, gh_ref[:, tok])

    units = [(g, eh) for g in range(n_groups) for eh in range(halves)]
    for k in range(len(units) + 1):
        if k < len(units):
            pre_activations(*units[k])
        if k >= 1:
            g, eh = units[k - 1]
            gated_activations(g, eh)
            if eh == halves - 1:
                accumulate(g)


def _peer_experts(hm, a_t, b_t, tau, u_all, v_all, layer):
    d, n = hm.shape
    tb = PEER_TOKEN_BLOCK
    ec = PEER_EXPERT_CHUNK
    n_chunks = N_EXPERTS // ec
    rpc = ec // PEER_NKEYS
    return pl.pallas_call(
        functools.partial(_peer_expert_kernel, tok_block=tb),
        grid=(n // tb, n_chunks),
        in_specs=[
            pl.BlockSpec((d, tb), lambda b, c: (0, b)),
            pl.BlockSpec((PEER_HEADS, rpc, tb), lambda b, c: (0, c, b)),
            pl.BlockSpec((PEER_HEADS, PEER_NKEYS, tb), lambda b, c: (0, 0, b)),
            pl.BlockSpec((PEER_HEADS, tb), lambda b, c: (0, b)),
            pl.BlockSpec((1, ec, d), lambda b, c: (layer, c, 0)),
            pl.BlockSpec((1, d, ec), lambda b, c: (layer, 0, c)),
        ],
        out_specs=pl.BlockSpec((d, tb), lambda b, c: (0, b)),
        out_shape=jax.ShapeDtypeStruct((d, n), F32),
        scratch_shapes=[
            pltpu.VMEM((ec, tb), F32),
            pltpu.VMEM((ec, tb), BF16),
        ],
        compiler_params=_params("arbitrary", "arbitrary"),
        name="peer_experts",
    )(hm, a_t, b_t, tau, u_all, v_all)


def _residual_ln_kernel(x_ref, f_ref, g_ref, lg_ref, lb_ref, o_ref):
    y = ALPHA * x_ref[...] + g_ref[0] * f_ref[...].T
    o_ref[...] = _layernorm_rows(y, lg_ref[...], lb_ref[...])


def _residual_ln(x, f, gate, ln_g, ln_b, rows_per_cond):
    n, d = x.shape
    tb = TOKEN_BLOCK
    bpc = rows_per_cond // tb
    row = pl.BlockSpec((tb, d), lambda b: (b, 0))
    return pl.pallas_call(
        _residual_ln_kernel,
        grid=(n // tb,),
        in_specs=[row, pl.BlockSpec((d, tb), lambda b: (0, b)),
                  pl.BlockSpec((1, 1, d), lambda b: (b // bpc, 0, 0)),
                  pl.BlockSpec((1, d), lambda b: (0, 0)),
                  pl.BlockSpec((1, d), lambda b: (0, 0))],
        out_specs=row,
        out_shape=jax.ShapeDtypeStruct((n, d), F32),
        compiler_params=_params("arbitrary"),
        name="residual_ln",
    )(x, f, gate, ln_g, ln_b)


def _rope_tables(seq_len):
    t = np.arange(seq_len)
    row = (t // GRID_W).astype(np.float32)
    col = (t % GRID_W).astype(np.float32)
    n_freq = HEAD_DIM // 4
    inv = (ROPE_BASE ** (-np.arange(n_freq, dtype=np.float32) / n_freq)).astype(np.float32)
    ang = jnp.asarray(np.concatenate([row[:, None] * inv, col[:, None] * inv], -1).astype(np.float32))
    cos, sin = jnp.cos(ang), jnp.sin(ang)
    cos_t = jnp.tile(jnp.concatenate([cos, cos], -1), (1, H_GROUP))
    sin_t = jnp.tile(jnp.concatenate([-sin, sin], -1), (1, H_GROUP))
    return cos_t, sin_t


def _swap_halves_perm():
    idx = np.arange(W_GROUP)
    half = HEAD_DIM // 2
    return np.where((idx % HEAD_DIM) < half, idx + half, idx - half)


def _attention_bias(rpb, rows):
    cols = np.arange(GRID_W)
    c_start = np.clip(cols - NA_WC // 2, 0, GRID_W - NA_WC)
    col_in = (cols[None, :] >= c_start[:, None]) & (cols[None, :] < c_start[:, None] + NA_WC)
    col_idx = np.clip(cols[None, :] - cols[:, None], 1 - NA_WC, NA_WC - 1) + NA_WC - 1
    pick = jnp.asarray((col_idx[None] == np.arange(2 * NA_WC - 1)[:, None, None]).astype(np.float32))
    rpb_cols = jnp.einsum('hdj,jqk->hdqk', rpb, pick, precision=lax.Precision.HIGHEST)
    masked = jnp.where(jnp.asarray(col_in)[None, None], rpb_cols, NEG_BIG)
    n_rel = 2 * NA_WR - 1
    flat = masked.transpose(0, 2, 1, 3).reshape(H_GROUP * GRID_W, n_rel * GRID_W)
    wide = jnp.pad(flat, ((0, 0), (0, (n_rel + 3) * GRID_W - flat.shape[1])))
    keep = (n_rel + 1) * GRID_W
    return jnp.stack([wide[:, :keep], wide[:, GRID_W:GRID_W + keep]], axis=0)


def _block_diag_states(s0):
    b = s0.shape[0]
    eye = jnp.eye(H_GROUP, dtype=s0.dtype)
    full = jnp.einsum('bchde,hg->bchdge', s0, eye)
    return full.reshape(b, 2, H_GROUP * HEAD_DIM, H_GROUP * HEAD_DIM)


def kernel(x_prompt, x_sample, c, cache_na_k, cache_na_v, state_ret, c_ctx, w_mod, b_mod, w_in, w_out,
           na_rpb, ret_decay, sc_w, cf_w, cf_ln_g, cf_ln_b, ln1_g, ln1_b, ln2_g, ln2_b,
           peer_wq, peer_keys, peer_u, peer_v):
    n_req, seq, d = x_prompt.shape
    n_lat, lat_seq, _ = x_sample.shape
    past = cache_na_k.shape[2]
    rows = lat_seq // GRID_W

    cond8 = jnp.concatenate([c_ctx[None], c, jnp.zeros((8 - 1 - n_lat, d), F32)], axis=0)
    mod = _modulation(cond8, w_mod, b_mod)

    def mod_vec(l, j, lo, hi):
        return mod[l, lo:hi, j * d:(j + 1) * d][:, None, :]

    perm = _swap_halves_perm()
    w_in_b = w_in.astype(BF16)
    w_in_lat = jnp.concatenate([w_in_b, w_in_b[:, :, 3 * W_GROUP + perm], w_in_b[:, :, 4 * W_GROUP + perm]], -1)
    w_out_b = w_out.astype(BF16)
    wq_b = peer_wq.astype(BF16)
    keys_b = peer_keys.astype(BF16).reshape(DEPTH, PEER_HEADS * 2, PEER_NKEYS, PEER_NKEYS)
    u_b = peer_u.astype(BF16)
    v_b = jnp.swapaxes(peer_v, 1, 2).astype(BF16)

    lane_h = np.arange(W_GROUP) // HEAD_DIM
    avg = jnp.asarray((lane_h[:, None] == lane_h[None, :]).astype(np.float32) / HEAD_DIM).astype(BF16)
    cos_t, sin_t = _rope_tables(lat_seq)
    dec_tiles = jnp.broadcast_to(ret_decay.reshape(DEPTH, 2 * H_GROUP, 1), (DEPTH, 8, LANES))
    scw = jnp.pad(sc_w, ((0, 0), (0, 8 - SC_WIDTH), (0, 0)))
    cfw = jnp.pad(cf_w, ((0, 0), (0, 32 - CF_WIDTH), (0, 0)))
    kc_all = cache_na_k.reshape(n_lat, DEPTH, past, W_GROUP)
    vc_all = cache_na_v.reshape(n_lat, DEPTH, past, W_GROUP)

    xp = x_prompt.reshape(n_req * seq, d)
    xs = x_sample.reshape(n_lat * lat_seq, d)
    ks_out, vs_out, ss_out = [], [], []
    ctx_pending = None

    for l in range(DEPTH):
        row1 = lambda a: a[l][None, :]
        cfg, cfb = row1(cf_ln_g), row1(cf_ln_b)
        l1g, l1b, l2g, l2b = row1(ln1_g), row1(ln1_b), row1(ln2_g), row1(ln2_b)

        n_ctx = n_req * seq
        xp, k_ctx, v_ctx, s_ctx = _ctx_layer(
            xp, mod_vec(l, 0, 0, 1), mod_vec(l, 1, 0, 1), mod_vec(l, 2, 0, 1), w_in_b[l], w_out_b[l], l1g, l1b,
            n_req, seq, dec_tiles[l], avg, scw[l], cfw[l], cfg, cfb, prev=ctx_pending)
        hm, a_t, b_t, tau = _peer_select(xp, mod_vec(l, 3, 0, 1), mod_vec(l, 4, 0, 1), wq_b[l], keys_b[l], n_ctx)
        ffn = _peer_experts(hm, a_t, b_t, tau, u_b, v_b, l)
        ctx_pending = (ffn, mod_vec(l, 5, 0, 1), l2g, l2b)
        ks_out.append(k_ctx.reshape(n_req, seq, H_GROUP, HEAD_DIM))
        vs_out.append(v_ctx.reshape(n_req, seq, H_GROUP, HEAD_DIM))
        ss_out.append(s_ctx)

        hi = 1 + n_lat
        proj = _modmm(xs, mod_vec(l, 0, 1, hi), mod_vec(l, 1, 1, hi), w_in_lat[l], lat_seq)
        bias = _attention_bias(na_rpb[l], rows)
        o_a = _lat_attention(proj, n_lat, lat_seq, kc_all[:, l], vc_all[:, l], bias)
        o_b = _lat_retention(proj, n_lat, lat_seq, cos_t, sin_t, _block_diag_states(state_ret[:, l]),
                             dec_tiles[l], avg)
        o_cd = _lat_convs(proj, n_lat, lat_seq, scw[l], cfw[l], cfg, cfb)
        xs = _outproj_ln([o_a, o_b, o_cd], xs, mod_vec(l, 2, 1, hi), w_out_b[l], l1g, l1b, lat_seq)
        hm, a_t, b_t, tau = _peer_select(xs, mod_vec(l, 3, 1, hi), mod_vec(l, 4, 1, hi), wq_b[l], keys_b[l], lat_seq)
        ffn = _peer_experts(hm, a_t, b_t, tau, u_b, v_b, l)
        xs = _residual_ln(xs, ffn, mod_vec(l, 5, 1, hi), l2g, l2b, lat_seq)

    xp = _residual_ln(xp, *ctx_pending, n_req * seq)
    return (xp.reshape(n_req, seq, d), xs.reshape(n_lat, lat_seq, d),
            jnp.stack(ks_out, axis=1), jnp.stack(vs_out, axis=1), jnp.stack(ss_out, axis=1))
```

```python
import functools
import math

import jax
import jax.numpy as jnp
import numpy as np
from jax import lax
from jax.experimental import pallas as pl
from jax.experimental.pallas import tpu as pltpu

F32 = jnp.float32
BF16 = jnp.bfloat16

D_MODEL = 1024
DEPTH = 4
GRID_W = 64
HEAD_DIM = 64
W_GROUP = 256
H_GROUP = 4
N_IN_PARTS = 12
NA_WR = 8
NA_WC = 16
ROPE_BASE = 10000.0
SC_WIDTH = 3
CF_WIDTH = 31
PEER_HEADS = 8
PEER_NKEYS = 128
PEER_TOPK = 16
N_EXPERTS = PEER_NKEYS * PEER_NKEYS
LN_EPS = 1e-5
ALPHA = (2.0 * DEPTH) ** 0.25
NEG_BIG = -1e30

VMEM_LIMIT_BYTES = 56 * 1024 * 1024
LANES = 128

TOKEN_BLOCK = 512
PEER_TOKEN_BLOCK = 1024
PEER_TOKEN_GROUP = 256
PEER_EXPERT_CHUNK = 2048
PEER_EXPERT_SPLIT = 1


def _params(*sem):
    return pltpu.CompilerParams(dimension_semantics=sem, vmem_limit_bytes=VMEM_LIMIT_BYTES)


def _sigmoid(x):
    return 1.0 / (1.0 + jnp.exp(-x))


def _silu(x):
    return x * _sigmoid(x)


def _layernorm_rows(x, g, b):
    mu = jnp.mean(x, axis=-1, keepdims=True)
    d = x - mu
    var = jnp.mean(d * d, axis=-1, keepdims=True)
    return d * lax.rsqrt(var + LN_EPS) * g + b


def _dot(a, b):
    return jnp.dot(a, b, preferred_element_type=F32)


def _dot_nt(a, b):
    return lax.dot_general(a, b, (((1,), (1,)), ((), ())), preferred_element_type=F32)


def _dot_f32(a, b):
    hi = a.astype(BF16)
    lo = (a - hi.astype(F32)).astype(BF16)
    return _dot(hi, b) + _dot(lo, b)


def _head_masks():
    lane = lax.broadcasted_iota(jnp.int32, (1, W_GROUP), 1)
    return [((lane >= h * HEAD_DIM) & (lane < (h + 1) * HEAD_DIM)).astype(F32) for h in range(H_GROUP)]


def _mod_kernel(c_ref, w_ref, b_ref, o_ref):
    c = c_ref[...]
    s = _silu(c)
    o_ref[0] = jnp.dot(s, w_ref[0], precision=lax.Precision.HIGHEST,
                       preferred_element_type=F32) + b_ref[0]


def _modulation(cond8, w_mod, b_mod):
    nc = 1024
    n_out = w_mod.shape[-1]
    return pl.pallas_call(
        _mod_kernel,
        grid=(DEPTH, n_out // nc),
        in_specs=[
            pl.BlockSpec((8, D_MODEL), lambda l, j: (0, 0)),
            pl.BlockSpec((1, D_MODEL, nc), lambda l, j: (l, 0, j)),
            pl.BlockSpec((1, 1, nc), lambda l, j: (l, 0, j)),
        ],
        out_specs=pl.BlockSpec((1, 8, nc), lambda l, j: (l, 0, j)),
        out_shape=jax.ShapeDtypeStruct((DEPTH, 8, n_out), F32),
        compiler_params=_params("arbitrary", "arbitrary"),
        name="modulation",
    )(cond8, w_mod, b_mod.reshape(DEPTH, 1, n_out))


def _modmm_kernel(x_ref, sh_ref, sc_ref, w_ref, o_ref):
    h = x_ref[...] * (1.0 + sc_ref[0]) + sh_ref[0]
    o_ref[...] = _dot(h.astype(BF16), w_ref[...])


def _modmm(x, sh, sc, w, rows_per_cond):
    n, d = x.shape
    n_out = w.shape[1]
    tb = TOKEN_BLOCK
    bpc = rows_per_cond // tb
    vec = pl.BlockSpec((1, 1, d), lambda b: (b // bpc, 0, 0))
    return pl.pallas_call(
        _modmm_kernel,
        grid=(n // tb,),
        in_specs=[
            pl.BlockSpec((tb, d), lambda b: (b, 0)),
            vec, vec,
            pl.BlockSpec((d, n_out), lambda b: (0, 0)),
        ],
        out_specs=pl.BlockSpec((tb, n_out), lambda b: (b, 0)),
        out_shape=jax.ShapeDtypeStruct((n, n_out), F32),
        compiler_params=_params("arbitrary"),
        name="modulated_projection",
    )(x, sh, sc, w)


def _log_gamma_tile(decay_tile):
    y = -decay_tile
    return -(jnp.maximum(y, 0.0) + jnp.log1p(jnp.exp(-jnp.abs(y))))


def _lane_vector(lg_tile, row0, masks):
    out = lg_tile[row0:row0 + 1, 0:1] * masks[0]
    for h in range(1, H_GROUP):
        out = out + lg_tile[row0 + h:row0 + h + 1, 0:1] * masks[h]
    return out


def _retention_pairs(rq, rk, rv, lg_tile, masks, seq_len, q_block):
    kb = rk.astype(BF16)
    vb = rv.astype(BF16)
    blocks = []
    for q0 in range(0, seq_len, q_block):
        rqb = rq[q0:q0 + q_block]
        ti = lax.broadcasted_iota(jnp.int32, (q_block, seq_len), 0) + q0
        si = lax.broadcasted_iota(jnp.int32, (q_block, seq_len), 1)
        dist = (ti - si).astype(F32)
        fwd = jnp.maximum(dist, 0.0)
        bwd = jnp.maximum(-dist, 0.0)
        acc = jnp.zeros((q_block, W_GROUP), F32)
        for h in range(H_GROUP):
            lgf = lg_tile[h:h + 1, 0:1]
            lgb = lg_tile[H_GROUP + h:H_GROUP + h + 1, 0:1]
            decay = (jnp.where(dist >= 0, jnp.exp(lgf * fwd), 0.0)
                     + jnp.where(dist <= 0, jnp.exp(lgb * bwd), 0.0))
            s = _dot_nt((rqb * masks[h]).astype(BF16), kb)
            acc = acc + _dot((s * decay).astype(BF16), vb) * masks[h]
        blocks.append(acc)
    return blocks[0] if len(blocks) == 1 else jnp.concatenate(blocks, axis=0)


def _retention_finish(o, r_g, avg):
    mu = _dot_f32(o, avg)
    d = o - mu
    var = _dot_f32(d * d, avg)
    return _silu(r_g) * (d * lax.rsqrt(var + LN_EPS))


def _shift_rows(z, off, seq_len):
    if off == 0:
        return z
    rolled = pltpu.roll(z, (-off) % seq_len, 0)
    t = lax.broadcasted_iota(jnp.int32, z.shape, 0)
    valid = (t + off >= 0) & (t + off < seq_len)
    return jnp.where(valid, rolled, 0.0)


def _dwconv(z, w_ref, width, seq_len):
    pad = (width - 1) // 2
    acc = None
    for k in range(width):
        term = _shift_rows(z, k - pad, seq_len) * w_ref[k:k + 1, :]
        acc = term if acc is None else acc + term
    return acc


def _conv_mixers(sc_b, sc_c, sc_x, cf_a, cf_gate, scw_ref, cfw_ref, cfg_ref, cfb_ref, seq_len):
    o_c = sc_b * _dwconv(sc_c * sc_x, scw_ref, SC_WIDTH, seq_len)
    u = _dwconv(cf_a * _sigmoid(cf_gate), cfw_ref, CF_WIDTH, seq_len)
    o_d = _silu(_layernorm_rows(u, cfg_ref[...], cfb_ref[...]))
    return o_c, o_d


def _ctx_mix_request(p_ref, dec_ref, avg_ref, scw_ref, cfw_ref, cfg_ref, cfb_ref,
                     mix_ref, k_ref, v_ref, st_ref, seq_len):
    masks = _head_masks()
    part = lambda j: p_ref[:, j * W_GROUP:(j + 1) * W_GROUP]
    na_q, na_k, na_v = part(0), part(1), part(2)
    k_ref[...] = na_k
    v_ref[...] = na_v

    kb = na_k.astype(BF16)
    vb = na_v.astype(BF16)
    qs = na_q * (HEAD_DIM ** -0.5)
    o_a = jnp.zeros((seq_len, W_GROUP), F32)
    for h in range(H_GROUP):
        s = _dot_nt((qs * masks[h]).astype(BF16), kb)
        m = jnp.max(s, axis=-1, keepdims=True)
        e = jnp.exp(s - m)
        p = e / jnp.sum(e, axis=-1, keepdims=True)
        o_a = o_a + _dot(p.astype(BF16), vb) * masks[h]
    mix_ref[:, 0:W_GROUP] = o_a

    lg = _log_gamma_tile(dec_ref[...])
    rq, rv, r_g = part(3), part(5), part(6)
    rk = part(4) * (HEAD_DIM ** -0.5)
    o = _retention_pairs(rq, rk, rv, lg, masks, seq_len, seq_len)
    mix_ref[:, W_GROUP:2 * W_GROUP] = _retention_finish(o, r_g, avg_ref[...])

    pos = lax.broadcasted_iota(jnp.int32, (seq_len, W_GROUP), 0).astype(F32)
    vb_r = rv.astype(BF16)
    for d in range(2):
        lane_lg = _lane_vector(lg, d * H_GROUP, masks)
        expo = (seq_len - 1.0 - pos) if d == 0 else pos
        kd = rk * jnp.exp(lane_lg * expo)
        full = _dot(kd.T.astype(BF16), vb_r)
        for h in range(H_GROUP):
            st_ref[d, h] = full[h * HEAD_DIM:(h + 1) * HEAD_DIM, h * HEAD_DIM:(h + 1) * HEAD_DIM]

    o_c, o_d = _conv_mixers(part(7), part(8), part(9), part(10), part(11),
                            scw_ref, cfw_ref, cfg_ref, cfb_ref, seq_len)
    mix_ref[:, 2 * W_GROUP:3 * W_GROUP] = o_c
    mix_ref[:, 3 * W_GROUP:4 * W_GROUP] = o_d


def _ctx_layer_kernel(*refs, seq_len, n_sub, has_prev):
    if has_prev:
        f_ref, pg_ref, plg_ref, plb_ref = refs[:4]
        refs = refs[4:]
    (x_ref, sh_ref, sc_ref, g_ref, win_ref, wout_ref, lg_ref, lb_ref,
     dec_ref, avg_ref, scw_ref, cfw_ref, cfg_ref, cfb_ref,
     o_ref, k_ref, v_ref, st_ref, proj_ref, mix_ref) = refs
    x = x_ref[...]
    if has_prev:
        x = _layernorm_rows(ALPHA * x + pg_ref[0] * f_ref[...].T, plg_ref[...], plb_ref[...])
    h = x * (1.0 + sc_ref[0]) + sh_ref[0]
    proj_ref[...] = _dot(h.astype(BF16), win_ref[...])
    for r in range(n_sub):
        rows = pl.ds(r * seq_len, seq_len)
        _ctx_mix_request(proj_ref.at[rows], dec_ref, avg_ref, scw_ref, cfw_ref, cfg_ref, cfb_ref,
                         mix_ref.at[rows], k_ref.at[rows], v_ref.at[rows], st_ref.at[r], seq_len)
    mix = _dot(mix_ref[...].astype(BF16), wout_ref[...])
    y = ALPHA * x + g_ref[0] * mix
    o_ref[...] = _layernorm_rows(y, lg_ref[...], lb_ref[...])


def _ctx_layer(x, sh, sc, gate, w_in, w_out, ln_g, ln_b, n_req, seq_len, dec_tile, avg, scw, cfw, cfg, cfb,
               prev=None):
    n, d = x.shape
    n_sub = TOKEN_BLOCK // seq_len
    tb = n_sub * seq_len
    n_in = w_in.shape[1]
    const = lambda shape: pl.BlockSpec(shape, lambda b: (0,) * len(shape))
    rows = lambda width: pl.BlockSpec((tb, width), lambda b: (b, 0))
    prev_specs = [] if prev is None else [pl.BlockSpec((d, tb), lambda b: (0, b)),
                                          const((1, 1, d)), const((1, d)), const((1, d))]
    prev_args = () if prev is None else tuple(prev)
    return pl.pallas_call(
        functools.partial(_ctx_layer_kernel, seq_len=seq_len, n_sub=n_sub, has_prev=prev is not None),
        grid=(n // tb,),
        in_specs=prev_specs + [
            rows(d), const((1, 1, d)), const((1, 1, d)), const((1, 1, d)),
            const((d, n_in)), const((d, d)), const((1, d)), const((1, d)),
            const((8, LANES)), const((W_GROUP, W_GROUP)), const((8, W_GROUP)),
            const((32, W_GROUP)), const((1, W_GROUP)), const((1, W_GROUP)),
        ],
        out_specs=[
            rows(d), rows(W_GROUP), rows(W_GROUP),
            pl.BlockSpec((n_sub, 2, H_GROUP, HEAD_DIM, HEAD_DIM), lambda b: (b, 0, 0, 0, 0)),
        ],
        out_shape=[
            jax.ShapeDtypeStruct((n, d), F32),
            jax.ShapeDtypeStruct((n, W_GROUP), F32),
            jax.ShapeDtypeStruct((n, W_GROUP), F32),
            jax.ShapeDtypeStruct((n_req, 2, H_GROUP, HEAD_DIM, HEAD_DIM), F32),
        ],
        scratch_shapes=[
            pltpu.VMEM((tb, n_in), F32),
            pltpu.VMEM((tb, d), F32),
        ],
        compiler_params=_params("arbitrary"),
        name="context_layer_mixing",
    )(*prev_args, x, sh, sc, gate, w_in, w_out, ln_g, ln_b, dec_tile, avg, scw, cfw, cfg, cfb)


def _lat_attn_kernel(q_ref, k_ref, v_ref, kc_ref, vc_ref, bias_ref, o_ref, *, rows):
    masks = _head_masks()
    kcb = kc_ref[0].astype(BF16)
    vcb = vc_ref[0].astype(BF16)
    wr = min(NA_WR, rows)

    def row_block(r, carry):
        rs = jnp.clip(r - wr // 2, 0, rows - wr)
        q0 = pl.multiple_of(r * GRID_W, GRID_W)
        k0 = pl.multiple_of(rs * GRID_W, GRID_W)
        qb = q_ref[pl.ds(q0, GRID_W), :] * (HEAD_DIM ** -0.5)
        qs = jnp.concatenate([qb * masks[h] for h in range(H_GROUP)], axis=0).astype(BF16)
        kl = k_ref[pl.ds(k0, wr * GRID_W), :].astype(BF16)
        vl = v_ref[pl.ds(k0, wr * GRID_W), :].astype(BF16)
        d0 = rs - r + (NA_WR - 1)
        off = pl.multiple_of((d0 // 2) * (2 * GRID_W), 2 * GRID_W)
        width = wr * GRID_W
        bias = jnp.where(d0 % 2 == 0, bias_ref[0, :, pl.ds(off, width)], bias_ref[1, :, pl.ds(off, width)])
        s_loc = _dot_nt(qs, kl) + bias
        s_ctx = _dot_nt(qs, kcb)
        m = jnp.maximum(jnp.max(s_loc, axis=-1, keepdims=True), jnp.max(s_ctx, axis=-1, keepdims=True))
        e_loc = jnp.exp(s_loc - m)
        e_ctx = jnp.exp(s_ctx - m)
        z = jnp.sum(e_loc, axis=-1, keepdims=True) + jnp.sum(e_ctx, axis=-1, keepdims=True)
        o = (_dot(e_loc.astype(BF16), vl) + _dot(e_ctx.astype(BF16), vcb)) / z
        acc = o[0:GRID_W] * masks[0]
        for h in range(1, H_GROUP):
            acc = acc + o[h * GRID_W:(h + 1) * GRID_W] * masks[h]
        o_ref[pl.ds(q0, GRID_W), :] = acc
        return carry

    lax.fori_loop(0, rows, row_block, 0)


def _lat_attention(proj, n_req, seq_len, kc, vc, bias):
    rows = seq_len // GRID_W
    part = lambda j: pl.BlockSpec((seq_len, W_GROUP), lambda b, j=j: (b, j))
    past = kc.shape[1]
    return pl.pallas_call(
        functools.partial(_lat_attn_kernel, rows=rows),
        grid=(n_req,),
        in_specs=[
            part(0), part(1), part(2),
            pl.BlockSpec((1, past, W_GROUP), lambda b: (b, 0, 0)),
            pl.BlockSpec((1, past, W_GROUP), lambda b: (b, 0, 0)),
            pl.BlockSpec(bias.shape, lambda b: (0, 0, 0)),
        ],
        out_specs=pl.BlockSpec((seq_len, W_GROUP), lambda b: (b, 0)),
        out_shape=jax.ShapeDtypeStruct((n_req * seq_len, W_GROUP), F32),
        compiler_params=_params("arbitrary"),
        name="latent_attention",
    )(proj, proj, proj, kc, vc, bias)


def _lat_ret_kernel(q_ref, k_ref, v_ref, g_ref, qs_ref, ks_ref, cos_ref, sin_ref, s0_ref,
                    dec_ref, avg_ref, o_ref, *, seq_len):
    masks = _head_masks()
    lg = _log_gamma_tile(dec_ref[...])
    cos = cos_ref[...]
    sin = sin_ref[...]
    rq = q_ref[...] * cos + qs_ref[...] * sin
    rk = (k_ref[...] * cos + ks_ref[...] * sin) * (HEAD_DIM ** -0.5)
    rv = v_ref[...]
    o = _retention_pairs(rq, rk, rv, lg, masks, seq_len, 256)
    pos = lax.broadcasted_iota(jnp.int32, (seq_len, W_GROUP), 0).astype(F32)
    rqb = rq.astype(BF16)
    lgf = _lane_vector(lg, 0, masks)
    lgb = _lane_vector(lg, H_GROUP, masks)
    o = o + _dot(rqb, s0_ref[0, 0].astype(BF16)) * jnp.exp(lgf * (pos + 1.0))
    o = o + _dot(rqb, s0_ref[0, 1].astype(BF16)) * jnp.exp(lgb * (seq_len - pos))
    o_ref[...] = _retention_finish(o, g_ref[...], avg_ref[...])


def _lat_retention(proj, n_req, seq_len, cos, sin, s0_bd, dec_tile, avg):
    part = lambda j: pl.BlockSpec((seq_len, W_GROUP), lambda b, j=j: (b, j))
    const = lambda shape: pl.BlockSpec(shape, lambda b: (0,) * len(shape))
    return pl.pallas_call(
        functools.partial(_lat_ret_kernel, seq_len=seq_len),
        grid=(n_req,),
        in_specs=[
            part(3), part(4), part(5), part(6), part(12), part(13),
            const((seq_len, W_GROUP)), const((seq_len, W_GROUP)),
            pl.BlockSpec((1, 2, W_GROUP, W_GROUP), lambda b: (b, 0, 0, 0)),
            const((8, LANES)), const((W_GROUP, W_GROUP)),
        ],
        out_specs=pl.BlockSpec((seq_len, W_GROUP), lambda b: (b, 0)),
        out_shape=jax.ShapeDtypeStruct((n_req * seq_len, W_GROUP), F32),
        compiler_params=_params("arbitrary"),
        name="latent_retention",
    )(proj, proj, proj, proj, proj, proj, cos, sin, s0_bd, dec_tile, avg)


def _lat_conv_kernel(b_ref, c_ref, x_ref, a_ref, gate_ref, scw_ref, cfw_ref, cfg_ref, cfb_ref,
                     o_ref, *, seq_len):
    o_c, o_d = _conv_mixers(b_ref[...], c_ref[...], x_ref[...], a_ref[...], gate_ref[...],
                            scw_ref, cfw_ref, cfg_ref, cfb_ref, seq_len)
    o_ref[:, 0:W_GROUP] = o_c
    o_ref[:, W_GROUP:2 * W_GROUP] = o_d


def _lat_convs(proj, n_req, seq_len, scw, cfw, cfg, cfb):
    part = lambda j: pl.BlockSpec((seq_len, W_GROUP), lambda b, j=j: (b, j))
    const = lambda shape: pl.BlockSpec(shape, lambda b: (0,) * len(shape))
    return pl.pallas_call(
        functools.partial(_lat_conv_kernel, seq_len=seq_len),
        grid=(n_req,),
        in_specs=[
            part(7), part(8), part(9), part(10), part(11),
            const((8, W_GROUP)), const((32, W_GROUP)), const((1, W_GROUP)), const((1, W_GROUP)),
        ],
        out_specs=pl.BlockSpec((seq_len, 2 * W_GROUP), lambda b: (b, 0)),
        out_shape=jax.ShapeDtypeStruct((n_req * seq_len, 2 * W_GROUP), F32),
        compiler_params=_params("arbitrary"),
        name="latent_convs",
    )(proj, proj, proj, proj, proj, scw, cfw, cfg, cfb)


def _outproj_kernel(*refs, widths):
    n_parts = len(widths)
    part_refs = refs[:n_parts]
    x_ref, g_ref, w_ref, lg_ref, lb_ref, o_ref = refs[n_parts:]
    mix = None
    row = 0
    for p_ref, wd in zip(part_refs, widths):
        term = _dot(p_ref[...].astype(BF16), w_ref[row:row + wd, :])
        mix = term if mix is None else mix + term
        row += wd
    y = ALPHA * x_ref[...] + g_ref[0] * mix
    o_ref[...] = _layernorm_rows(y, lg_ref[...], lb_ref[...])


def _outproj_ln(parts, x, gate, w_out, ln_g, ln_b, rows_per_cond):
    n, d = x.shape
    tb = TOKEN_BLOCK
    bpc = rows_per_cond // tb
    widths = tuple(p.shape[1] for p in parts)
    return pl.pallas_call(
        functools.partial(_outproj_kernel, widths=widths),
        grid=(n // tb,),
        in_specs=[pl.BlockSpec((tb, wd), lambda b: (b, 0)) for wd in widths] + [
            pl.BlockSpec((tb, d), lambda b: (b, 0)),
            pl.BlockSpec((1, 1, d), lambda b: (b // bpc, 0, 0)),
            pl.BlockSpec((d, d), lambda b: (0, 0)),
            pl.BlockSpec((1, d), lambda b: (0, 0)),
            pl.BlockSpec((1, d), lambda b: (0, 0)),
        ],
        out_specs=pl.BlockSpec((tb, d), lambda b: (b, 0)),
        out_shape=jax.ShapeDtypeStruct((n, d), F32),
        compiler_params=_params("arbitrary"),
        name="output_projection_ln",
    )(*parts, x, gate, w_out, ln_g, ln_b)


GATE_DTYPE = jnp.bfloat16


def _pair_words(x):
    bits = pltpu.bitcast(x.astype(F32), jnp.uint32)
    return bits | (bits >> 16)


def _packed_rows(word_row, rows):
    return pltpu.bitcast(jnp.broadcast_to(word_row, (rows // 2, word_row.shape[-1])), GATE_DTYPE)


def _bitonic_merge_desc(vals):
    vals = list(vals)
    n = len(vals)
    j = n // 2
    while j >= 1:
        for i in range(n):
            l = i ^ j
            if l > i:
                vals[i], vals[l] = jnp.maximum(vals[i], vals[l]), jnp.minimum(vals[i], vals[l])
        j //= 2
    return vals


def _bitonic_sort_desc(vals):
    vals = list(vals)
    n = len(vals)
    k = 2
    while k <= n:
        j = k // 2
        while j >= 1:
            for i in range(n):
                l = i ^ j
                if l > i:
                    hi, lo = jnp.maximum(vals[i], vals[l]), jnp.minimum(vals[i], vals[l])
                    vals[i], vals[l] = (hi, lo) if (i & k) == 0 else (lo, hi)
            j //= 2
        k *= 2
    return vals


def _extract_top(s, count, roll=None):
    roll = roll or (lambda v, r: pltpu.roll(v, r, 0))
    sub = 8
    assert s.shape[0] == sub * count
    slabs = _bitonic_sort_desc([s[sub * k:sub * (k + 1)] for k in range(count)])
    r = sub // 2
    while r >= 1:
        other = [roll(v, r) for v in slabs]
        slabs = _bitonic_merge_desc([jnp.maximum(slabs[k], other[count - 1 - k]) for k in range(count)])
        r //= 2
    return [v[0:1] for v in slabs]


def _peer_select_kernel(x_ref, sh_ref, sc_ref, wq_ref, keys_ref, hm_ref, a_ref, b_ref, tau_ref):
    hm = (x_ref[...] * (1.0 + sc_ref[0]) + sh_ref[0]).astype(BF16)
    hm_ref[...] = hm.T
    q = _dot(hm, wq_ref[...]).astype(BF16)
    dk = PEER_NKEYS
    as_gate = lambda v: v.astype(GATE_DTYPE).astype(F32)
    tops = [[None, None] for _ in range(PEER_HEADS)]
    for h in range(PEER_HEADS):
        for p in range(2):
            j = h * 2 + p
            s = _dot_nt(keys_ref[j], q[:, j * dk:(j + 1) * dk])
            top = _extract_top(s, PEER_TOPK)
            e = jnp.where(s >= top[-1], jnp.exp(s - top[0]), 0.0).astype(GATE_DTYPE)
            tops[h][p] = [as_gate(jnp.exp(t - top[0])) for t in top]
            if p == 0:
                a_ref[h] = _pair_words(e)
            else:
                b_ref[h] = e

    stack = lambda rows_per_head: jnp.concatenate(rows_per_head, axis=0)
    a_rank = [stack([tops[h][0][r] for h in range(PEER_HEADS)]) for r in range(PEER_TOPK)]
    b_rank = [stack([tops[h][1][r] for h in range(PEER_HEADS)]) for r in range(PEER_TOPK)]
    likely = [a_rank[r] * b_rank[c] for r in range(PEER_TOPK) for c in range(PEER_TOPK)
              if (r + 1) * (c + 1) <= PEER_TOPK]
    padded = likely + [jnp.full_like(likely[0], -1.0)] * (64 - len(likely))
    kth = _bitonic_sort_desc(padded)[PEER_TOPK - 1]
    a_all = jnp.stack(a_rank)[:, None]
    b_all = jnp.stack(b_rank)[None]
    cand = a_all * b_all
    chosen = cand >= kth
    z = jnp.sum(jnp.where(chosen, cand, 0.0), axis=(0, 1))
    rz = 1.0 / z
    scaled = (a_all.astype(GATE_DTYPE) * (b_all * rz).astype(GATE_DTYPE)).astype(F32)
    tau = jnp.min(jnp.where(chosen, scaled, jnp.inf), axis=(0, 1))
    tau_ref[...] = _pair_words(tau.astype(GATE_DTYPE))
    for h in range(PEER_HEADS):
        b_ref[h] = (b_ref[h].astype(F32) * rz[h:h + 1, :]).astype(GATE_DTYPE)


def _peer_select(x, sh, sc, wq, keys, rows_per_cond):
    n, d = x.shape
    tb = TOKEN_BLOCK
    bpc = rows_per_cond // tb
    vec = pl.BlockSpec((1, 1, d), lambda b: (b // bpc, 0, 0))
    grid_hn = lambda: pl.BlockSpec((PEER_HEADS, PEER_NKEYS, tb), lambda b: (0, 0, b))
    return pl.pallas_call(
        _peer_select_kernel,
        grid=(n // tb,),
        in_specs=[
            pl.BlockSpec((tb, d), lambda b: (b, 0)),
            vec, vec,
            pl.BlockSpec(wq.shape, lambda b: (0, 0)),
            pl.BlockSpec(keys.shape, lambda b: (0, 0, 0)),
        ],
        out_specs=[
            pl.BlockSpec((d, tb), lambda b: (0, b)),
            grid_hn(), grid_hn(),
            pl.BlockSpec((PEER_HEADS, tb), lambda b: (0, b)),
        ],
        out_shape=[
            jax.ShapeDtypeStruct((d, n), BF16),
            jax.ShapeDtypeStruct((PEER_HEADS, PEER_NKEYS, n), jnp.uint32),
            jax.ShapeDtypeStruct((PEER_HEADS, PEER_NKEYS, n), GATE_DTYPE),
            jax.ShapeDtypeStruct((PEER_HEADS, n), jnp.uint32),
        ],
        compiler_params=_params("arbitrary"),
        name="peer_select",
    )(x, sh, sc, wq, keys)


def _gelu(x):
    return 0.5 * x * (1.0 + lax.erf(x * (2.0 ** -0.5)))


def _peer_expert_kernel(hm_ref, a_ref, b_ref, tau_ref, u_ref, v_ref, o_ref, ht_ref, gh_ref, *, tok_block):
    c = pl.program_id(1)

    @pl.when(c == 0)
    def _():
        o_ref[...] = jnp.zeros_like(o_ref)

    rows_per_chunk = PEER_EXPERT_CHUNK // PEER_NKEYS
    group = PEER_TOKEN_GROUP
    n_groups = tok_block // group
    v_t = v_ref[0]

    halves = PEER_EXPERT_SPLIT
    half_rows = rows_per_chunk // halves
    half = PEER_EXPERT_CHUNK // halves

    def pre_activations(g, eh):
        tok = pl.ds(g * group, group)
        ex = pl.ds(eh * half, half)
        ht_ref[ex, tok] = _dot(u_ref[0, ex, :], hm_ref[:, tok])

    def gated_activations(g, eh):
        for sub in range(group // LANES):
            tok = pl.ds(g * group + sub * LANES, LANES)
            for i in range(eh * half_rows, (eh + 1) * half_rows):
                rows = pl.ds(i * PEER_NKEYS, PEER_NKEYS)
                gate = jnp.zeros((PEER_NKEYS, LANES), GATE_DTYPE)
                for h in range(PEER_HEADS):
                    e = _packed_rows(a_ref[h, i:i + 1, tok], PEER_NKEYS) * b_ref[h, :, tok]
                    tau = _packed_rows(tau_ref[h:h + 1, tok], PEER_NKEYS)
                    gate = jnp.where(e >= tau, gate + e, gate)
                act = _gelu(ht_ref[rows, tok]).astype(GATE_DTYPE)
                gh_ref[rows, tok] = (gate * act).astype(BF16)

    def accumulate(g):
        tok = pl.ds(g * group, group)
        o_ref[:, tok] += _dot(v_t, gh_ref[:, tok])

    units = [(g, eh) for g in range(n_groups) for eh in range(halves)]
    for k in range(len(units) + 1):
        if k < len(units):
            pre_activations(*units[k])
        if k >= 1:
            g, eh = units[k - 1]
            gated_activations(g, eh)
            if eh == halves - 1:
                accumulate(g)


def _peer_experts(hm, a_t, b_t, tau, u_all, v_all, layer):
    d, n = hm.shape
    tb = PEER_TOKEN_BLOCK
    ec = PEER_EXPERT_CHUNK
    n_chunks = N_EXPERTS // ec
    rpc = ec // PEER_NKEYS
    return pl.pallas_call(
        functools.partial(_peer_expert_kernel, tok_block=tb),
        grid=(n // tb, n_chunks),
        in_specs=[
            pl.BlockSpec((d, tb), lambda b, c: (0, b)),
            pl.BlockSpec((PEER_HEADS, rpc, tb), lambda b, c: (0, c, b)),
            pl.BlockSpec((PEER_HEADS, PEER_NKEYS, tb), lambda b, c: (0, 0, b)),
            pl.BlockSpec((PEER_HEADS, tb), lambda b, c: (0, b)),
            pl.BlockSpec((1, ec, d), lambda b, c: (layer, c, 0)),
            pl.BlockSpec((1, d, ec), lambda b, c: (layer, 0, c)),
        ],
        out_specs=pl.BlockSpec((d, tb), lambda b, c: (0, b)),
        out_shape=jax.ShapeDtypeStruct((d, n), F32),
        scratch_shapes=[
            pltpu.VMEM((ec, tb), F32),
            pltpu.VMEM((ec, tb), BF16),
        ],
        compiler_params=_params("arbitrary", "arbitrary"),
        name="peer_experts",
    )(hm, a_t, b_t, tau, u_all, v_all)


def _residual_ln_kernel(x_ref, f_ref, g_ref, lg_ref, lb_ref, o_ref):
    y = ALPHA * x_ref[...] + g_ref[0] * f_ref[...].T
    o_ref[...] = _layernorm_rows(y, lg_ref[...], lb_ref[...])


def _residual_ln(x, f, gate, ln_g, ln_b, rows_per_cond):
    n, d = x.shape
    tb = TOKEN_BLOCK
    bpc = rows_per_cond // tb
    row = pl.BlockSpec((tb, d), lambda b: (b, 0))
    return pl.pallas_call(
        _residual_ln_kernel,
        grid=(n // tb,),
        in_specs=[row, pl.BlockSpec((d, tb), lambda b: (0, b)),
                  pl.BlockSpec((1, 1, d), lambda b: (b // bpc, 0, 0)),
                  pl.BlockSpec((1, d), lambda b: (0, 0)),
                  pl.BlockSpec((1, d), lambda b: (0, 0))],
        out_specs=row,
        out_shape=jax.ShapeDtypeStruct((n, d), F32),
        compiler_params=_params("arbitrary"),
        name="residual_ln",
    )(x, f, gate, ln_g, ln_b)


def _expert_tables_kernel(u_ref, v_ref, ub_ref, vt_ref):
    ub_ref[0] = u_ref[0].astype(BF16)
    vt_ref[0] = v_ref[0].astype(BF16).T


def _expert_tables(peer_u, peer_v):
    depth, n_exp, d = peer_u.shape
    eb = 1024
    rows = pl.BlockSpec((1, eb, d), lambda l, j: (l, j, 0))
    return pl.pallas_call(
        _expert_tables_kernel,
        grid=(depth, n_exp // eb),
        in_specs=[rows, rows],
        out_specs=[rows, pl.BlockSpec((1, d, eb), lambda l, j: (l, 0, j))],
        out_shape=[jax.ShapeDtypeStruct((depth, n_exp, d), BF16),
                   jax.ShapeDtypeStruct((depth, d, n_exp), BF16)],
        compiler_params=_params("arbitrary", "arbitrary"),
        name="expert_tables_bf16",
    )(peer_u, peer_v)


def _rope_tables(seq_len):
    t = np.arange(seq_len)
    row = (t // GRID_W).astype(np.float32)
    col = (t % GRID_W).astype(np.float32)
    n_freq = HEAD_DIM // 4
    inv = (ROPE_BASE ** (-np.arange(n_freq, dtype=np.float32) / n_freq)).astype(np.float32)
    ang = jnp.asarray(np.concatenate([row[:, None] * inv, col[:, None] * inv], -1).astype(np.float32))
    cos, sin = jnp.cos(ang), jnp.sin(ang)
    cos_t = jnp.tile(jnp.concatenate([cos, cos], -1), (1, H_GROUP))
    sin_t = jnp.tile(jnp.concatenate([-sin, sin], -1), (1, H_GROUP))
    return cos_t, sin_t


def _swap_halves_perm():
    idx = np.arange(W_GROUP)
    half = HEAD_DIM // 2
    return np.where((idx % HEAD_DIM) < half, idx + half, idx - half)


def _attention_bias(rpb, rows):
    cols = np.arange(GRID_W)
    c_start = np.clip(cols - NA_WC // 2, 0, GRID_W - NA_WC)
    col_in = (cols[None, :] >= c_start[:, None]) & (cols[None, :] < c_start[:, None] + NA_WC)
    col_idx = np.clip(cols[None, :] - cols[:, None], 1 - NA_WC, NA_WC - 1) + NA_WC - 1
    pick = jnp.asarray((col_idx[None] == np.arange(2 * NA_WC - 1)[:, None, None]).astype(np.float32))
    rpb_cols = jnp.einsum('hdj,jqk->hdqk', rpb, pick, precision=lax.Precision.HIGHEST)
    masked = jnp.where(jnp.asarray(col_in)[None, None], rpb_cols, NEG_BIG)
    n_rel = 2 * NA_WR - 1
    flat = masked.transpose(0, 2, 1, 3).reshape(H_GROUP * GRID_W, n_rel * GRID_W)
    wide = jnp.pad(flat, ((0, 0), (0, (n_rel + 3) * GRID_W - flat.shape[1])))
    keep = (n_rel + 1) * GRID_W
    return jnp.stack([wide[:, :keep], wide[:, GRID_W:GRID_W + keep]], axis=0)


def _block_diag_states(s0):
    b = s0.shape[0]
    eye = jnp.eye(H_GROUP, dtype=s0.dtype)
    full = jnp.einsum('bchde,hg->bchdge', s0, eye)
    return full.reshape(b, 2, H_GROUP * HEAD_DIM, H_GROUP * HEAD_DIM)


def kernel(x_prompt, x_sample, c, cache_na_k, cache_na_v, state_ret, c_ctx, w_mod, b_mod, w_in, w_out,
           na_rpb, ret_decay, sc_w, cf_w, cf_ln_g, cf_ln_b, ln1_g, ln1_b, ln2_g, ln2_b,
           peer_wq, peer_keys, peer_u, peer_v):
    n_req, seq, d = x_prompt.shape
    n_lat, lat_seq, _ = x_sample.shape
    past = cache_na_k.shape[2]
    rows = lat_seq // GRID_W

    cond8 = jnp.concatenate([c_ctx[None], c, jnp.zeros((8 - 1 - n_lat, d), F32)], axis=0)
    mod = _modulation(cond8, w_mod, b_mod)

    def mod_vec(l, j, lo, hi):
        return mod[l, lo:hi, j * d:(j + 1) * d][:, None, :]

    perm = _swap_halves_perm()
    w_in_b = w_in.astype(BF16)
    w_in_lat = jnp.concatenate([w_in_b, w_in_b[:, :, 3 * W_GROUP + perm], w_in_b[:, :, 4 * W_GROUP + perm]], -1)
    w_out_b = w_out.astype(BF16)
    wq_b = peer_wq.astype(BF16)
    keys_b = peer_keys.astype(BF16).reshape(DEPTH, PEER_HEADS * 2, PEER_NKEYS, PEER_NKEYS)
    u_b, v_b = _expert_tables(peer_u, peer_v)

    lane_h = np.arange(W_GROUP) // HEAD_DIM
    avg = jnp.asarray((lane_h[:, None] == lane_h[None, :]).astype(np.float32) / HEAD_DIM).astype(BF16)
    cos_t, sin_t = _rope_tables(lat_seq)
    dec_tiles = jnp.broadcast_to(ret_decay.reshape(DEPTH, 2 * H_GROUP, 1), (DEPTH, 8, LANES))
    scw = jnp.pad(sc_w, ((0, 0), (0, 8 - SC_WIDTH), (0, 0)))
    cfw = jnp.pad(cf_w, ((0, 0), (0, 32 - CF_WIDTH), (0, 0)))
    kc_all = cache_na_k.reshape(n_lat, DEPTH, past, W_GROUP)
    vc_all = cache_na_v.reshape(n_lat, DEPTH, past, W_GROUP)

    xp = x_prompt.reshape(n_req * seq, d)
    xs = x_sample.reshape(n_lat * lat_seq, d)
    ks_out, vs_out, ss_out = [], [], []
    ctx_pending = None

    for l in range(DEPTH):
        row1 = lambda a: a[l][None, :]
        cfg, cfb = row1(cf_ln_g), row1(cf_ln_b)
        l1g, l1b, l2g, l2b = row1(ln1_g), row1(ln1_b), row1(ln2_g), row1(ln2_b)

        n_ctx = n_req * seq
        xp, k_ctx, v_ctx, s_ctx = _ctx_layer(
            xp, mod_vec(l, 0, 0, 1), mod_vec(l, 1, 0, 1), mod_vec(l, 2, 0, 1), w_in_b[l], w_out_b[l], l1g, l1b,
            n_req, seq, dec_tiles[l], avg, scw[l], cfw[l], cfg, cfb, prev=ctx_pending)
        hm, a_t, b_t, tau = _peer_select(xp, mod_vec(l, 3, 0, 1), mod_vec(l, 4, 0, 1), wq_b[l], keys_b[l], n_ctx)
        ffn = _peer_experts(hm, a_t, b_t, tau, u_b, v_b, l)
        ctx_pending = (ffn, mod_vec(l, 5, 0, 1), l2g, l2b)
        ks_out.append(k_ctx.reshape(n_req, seq, H_GROUP, HEAD_DIM))
        vs_out.append(v_ctx.reshape(n_req, seq, H_GROUP, HEAD_DIM))
        ss_out.append(s_ctx)

        hi = 1 + n_lat
        proj = _modmm(xs, mod_vec(l, 0, 1, hi), mod_vec(l, 1, 1, hi), w_in_lat[l], lat_seq)
        bias = _attention_bias(na_rpb[l], rows)
        o_a = _lat_attention(proj, n_lat, lat_seq, kc_all[:, l], vc_all[:, l], bias)
        o_b = _lat_retention(proj, n_lat, lat_seq, cos_t, sin_t, _block_diag_states(state_ret[:, l]),
                             dec_tiles[l], avg)
        o_cd = _lat_convs(proj, n_lat, lat_seq, scw[l], cfw[l], cfg, cfb)
        xs = _outproj_ln([o_a, o_b, o_cd], xs, mod_vec(l, 2, 1, hi), w_out_b[l], l1g, l1b, lat_seq)
        hm, a_t, b_t, tau = _peer_select(xs, mod_vec(l, 3, 1, hi), mod_vec(l, 4, 1, hi), wq_b[l], keys_b[l], lat_seq)
        ffn = _peer_experts(hm, a_t, b_t, tau, u_b, v_b, l)
        xs = _residual_ln(xs, ffn, mod_vec(l, 5, 1, hi), l2g, l2b, lat_seq)

    xp = _residual_ln(xp, *ctx_pending, n_req * seq)
    return (xp.reshape(n_req, seq, d), xs.reshape(n_lat, lat_seq, d),
            jnp.stack(ks_out, axis=1), jnp.stack(vs_out, axis=1), jnp.stack(ss_out, axis=1))
```

```python
import functools
import math

import jax
import jax.numpy as jnp
import numpy as np
from jax import lax
from jax.experimental import pallas as pl
from jax.experimental.pallas import tpu as pltpu

F32 = jnp.float32
BF16 = jnp.bfloat16

D_MODEL = 1024
DEPTH = 4
GRID_W = 64
HEAD_DIM = 64
W_GROUP = 256
H_GROUP = 4
N_IN_PARTS = 12
NA_WR = 8
NA_WC = 16
ROPE_BASE = 10000.0
SC_WIDTH = 3
CF_WIDTH = 31
PEER_HEADS = 8
PEER_NKEYS = 128
PEER_TOPK = 16
N_EXPERTS = PEER_NKEYS * PEER_NKEYS
LN_EPS = 1e-5
ALPHA = (2.0 * DEPTH) ** 0.25
NEG_BIG = -1e30

VMEM_LIMIT_BYTES = 56 * 1024 * 1024
LANES = 128

TOKEN_BLOCK = 512
PEER_TOKEN_BLOCK = 1024
PEER_TOKEN_GROUP = 256
PEER_EXPERT_CHUNK = 2048
PEER_EXPERT_SPLIT = 1


def _params(*sem):
    return pltpu.CompilerParams(dimension_semantics=sem, vmem_limit_bytes=VMEM_LIMIT_BYTES)


def _sigmoid(x):
    return 1.0 / (1.0 + jnp.exp(-x))


def _silu(x):
    return x * _sigmoid(x)


def _layernorm_rows(x, g, b):
    mu = jnp.mean(x, axis=-1, keepdims=True)
    d = x - mu
    var = jnp.mean(d * d, axis=-1, keepdims=True)
    return d * lax.rsqrt(var + LN_EPS) * g + b


def _dot(a, b):
    return jnp.dot(a, b, preferred_element_type=F32)


def _dot_nt(a, b):
    return lax.dot_general(a, b, (((1,), (1,)), ((), ())), preferred_element_type=F32)


def _dot_f32(a, b):
    hi = a.astype(BF16)
    lo = (a - hi.astype(F32)).astype(BF16)
    return _dot(hi, b) + _dot(lo, b)


def _head_masks():
    lane = lax.broadcasted_iota(jnp.int32, (1, W_GROUP), 1)
    return [((lane >= h * HEAD_DIM) & (lane < (h + 1) * HEAD_DIM)).astype(F32) for h in range(H_GROUP)]


def _mod_kernel(c_ref, w_ref, b_ref, o_ref):
    c = c_ref[...]
    s = _silu(c)
    o_ref[0] = jnp.dot(s, w_ref[0], precision=lax.Precision.HIGHEST,
                       preferred_element_type=F32) + b_ref[0]


def _modulation(cond8, w_mod, b_mod):
    nc = 1024
    n_out = w_mod.shape[-1]
    return pl.pallas_call(
        _mod_kernel,
        grid=(DEPTH, n_out // nc),
        in_specs=[
            pl.BlockSpec((8, D_MODEL), lambda l, j: (0, 0)),
            pl.BlockSpec((1, D_MODEL, nc), lambda l, j: (l, 0, j)),
            pl.BlockSpec((1, 1, nc), lambda l, j: (l, 0, j)),
        ],
        out_specs=pl.BlockSpec((1, 8, nc), lambda l, j: (l, 0, j)),
        out_shape=jax.ShapeDtypeStruct((DEPTH, 8, n_out), F32),
        compiler_params=_params("arbitrary", "arbitrary"),
        name="modulation",
    )(cond8, w_mod, b_mod.reshape(DEPTH, 1, n_out))


def _modmm_kernel(x_ref, sh_ref, sc_ref, w_ref, o_ref):
    h = x_ref[...] * (1.0 + sc_ref[0]) + sh_ref[0]
    o_ref[...] = _dot(h.astype(BF16), w_ref[...])


def _modmm(x, sh, sc, w, rows_per_cond):
    n, d = x.shape
    n_out = w.shape[1]
    tb = TOKEN_BLOCK
    bpc = rows_per_cond // tb
    vec = pl.BlockSpec((1, 1, d), lambda b: (b // bpc, 0, 0))
    return pl.pallas_call(
        _modmm_kernel,
        grid=(n // tb,),
        in_specs=[
            pl.BlockSpec((tb, d), lambda b: (b, 0)),
            vec, vec,
            pl.BlockSpec((d, n_out), lambda b: (0, 0)),
        ],
        out_specs=pl.BlockSpec((tb, n_out), lambda b: (b, 0)),
        out_shape=jax.ShapeDtypeStruct((n, n_out), F32),
        compiler_params=_params("arbitrary"),
        name="modulated_projection",
    )(x, sh, sc, w)


def _log_gamma_tile(decay_tile):
    y = -decay_tile
    return -(jnp.maximum(y, 0.0) + jnp.log1p(jnp.exp(-jnp.abs(y))))


def _lane_vector(lg_tile, row0, masks):
    out = lg_tile[row0:row0 + 1, 0:1] * masks[0]
    for h in range(1, H_GROUP):
        out = out + lg_tile[row0 + h:row0 + h + 1, 0:1] * masks[h]
    return out


def _retention_pairs(rq, rk, rv, lg_tile, masks, seq_len, q_block):
    kb = rk.astype(BF16)
    vb = rv.astype(BF16)
    blocks = []
    for q0 in range(0, seq_len, q_block):
        rqb = rq[q0:q0 + q_block]
        ti = lax.broadcasted_iota(jnp.int32, (q_block, seq_len), 0) + q0
        si = lax.broadcasted_iota(jnp.int32, (q_block, seq_len), 1)
        dist = (ti - si).astype(F32)
        fwd = jnp.maximum(dist, 0.0)
        bwd = jnp.maximum(-dist, 0.0)
        acc = jnp.zeros((q_block, W_GROUP), F32)
        for h in range(H_GROUP):
            lgf = lg_tile[h:h + 1, 0:1]
            lgb = lg_tile[H_GROUP + h:H_GROUP + h + 1, 0:1]
            decay = (jnp.where(dist >= 0, jnp.exp(lgf * fwd), 0.0)
                     + jnp.where(dist <= 0, jnp.exp(lgb * bwd), 0.0))
            s = _dot_nt((rqb * masks[h]).astype(BF16), kb)
            acc = acc + _dot((s * decay).astype(BF16), vb) * masks[h]
        blocks.append(acc)
    return blocks[0] if len(blocks) == 1 else jnp.concatenate(blocks, axis=0)


def _retention_finish(o, r_g, avg):
    mu = _dot_f32(o, avg)
    d = o - mu
    var = _dot_f32(d * d, avg)
    return _silu(r_g) * (d * lax.rsqrt(var + LN_EPS))


def _shift_rows(z, off, seq_len):
    if off == 0:
        return z
    rolled = pltpu.roll(z, (-off) % seq_len, 0)
    t = lax.broadcasted_iota(jnp.int32, z.shape, 0)
    valid = (t + off >= 0) & (t + off < seq_len)
    return jnp.where(valid, rolled, 0.0)


def _dwconv(z, w_ref, width, seq_len):
    pad = (width - 1) // 2
    acc = None
    for k in range(width):
        term = _shift_rows(z, k - pad, seq_len) * w_ref[k:k + 1, :]
        acc = term if acc is None else acc + term
    return acc


def _conv_mixers(sc_b, sc_c, sc_x, cf_a, cf_gate, scw_ref, cfw_ref, cfg_ref, cfb_ref, seq_len):
    o_c = sc_b * _dwconv(sc_c * sc_x, scw_ref, SC_WIDTH, seq_len)
    u = _dwconv(cf_a * _sigmoid(cf_gate), cfw_ref, CF_WIDTH, seq_len)
    o_d = _silu(_layernorm_rows(u, cfg_ref[...], cfb_ref[...]))
    return o_c, o_d


def _ctx_mix_request(p_ref, dec_ref, avg_ref, scw_ref, cfw_ref, cfg_ref, cfb_ref,
                     mix_ref, k_ref, v_ref, st_ref, seq_len):
    masks = _head_masks()
    part = lambda j: p_ref[:, j * W_GROUP:(j + 1) * W_GROUP]
    na_q, na_k, na_v = part(0), part(1), part(2)
    k_ref[...] = na_k
    v_ref[...] = na_v

    kb = na_k.astype(BF16)
    vb = na_v.astype(BF16)
    qs = na_q * (HEAD_DIM ** -0.5)
    o_a = jnp.zeros((seq_len, W_GROUP), F32)
    for h in range(H_GROUP):
        s = _dot_nt((qs * masks[h]).astype(BF16), kb)
        m = jnp.max(s, axis=-1, keepdims=True)
        e = jnp.exp(s - m)
        p = e / jnp.sum(e, axis=-1, keepdims=True)
        o_a = o_a + _dot(p.astype(BF16), vb) * masks[h]
    mix_ref[:, 0:W_GROUP] = o_a

    lg = _log_gamma_tile(dec_ref[...])
    rq, rv, r_g = part(3), part(5), part(6)
    rk = part(4) * (HEAD_DIM ** -0.5)
    o = _retention_pairs(rq, rk, rv, lg, masks, seq_len, seq_len)
    mix_ref[:, W_GROUP:2 * W_GROUP] = _retention_finish(o, r_g, avg_ref[...])

    pos = lax.broadcasted_iota(jnp.int32, (seq_len, W_GROUP), 0).astype(F32)
    vb_r = rv.astype(BF16)
    for d in range(2):
        lane_lg = _lane_vector(lg, d * H_GROUP, masks)
        expo = (seq_len - 1.0 - pos) if d == 0 else pos
        kd = rk * jnp.exp(lane_lg * expo)
        full = _dot(kd.T.astype(BF16), vb_r)
        for h in range(H_GROUP):
            st_ref[d, h] = full[h * HEAD_DIM:(h + 1) * HEAD_DIM, h * HEAD_DIM:(h + 1) * HEAD_DIM]

    o_c, o_d = _conv_mixers(part(7), part(8), part(9), part(10), part(11),
                            scw_ref, cfw_ref, cfg_ref, cfb_ref, seq_len)
    mix_ref[:, 2 * W_GROUP:3 * W_GROUP] = o_c
    mix_ref[:, 3 * W_GROUP:4 * W_GROUP] = o_d


def _ctx_layer_kernel(*refs, seq_len, n_sub, has_prev):
    if has_prev:
        f_ref, pg_ref, plg_ref, plb_ref = refs[:4]
        refs = refs[4:]
    (x_ref, sh_ref, sc_ref, g_ref, win_ref, wout_ref, lg_ref, lb_ref,
     dec_ref, avg_ref, scw_ref, cfw_ref, cfg_ref, cfb_ref,
     o_ref, k_ref, v_ref, st_ref, proj_ref, mix_ref) = refs
    x = x_ref[...]
    if has_prev:
        x = _layernorm_rows(ALPHA * x + pg_ref[0] * f_ref[...].T, plg_ref[...], plb_ref[...])
    h = x * (1.0 + sc_ref[0]) + sh_ref[0]
    proj_ref[...] = _dot(h.astype(BF16), win_ref[...])
    for r in range(n_sub):
        rows = pl.ds(r * seq_len, seq_len)
        _ctx_mix_request(proj_ref.at[rows], dec_ref, avg_ref, scw_ref, cfw_ref, cfg_ref, cfb_ref,
                         mix_ref.at[rows], k_ref.at[rows], v_ref.at[rows], st_ref.at[r], seq_len)
    mix = _dot(mix_ref[...].astype(BF16), wout_ref[...])
    y = ALPHA * x + g_ref[0] * mix
    o_ref[...] = _layernorm_rows(y, lg_ref[...], lb_ref[...])


def _ctx_layer(x, sh, sc, gate, w_in, w_out, ln_g, ln_b, n_req, seq_len, dec_tile, avg, scw, cfw, cfg, cfb,
               prev=None):
    n, d = x.shape
    n_sub = TOKEN_BLOCK // seq_len
    tb = n_sub * seq_len
    n_in = w_in.shape[1]
    const = lambda shape: pl.BlockSpec(shape, lambda b: (0,) * len(shape))
    rows = lambda width: pl.BlockSpec((tb, width), lambda b: (b, 0))
    prev_specs = [] if prev is None else [pl.BlockSpec((d, tb), lambda b: (0, b)),
                                          const((1, 1, d)), const((1, d)), const((1, d))]
    prev_args = () if prev is None else tuple(prev)
    return pl.pallas_call(
        functools.partial(_ctx_layer_kernel, seq_len=seq_len, n_sub=n_sub, has_prev=prev is not None),
        grid=(n // tb,),
        in_specs=prev_specs + [
            rows(d), const((1, 1, d)), const((1, 1, d)), const((1, 1, d)),
            const((d, n_in)), const((d, d)), const((1, d)), const((1, d)),
            const((8, LANES)), const((W_GROUP, W_GROUP)), const((8, W_GROUP)),
            const((32, W_GROUP)), const((1, W_GROUP)), const((1, W_GROUP)),
        ],
        out_specs=[
            rows(d), rows(W_GROUP), rows(W_GROUP),
            pl.BlockSpec((n_sub, 2, H_GROUP, HEAD_DIM, HEAD_DIM), lambda b: (b, 0, 0, 0, 0)),
        ],
        out_shape=[
            jax.ShapeDtypeStruct((n, d), F32),
            jax.ShapeDtypeStruct((n, W_GROUP), F32),
            jax.ShapeDtypeStruct((n, W_GROUP), F32),
            jax.ShapeDtypeStruct((n_req, 2, H_GROUP, HEAD_DIM, HEAD_DIM), F32),
        ],
        scratch_shapes=[
            pltpu.VMEM((tb, n_in), F32),
            pltpu.VMEM((tb, d), F32),
        ],
        compiler_params=_params("arbitrary"),
        name="context_layer_mixing",
    )(*prev_args, x, sh, sc, gate, w_in, w_out, ln_g, ln_b, dec_tile, avg, scw, cfw, cfg, cfb)


def _lat_attn_kernel(q_ref, k_ref, v_ref, kc_ref, vc_ref, bias_ref, o_ref, *, rows):
    masks = _head_masks()
    kcb = kc_ref[0].astype(BF16)
    vcb = vc_ref[0].astype(BF16)
    wr = min(NA_WR, rows)

    def row_block(r, carry):
        rs = jnp.clip(r - wr // 2, 0, rows - wr)
        q0 = pl.multiple_of(r * GRID_W, GRID_W)
        k0 = pl.multiple_of(rs * GRID_W, GRID_W)
        qb = q_ref[pl.ds(q0, GRID_W), :] * (HEAD_DIM ** -0.5)
        qs = jnp.concatenate([qb * masks[h] for h in range(H_GROUP)], axis=0).astype(BF16)
        kl = k_ref[pl.ds(k0, wr * GRID_W), :].astype(BF16)
        vl = v_ref[pl.ds(k0, wr * GRID_W), :].astype(BF16)
        d0 = rs - r + (NA_WR - 1)
        off = pl.multiple_of((d0 // 2) * (2 * GRID_W), 2 * GRID_W)
        width = wr * GRID_W
        bias = jnp.where(d0 % 2 == 0, bias_ref[0, :, pl.ds(off, width)], bias_ref[1, :, pl.ds(off, width)])
        s_loc = _dot_nt(qs, kl) + bias
        s_ctx = _dot_nt(qs, kcb)
        m = jnp.maximum(jnp.max(s_loc, axis=-1, keepdims=True), jnp.max(s_ctx, axis=-1, keepdims=True))
        e_loc = jnp.exp(s_loc - m)
        e_ctx = jnp.exp(s_ctx - m)
        z = jnp.sum(e_loc, axis=-1, keepdims=True) + jnp.sum(e_ctx, axis=-1, keepdims=True)
        o = (_dot(e_loc.astype(BF16), vl) + _dot(e_ctx.astype(BF16), vcb)) / z
        acc = o[0:GRID_W] * masks[0]
        for h in range(1, H_GROUP):
            acc = acc + o[h * GRID_W:(h + 1) * GRID_W] * masks[h]
        o_ref[pl.ds(q0, GRID_W), :] = acc
        return carry

    lax.fori_loop(0, rows, row_block, 0)


def _lat_attention(proj, n_req, seq_len, kc, vc, bias):
    rows = seq_len // GRID_W
    part = lambda j: pl.BlockSpec((seq_len, W_GROUP), lambda b, j=j: (b, j))
    past = kc.shape[1]
    return pl.pallas_call(
        functools.partial(_lat_attn_kernel, rows=rows),
        grid=(n_req,),
        in_specs=[
            part(0), part(1), part(2),
            pl.BlockSpec((1, past, W_GROUP), lambda b: (b, 0, 0)),
            pl.BlockSpec((1, past, W_GROUP), lambda b: (b, 0, 0)),
            pl.BlockSpec(bias.shape, lambda b: (0, 0, 0)),
        ],
        out_specs=pl.BlockSpec((seq_len, W_GROUP), lambda b: (b, 0)),
        out_shape=jax.ShapeDtypeStruct((n_req * seq_len, W_GROUP), F32),
        compiler_params=_params("arbitrary"),
        name="latent_attention",
    )(proj, proj, proj, kc, vc, bias)


def _swap_head_halves(x):
    half = HEAD_DIM // 2
    lane = lax.broadcasted_iota(jnp.int32, x.shape, 1)
    return jnp.where((lane % HEAD_DIM) < half, pltpu.roll(x, W_GROUP - half, 1), pltpu.roll(x, half, 1))


def _lat_ret_kernel(q_ref, k_ref, v_ref, g_ref, cos_ref, sin_ref, s0_ref,
                    dec_ref, avg_ref, o_ref, *, seq_len):
    masks = _head_masks()
    lg = _log_gamma_tile(dec_ref[...])
    cos = cos_ref[...]
    sin = sin_ref[...]
    q = q_ref[...]
    k = k_ref[...]
    rq = q * cos + _swap_head_halves(q) * sin
    rk = (k * cos + _swap_head_halves(k) * sin) * (HEAD_DIM ** -0.5)
    rv = v_ref[...]
    o = _retention_pairs(rq, rk, rv, lg, masks, seq_len, 256)
    pos = lax.broadcasted_iota(jnp.int32, (seq_len, W_GROUP), 0).astype(F32)
    rqb = rq.astype(BF16)
    lgf = _lane_vector(lg, 0, masks)
    lgb = _lane_vector(lg, H_GROUP, masks)
    o = o + _dot(rqb, s0_ref[0, 0].astype(BF16)) * jnp.exp(lgf * (pos + 1.0))
    o = o + _dot(rqb, s0_ref[0, 1].astype(BF16)) * jnp.exp(lgb * (seq_len - pos))
    o_ref[...] = _retention_finish(o, g_ref[...], avg_ref[...])


def _lat_retention(proj, n_req, seq_len, cos, sin, s0_bd, dec_tile, avg):
    part = lambda j: pl.BlockSpec((seq_len, W_GROUP), lambda b, j=j: (b, j))
    const = lambda shape: pl.BlockSpec(shape, lambda b: (0,) * len(shape))
    return pl.pallas_call(
        functools.partial(_lat_ret_kernel, seq_len=seq_len),
        grid=(n_req,),
        in_specs=[
            part(3), part(4), part(5), part(6),
            const((seq_len, W_GROUP)), const((seq_len, W_GROUP)),
            pl.BlockSpec((1, 2, W_GROUP, W_GROUP), lambda b: (b, 0, 0, 0)),
            const((8, LANES)), const((W_GROUP, W_GROUP)),
        ],
        out_specs=pl.BlockSpec((seq_len, W_GROUP), lambda b: (b, 0)),
        out_shape=jax.ShapeDtypeStruct((n_req * seq_len, W_GROUP), F32),
        compiler_params=_params("arbitrary"),
        name="latent_retention",
    )(proj, proj, proj, proj, cos, sin, s0_bd, dec_tile, avg)


def _lat_conv_kernel(b_ref, c_ref, x_ref, a_ref, gate_ref, scw_ref, cfw_ref, cfg_ref, cfb_ref,
                     o_ref, *, seq_len):
    o_c, o_d = _conv_mixers(b_ref[...], c_ref[...], x_ref[...], a_ref[...], gate_ref[...],
                            scw_ref, cfw_ref, cfg_ref, cfb_ref, seq_len)
    o_ref[:, 0:W_GROUP] = o_c
    o_ref[:, W_GROUP:2 * W_GROUP] = o_d


def _lat_convs(proj, n_req, seq_len, scw, cfw, cfg, cfb):
    part = lambda j: pl.BlockSpec((seq_len, W_GROUP), lambda b, j=j: (b, j))
    const = lambda shape: pl.BlockSpec(shape, lambda b: (0,) * len(shape))
    return pl.pallas_call(
        functools.partial(_lat_conv_kernel, seq_len=seq_len),
        grid=(n_req,),
        in_specs=[
            part(7), part(8), part(9), part(10), part(11),
            const((8, W_GROUP)), const((32, W_GROUP)), const((1, W_GROUP)), const((1, W_GROUP)),
        ],
        out_specs=pl.BlockSpec((seq_len, 2 * W_GROUP), lambda b: (b, 0)),
        out_shape=jax.ShapeDtypeStruct((n_req * seq_len, 2 * W_GROUP), F32),
        compiler_params=_params("arbitrary"),
        name="latent_convs",
    )(proj, proj, proj, proj, proj, scw, cfw, cfg, cfb)


def _outproj_kernel(*refs, widths):
    n_parts = len(widths)
    part_refs = refs[:n_parts]
    x_ref, g_ref, w_ref, lg_ref, lb_ref, o_ref = refs[n_parts:]
    mix = None
    row = 0
    for p_ref, wd in zip(part_refs, widths):
        term = _dot(p_ref[...].astype(BF16), w_ref[row:row + wd, :])
        mix = term if mix is None else mix + term
        row += wd
    y = ALPHA * x_ref[...] + g_ref[0] * mix
    o_ref[...] = _layernorm_rows(y, lg_ref[...], lb_ref[...])


def _outproj_ln(parts, x, gate, w_out, ln_g, ln_b, rows_per_cond):
    n, d = x.shape
    tb = TOKEN_BLOCK
    bpc = rows_per_cond // tb
    widths = tuple(p.shape[1] for p in parts)
    return pl.pallas_call(
        functools.partial(_outproj_kernel, widths=widths),
        grid=(n // tb,),
        in_specs=[pl.BlockSpec((tb, wd), lambda b: (b, 0)) for wd in widths] + [
            pl.BlockSpec((tb, d), lambda b: (b, 0)),
            pl.BlockSpec((1, 1, d), lambda b: (b // bpc, 0, 0)),
            pl.BlockSpec((d, d), lambda b: (0, 0)),
            pl.BlockSpec((1, d), lambda b: (0, 0)),
            pl.BlockSpec((1, d), lambda b: (0, 0)),
        ],
        out_specs=pl.BlockSpec((tb, d), lambda b: (b, 0)),
        out_shape=jax.ShapeDtypeStruct((n, d), F32),
        compiler_params=_params("arbitrary"),
        name="output_projection_ln",
    )(*parts, x, gate, w_out, ln_g, ln_b)


GATE_DTYPE = jnp.bfloat16


def _pair_words(x):
    bits = pltpu.bitcast(x.astype(F32), jnp.uint32)
    return bits | (bits >> 16)


def _packed_rows(word_row, rows):
    return pltpu.bitcast(jnp.broadcast_to(word_row, (rows // 2, word_row.shape[-1])), GATE_DTYPE)


def _bitonic_merge_desc(vals):
    vals = list(vals)
    n = len(vals)
    j = n // 2
    while j >= 1:
        for i in range(n):
            l = i ^ j
            if l > i:
                vals[i], vals[l] = jnp.maximum(vals[i], vals[l]), jnp.minimum(vals[i], vals[l])
        j //= 2
    return vals


def _bitonic_sort_desc(vals):
    vals = list(vals)
    n = len(vals)
    k = 2
    while k <= n:
        j = k // 2
        while j >= 1:
            for i in range(n):
                l = i ^ j
                if l > i:
                    hi, lo = jnp.maximum(vals[i], vals[l]), jnp.minimum(vals[i], vals[l])
                    vals[i], vals[l] = (hi, lo) if (i & k) == 0 else (lo, hi)
            j //= 2
        k *= 2
    return vals


def _extract_top(s, count, roll=None):
    roll = roll or (lambda v, r: pltpu.roll(v, r, 0))
    sub = 8
    assert s.shape[0] == sub * count
    slabs = _bitonic_sort_desc([s[sub * k:sub * (k + 1)] for k in range(count)])
    r = sub // 2
    while r >= 1:
        other = [roll(v, r) for v in slabs]
        slabs = _bitonic_merge_desc([jnp.maximum(slabs[k], other[count - 1 - k]) for k in range(count)])
        r //= 2
    return [v[0:1] for v in slabs]


def _peer_select_kernel(x_ref, sh_ref, sc_ref, wq_ref, keys_ref, hm_ref, a_ref, b_ref, tau_ref):
    hm = (x_ref[...] * (1.0 + sc_ref[0]) + sh_ref[0]).astype(BF16)
    hm_ref[...] = hm.T
    q = _dot(hm, wq_ref[...]).astype(BF16)
    dk = PEER_NKEYS
    as_gate = lambda v: v.astype(GATE_DTYPE).astype(F32)
    tops = [[None, None] for _ in range(PEER_HEADS)]
    for h in range(PEER_HEADS):
        for p in range(2):
            j = h * 2 + p
            s = _dot_nt(keys_ref[j], q[:, j * dk:(j + 1) * dk])
            top = _extract_top(s, PEER_TOPK)
            e = jnp.where(s >= top[-1], jnp.exp(s - top[0]), 0.0).astype(GATE_DTYPE)
            tops[h][p] = [as_gate(jnp.exp(t - top[0])) for t in top]
            if p == 0:
                a_ref[h] = _pair_words(e)
            else:
                b_ref[h] = e

    stack = lambda rows_per_head: jnp.concatenate(rows_per_head, axis=0)
    a_rank = [stack([tops[h][0][r] for h in range(PEER_HEADS)]) for r in range(PEER_TOPK)]
    b_rank = [stack([tops[h][1][r] for h in range(PEER_HEADS)]) for r in range(PEER_TOPK)]
    likely = [a_rank[r] * b_rank[c] for r in range(PEER_TOPK) for c in range(PEER_TOPK)
              if (r + 1) * (c + 1) <= PEER_TOPK]
    padded = likely + [jnp.full_like(likely[0], -1.0)] * (64 - len(likely))
    kth = _bitonic_sort_desc(padded)[PEER_TOPK - 1]
    a_all = jnp.stack(a_rank)[:, None]
    b_all = jnp.stack(b_rank)[None]
    cand = a_all * b_all
    chosen = cand >= kth
    z = jnp.sum(jnp.where(chosen, cand, 0.0), axis=(0, 1))
    rz = 1.0 / z
    scaled = (a_all.astype(GATE_DTYPE) * (b_all * rz).astype(GATE_DTYPE)).astype(F32)
    tau = jnp.min(jnp.where(chosen, scaled, jnp.inf), axis=(0, 1))
    tau_ref[...] = _pair_words(tau.astype(GATE_DTYPE))
    for h in range(PEER_HEADS):
        b_ref[h] = (b_ref[h].astype(F32) * rz[h:h + 1, :]).astype(GATE_DTYPE)


def _peer_select(x, sh, sc, wq, keys, rows_per_cond):
    n, d = x.shape
    tb = TOKEN_BLOCK
    bpc = rows_per_cond // tb
    vec = pl.BlockSpec((1, 1, d), lambda b: (b // bpc, 0, 0))
    grid_hn = lambda: pl.BlockSpec((PEER_HEADS, PEER_NKEYS, tb), lambda b: (0, 0, b))
    return pl.pallas_call(
        _peer_select_kernel,
        grid=(n // tb,),
        in_specs=[
            pl.BlockSpec((tb, d), lambda b: (b, 0)),
            vec, vec,
            pl.BlockSpec(wq.shape, lambda b: (0, 0)),
            pl.BlockSpec(keys.shape, lambda b: (0, 0, 0)),
        ],
        out_specs=[
            pl.BlockSpec((d, tb), lambda b: (0, b)),
            grid_hn(), grid_hn(),
            pl.BlockSpec((PEER_HEADS, tb), lambda b: (0, b)),
        ],
        out_shape=[
            jax.ShapeDtypeStruct((d, n), BF16),
            jax.ShapeDtypeStruct((PEER_HEADS, PEER_NKEYS, n), jnp.uint32),
            jax.ShapeDtypeStruct((PEER_HEADS, PEER_NKEYS, n), GATE_DTYPE),
            jax.ShapeDtypeStruct((PEER_HEADS, n), jnp.uint32),
        ],
        compiler_params=_params("arbitrary"),
        name="peer_select",
    )(x, sh, sc, wq, keys)


def _gelu(x):
    return 0.5 * x * (1.0 + lax.erf(x * (2.0 ** -0.5)))


def _peer_expert_kernel(hm_ref, a_ref, b_ref, tau_ref, u_ref, v_ref, o_ref, ht_ref, gh_ref, *, tok_block):
    c = pl.program_id(1)

    @pl.when(c == 0)
    def _():
        o_ref[...] = jnp.zeros_like(o_ref)

    rows_per_chunk = PEER_EXPERT_CHUNK // PEER_NKEYS
    group = PEER_TOKEN_GROUP
    n_groups = tok_block // group
    v_t = v_ref[0]

    halves = PEER_EXPERT_SPLIT
    half_rows = rows_per_chunk // halves
    half = PEER_EXPERT_CHUNK // halves

    def pre_activations(g, eh):
        tok = pl.ds(g * group, group)
        ex = pl.ds(eh * half, half)
        ht_ref[ex, tok] = _dot(u_ref[0, ex, :], hm_ref[:, tok])

    def gated_activations(g, eh):
        for sub in range(group // LANES):
            tok = pl.ds(g * group + sub * LANES, LANES)
            for i in range(eh * half_rows, (eh + 1) * half_rows):
                rows = pl.ds(i * PEER_NKEYS, PEER_NKEYS)
                gate = jnp.zeros((PEER_NKEYS, LANES), GATE_DTYPE)
                for h in range(PEER_HEADS):
                    e = _packed_rows(a_ref[h, i:i + 1, tok], PEER_NKEYS) * b_ref[h, :, tok]
                    tau = _packed_rows(tau_ref[h:h + 1, tok], PEER_NKEYS)
                    gate = jnp.where(e >= tau, gate + e, gate)
                act = _gelu(ht_ref[rows, tok]).astype(GATE_DTYPE)
                gh_ref[rows, tok] = (gate * act).astype(BF16)

    def accumulate(g):
        tok = pl.ds(g * group, group)
        o_ref[:, tok] += _dot(v_t, gh_ref[:, tok])

    units = [(g, eh) for g in range(n_groups) for eh in range(halves)]
    for k in range(len(units) + 1):
        if k < len(units):
            pre_activations(*units[k])
        if k >= 1:
            g, eh = units[k - 1]
            gated_activations(g, eh)
            if eh == halves - 1:
                accumulate(g)


def _peer_experts(hm, a_t, b_t, tau, u_all, v_all, layer):
    d, n = hm.shape
    tb = PEER_TOKEN_BLOCK
    ec = PEER_EXPERT_CHUNK
    n_chunks = N_EXPERTS // ec
    rpc = ec // PEER_NKEYS
    return pl.pallas_call(
        functools.partial(_peer_expert_kernel, tok_block=tb),
        grid=(n // tb, n_chunks),
        in_specs=[
            pl.BlockSpec((d, tb), lambda b, c: (0, b)),
            pl.BlockSpec((PEER_HEADS, rpc, tb), lambda b, c: (0, c, b)),
            pl.BlockSpec((PEER_HEADS, PEER_NKEYS, tb), lambda b, c: (0, 0, b)),
            pl.BlockSpec((PEER_HEADS, tb), lambda b, c: (0, b)),
            pl.BlockSpec((1, ec, d), lambda b, c: (layer, c, 0)),
            pl.BlockSpec((1, d, ec), lambda b, c: (layer, 0, c)),
        ],
        out_specs=pl.BlockSpec((d, tb), lambda b, c: (0, b)),
        out_shape=jax.ShapeDtypeStruct((d, n), F32),
        scratch_shapes=[
            pltpu.VMEM((ec, tb), F32),
            pltpu.VMEM((ec, tb), BF16),
        ],
        compiler_params=_params("arbitrary", "arbitrary"),
        name="peer_experts",
    )(hm, a_t, b_t, tau, u_all, v_all)


def _residual_ln_kernel(x_ref, f_ref, g_ref, lg_ref, lb_ref, o_ref):
    y = ALPHA * x_ref[...] + g_ref[0] * f_ref[...].T
    o_ref[...] = _layernorm_rows(y, lg_ref[...], lb_ref[...])


def _residual_ln(x, f, gate, ln_g, ln_b, rows_per_cond):
    n, d = x.shape
    tb = TOKEN_BLOCK
    bpc = rows_per_cond // tb
    row = pl.BlockSpec((tb, d), lambda b: (b, 0))
    return pl.pallas_call(
        _residual_ln_kernel,
        grid=(n // tb,),
        in_specs=[row, pl.BlockSpec((d, tb), lambda b: (0, b)),
                  pl.BlockSpec((1, 1, d), lambda b: (b // bpc, 0, 0)),
                  pl.BlockSpec((1, d), lambda b: (0, 0)),
                  pl.BlockSpec((1, d), lambda b: (0, 0))],
        out_specs=row,
        out_shape=jax.ShapeDtypeStruct((n, d), F32),
        compiler_params=_params("arbitrary"),
        name="residual_ln",
    )(x, f, gate, ln_g, ln_b)


def _expert_tables_kernel(u_ref, v_ref, ub_ref, vt_ref):
    ub_ref[0] = u_ref[0].astype(BF16)
    vt_ref[0] = v_ref[0].astype(BF16).T


def _expert_tables(peer_u, peer_v):
    depth, n_exp, d = peer_u.shape
    eb = 1024
    rows = pl.BlockSpec((1, eb, d), lambda l, j: (l, j, 0))
    return pl.pallas_call(
        _expert_tables_kernel,
        grid=(depth, n_exp // eb),
        in_specs=[rows, rows],
        out_specs=[rows, pl.BlockSpec((1, d, eb), lambda l, j: (l, 0, j))],
        out_shape=[jax.ShapeDtypeStruct((depth, n_exp, d), BF16),
                   jax.ShapeDtypeStruct((depth, d, n_exp), BF16)],
        compiler_params=_params("arbitrary", "arbitrary"),
        name="expert_tables_bf16",
    )(peer_u, peer_v)


def _rope_tables(seq_len):
    t = np.arange(seq_len)
    row = (t // GRID_W).astype(np.float32)
    col = (t % GRID_W).astype(np.float32)
    n_freq = HEAD_DIM // 4
    inv = (ROPE_BASE ** (-np.arange(n_freq, dtype=np.float32) / n_freq)).astype(np.float32)
    ang = jnp.asarray(np.concatenate([row[:, None] * inv, col[:, None] * inv], -1).astype(np.float32))
    cos, sin = jnp.cos(ang), jnp.sin(ang)
    cos_t = jnp.tile(jnp.concatenate([cos, cos], -1), (1, H_GROUP))
    sin_t = jnp.tile(jnp.concatenate([-sin, sin], -1), (1, H_GROUP))
    return cos_t, sin_t


def _attention_bias(rpb, rows):
    cols = np.arange(GRID_W)
    c_start = np.clip(cols - NA_WC // 2, 0, GRID_W - NA_WC)
    col_in = (cols[None, :] >= c_start[:, None]) & (cols[None, :] < c_start[:, None] + NA_WC)
    col_idx = np.clip(cols[None, :] - cols[:, None], 1 - NA_WC, NA_WC - 1) + NA_WC - 1
    pick = jnp.asarray((col_idx[None] == np.arange(2 * NA_WC - 1)[:, None, None]).astype(np.float32))
    rpb_cols = jnp.einsum('hdj,jqk->hdqk', rpb, pick, precision=lax.Precision.HIGHEST)
    masked = jnp.where(jnp.asarray(col_in)[None, None], rpb_cols, NEG_BIG)
    n_rel = 2 * NA_WR - 1
    flat = masked.transpose(0, 2, 1, 3).reshape(H_GROUP * GRID_W, n_rel * GRID_W)
    wide = jnp.pad(flat, ((0, 0), (0, (n_rel + 3) * GRID_W - flat.shape[1])))
    keep = (n_rel + 1) * GRID_W
    return jnp.stack([wide[:, :keep], wide[:, GRID_W:GRID_W + keep]], axis=0)


def _block_diag_states(s0):
    b = s0.shape[0]
    eye = jnp.eye(H_GROUP, dtype=s0.dtype)
    full = jnp.einsum('bchde,hg->bchdge', s0, eye)
    return full.reshape(b, 2, H_GROUP * HEAD_DIM, H_GROUP * HEAD_DIM)


def kernel(x_prompt, x_sample, c, cache_na_k, cache_na_v, state_ret, c_ctx, w_mod, b_mod, w_in, w_out,
           na_rpb, ret_decay, sc_w, cf_w, cf_ln_g, cf_ln_b, ln1_g, ln1_b, ln2_g, ln2_b,
           peer_wq, peer_keys, peer_u, peer_v):
    n_req, seq, d = x_prompt.shape
    n_lat, lat_seq, _ = x_sample.shape
    past = cache_na_k.shape[2]
    rows = lat_seq // GRID_W

    cond8 = jnp.concatenate([c_ctx[None], c, jnp.zeros((8 - 1 - n_lat, d), F32)], axis=0)
    mod = _modulation(cond8, w_mod, b_mod)

    def mod_vec(l, j, lo, hi):
        return mod[l, lo:hi, j * d:(j + 1) * d][:, None, :]

    w_in_b = w_in.astype(BF16)
    w_out_b = w_out.astype(BF16)
    wq_b = peer_wq.astype(BF16)
    keys_b = peer_keys.astype(BF16).reshape(DEPTH, PEER_HEADS * 2, PEER_NKEYS, PEER_NKEYS)
    u_b, v_b = _expert_tables(peer_u, peer_v)

    lane_h = np.arange(W_GROUP) // HEAD_DIM
    avg = jnp.asarray((lane_h[:, None] == lane_h[None, :]).astype(np.float32) / HEAD_DIM).astype(BF16)
    cos_t, sin_t = _rope_tables(lat_seq)
    dec_tiles = jnp.broadcast_to(ret_decay.reshape(DEPTH, 2 * H_GROUP, 1), (DEPTH, 8, LANES))
    scw = jnp.pad(sc_w, ((0, 0), (0, 8 - SC_WIDTH), (0, 0)))
    cfw = jnp.pad(cf_w, ((0, 0), (0, 32 - CF_WIDTH), (0, 0)))
    kc_all = cache_na_k.reshape(n_lat, DEPTH, past, W_GROUP)
    vc_all = cache_na_v.reshape(n_lat, DEPTH, past, W_GROUP)

    xp = x_prompt.reshape(n_req * seq, d)
    xs = x_sample.reshape(n_lat * lat_seq, d)
    ks_out, vs_out, ss_out = [], [], []
    ctx_pending = None

    for l in range(DEPTH):
        row1 = lambda a: a[l][None, :]
        cfg, cfb = row1(cf_ln_g), row1(cf_ln_b)
        l1g, l1b, l2g, l2b = row1(ln1_g), row1(ln1_b), row1(ln2_g), row1(ln2_b)

        n_ctx = n_req * seq
        xp, k_ctx, v_ctx, s_ctx = _ctx_layer(
            xp, mod_vec(l, 0, 0, 1), mod_vec(l, 1, 0, 1), mod_vec(l, 2, 0, 1), w_in_b[l], w_out_b[l], l1g, l1b,
            n_req, seq, dec_tiles[l], avg, scw[l], cfw[l], cfg, cfb, prev=ctx_pending)
        hm, a_t, b_t, tau = _peer_select(xp, mod_vec(l, 3, 0, 1), mod_vec(l, 4, 0, 1), wq_b[l], keys_b[l], n_ctx)
        ffn = _peer_experts(hm, a_t, b_t, tau, u_b, v_b, l)
        ctx_pending = (ffn, mod_vec(l, 5, 0, 1), l2g, l2b)
        ks_out.append(k_ctx.reshape(n_req, seq, H_GROUP, HEAD_DIM))
        vs_out.append(v_ctx.reshape(n_req, seq, H_GROUP, HEAD_DIM))
        ss_out.append(s_ctx)

        hi = 1 + n_lat
        proj = _modmm(xs, mod_vec(l, 0, 1, hi), mod_vec(l, 1, 1, hi), w_in_b[l], lat_seq)
        bias = _attention_bias(na_rpb[l], rows)
        o_a = _lat_attention(proj, n_lat, lat_seq, kc_all[:, l], vc_all[:, l], bias)
        o_b = _lat_retention(proj, n_lat, lat_seq, cos_t, sin_t, _block_diag_states(state_ret[:, l]),
                             dec_tiles[l], avg)
        o_cd = _lat_convs(proj, n_lat, lat_seq, scw[l], cfw[l], cfg, cfb)
        xs = _outproj_ln([o_a, o_b, o_cd], xs, mod_vec(l, 2, 1, hi), w_out_b[l], l1g, l1b, lat_seq)
        hm, a_t, b_t, tau = _peer_select(xs, mod_vec(l, 3, 1, hi), mod_vec(l, 4, 1, hi), wq_b[l], keys_b[l], lat_seq)
        ffn = _peer_experts(hm, a_t, b_t, tau, u_b, v_b, l)
        xs = _residual_ln(xs, ffn, mod_vec(l, 5, 1, hi), l2g, l2b, lat_seq)

    xp = _residual_ln(xp, *ctx_pending, n_req * seq)
    return (xp.reshape(n_req, seq, d), xs.reshape(n_lat, lat_seq, d),
            jnp.stack(ks_out, axis=1), jnp.stack(vs_out, axis=1), jnp.stack(ss_out, axis=1))
```

```python
import functools
import math

import jax
import jax.numpy as jnp
import numpy as np
from jax import lax
from jax.experimental import pallas as pl
from jax.experimental.pallas import tpu as pltpu

F32 = jnp.float32
BF16 = jnp.bfloat16

D_MODEL = 1024
DEPTH = 4
GRID_W = 64
HEAD_DIM = 64
W_GROUP = 256
H_GROUP = 4
N_IN_PARTS = 12
NA_WR = 8
NA_WC = 16
ROPE_BASE = 10000.0
SC_WIDTH = 3
CF_WIDTH = 31
PEER_HEADS = 8
PEER_NKEYS = 128
PEER_TOPK = 16
N_EXPERTS = PEER_NKEYS * PEER_NKEYS
LN_EPS = 1e-5
ALPHA = (2.0 * DEPTH) ** 0.25
NEG_BIG = -1e30

VMEM_LIMIT_BYTES = 56 * 1024 * 1024
LANES = 128

TOKEN_BLOCK = 512
PEER_TOKEN_BLOCK = 1024
PEER_TOKEN_GROUP = 256
PEER_EXPERT_CHUNK = 2048
PEER_EXPERT_SPLIT = 1


def _params(*sem):
    return pltpu.CompilerParams(dimension_semantics=sem, vmem_limit_bytes=VMEM_LIMIT_BYTES)


def _sigmoid(x):
    return 1.0 / (1.0 + jnp.exp(-x))


def _silu(x):
    return x * _sigmoid(x)


def _layernorm_rows(x, g, b):
    mu = jnp.mean(x, axis=-1, keepdims=True)
    d = x - mu
    var = jnp.mean(d * d, axis=-1, keepdims=True)
    return d * lax.rsqrt(var + LN_EPS) * g + b


def _dot(a, b):
    return jnp.dot(a, b, preferred_element_type=F32)


def _dot_nt(a, b):
    return lax.dot_general(a, b, (((1,), (1,)), ((), ())), preferred_element_type=F32)


def _dot_f32(a, b):
    hi = a.astype(BF16)
    lo = (a - hi.astype(F32)).astype(BF16)
    return _dot(hi, b) + _dot(lo, b)


def _head_masks():
    lane = lax.broadcasted_iota(jnp.int32, (1, W_GROUP), 1)
    return [((lane >= h * HEAD_DIM) & (lane < (h + 1) * HEAD_DIM)).astype(F32) for h in range(H_GROUP)]


def _mod_kernel(c_ref, w_ref, b_ref, o_ref):
    c = c_ref[...]
    s = _silu(c)
    o_ref[0] = jnp.dot(s, w_ref[0], precision=lax.Precision.HIGHEST,
                       preferred_element_type=F32) + b_ref[0]


def _modulation(cond8, w_mod, b_mod):
    nc = 1024
    n_out = w_mod.shape[-1]
    return pl.pallas_call(
        _mod_kernel,
        grid=(DEPTH, n_out // nc),
        in_specs=[
            pl.BlockSpec((8, D_MODEL), lambda l, j: (0, 0)),
            pl.BlockSpec((1, D_MODEL, nc), lambda l, j: (l, 0, j)),
            pl.BlockSpec((1, 1, nc), lambda l, j: (l, 0, j)),
        ],
        out_specs=pl.BlockSpec((1, 8, nc), lambda l, j: (l, 0, j)),
        out_shape=jax.ShapeDtypeStruct((DEPTH, 8, n_out), F32),
        compiler_params=_params("arbitrary", "arbitrary"),
        name="modulation",
    )(cond8, w_mod, b_mod.reshape(DEPTH, 1, n_out))


def _modmm_kernel(x_ref, sh_ref, sc_ref, w_ref, o_ref):
    h = x_ref[...] * (1.0 + sc_ref[0]) + sh_ref[0]
    o_ref[...] = _dot(h.astype(BF16), w_ref[...])


def _modmm(x, sh, sc, w, rows_per_cond):
    n, d = x.shape
    n_out = w.shape[1]
    tb = TOKEN_BLOCK
    bpc = rows_per_cond // tb
    vec = pl.BlockSpec((1, 1, d), lambda b: (b // bpc, 0, 0))
    return pl.pallas_call(
        _modmm_kernel,
        grid=(n // tb,),
        in_specs=[
            pl.BlockSpec((tb, d), lambda b: (b, 0)),
            vec, vec,
            pl.BlockSpec((d, n_out), lambda b: (0, 0)),
        ],
        out_specs=pl.BlockSpec((tb, n_out), lambda b: (b, 0)),
        out_shape=jax.ShapeDtypeStruct((n, n_out), F32),
        compiler_params=_params("arbitrary"),
        name="modulated_projection",
    )(x, sh, sc, w)


def _log_gamma_tile(decay_tile):
    y = -decay_tile
    return -(jnp.maximum(y, 0.0) + jnp.log1p(jnp.exp(-jnp.abs(y))))


def _lane_vector(lg_tile, row0, masks):
    out = lg_tile[row0:row0 + 1, 0:1] * masks[0]
    for h in range(1, H_GROUP):
        out = out + lg_tile[row0 + h:row0 + h + 1, 0:1] * masks[h]
    return out


def _retention_pairs(rq, rk, rv, lg_tile, masks, seq_len, q_block):
    kb = rk.astype(BF16)
    vb = rv.astype(BF16)
    blocks = []
    for q0 in range(0, seq_len, q_block):
        rqb = rq[q0:q0 + q_block]
        ti = lax.broadcasted_iota(jnp.int32, (q_block, seq_len), 0) + q0
        si = lax.broadcasted_iota(jnp.int32, (q_block, seq_len), 1)
        dist = (ti - si).astype(F32)
        fwd = jnp.maximum(dist, 0.0)
        bwd = jnp.maximum(-dist, 0.0)
        acc = jnp.zeros((q_block, W_GROUP), F32)
        for h in range(H_GROUP):
            lgf = lg_tile[h:h + 1, 0:1]
            lgb = lg_tile[H_GROUP + h:H_GROUP + h + 1, 0:1]
            decay = (jnp.where(dist >= 0, jnp.exp(lgf * fwd), 0.0)
                     + jnp.where(dist <= 0, jnp.exp(lgb * bwd), 0.0))
            s = _dot_nt((rqb * masks[h]).astype(BF16), kb)
            acc = acc + _dot((s * decay).astype(BF16), vb) * masks[h]
        blocks.append(acc)
    return blocks[0] if len(blocks) == 1 else jnp.concatenate(blocks, axis=0)


def _retention_finish(o, r_g, avg):
    mu = _dot_f32(o, avg)
    d = o - mu
    var = _dot_f32(d * d, avg)
    return _silu(r_g) * (d * lax.rsqrt(var + LN_EPS))


def _shift_rows(z, off, seq_len):
    if off == 0:
        return z
    rolled = pltpu.roll(z, (-off) % seq_len, 0)
    t = lax.broadcasted_iota(jnp.int32, z.shape, 0)
    valid = (t + off >= 0) & (t + off < seq_len)
    return jnp.where(valid, rolled, 0.0)


def _dwconv(z, w_ref, width, seq_len):
    pad = (width - 1) // 2
    acc = None
    for k in range(width):
        term = _shift_rows(z, k - pad, seq_len) * w_ref[k:k + 1, :]
        acc = term if acc is None else acc + term
    return acc


def _conv_mixers(sc_b, sc_c, sc_x, cf_a, cf_gate, scw_ref, cfw_ref, cfg_ref, cfb_ref, seq_len):
    o_c = sc_b * _dwconv(sc_c * sc_x, scw_ref, SC_WIDTH, seq_len)
    u = _dwconv(cf_a * _sigmoid(cf_gate), cfw_ref, CF_WIDTH, seq_len)
    o_d = _silu(_layernorm_rows(u, cfg_ref[...], cfb_ref[...]))
    return o_c, o_d


def _ctx_mix_request(p_ref, dec_ref, avg_ref, scw_ref, cfw_ref, cfg_ref, cfb_ref,
                     mix_ref, k_ref, v_ref, st_ref, seq_len):
    masks = _head_masks()
    part = lambda j: p_ref[:, j * W_GROUP:(j + 1) * W_GROUP]
    na_q, na_k, na_v = part(0), part(1), part(2)
    k_ref[...] = na_k
    v_ref[...] = na_v

    kb = na_k.astype(BF16)
    vb = na_v.astype(BF16)
    qs = na_q * (HEAD_DIM ** -0.5)
    o_a = jnp.zeros((seq_len, W_GROUP), F32)
    for h in range(H_GROUP):
        s = _dot_nt((qs * masks[h]).astype(BF16), kb)
        m = jnp.max(s, axis=-1, keepdims=True)
        e = jnp.exp(s - m)
        p = e / jnp.sum(e, axis=-1, keepdims=True)
        o_a = o_a + _dot(p.astype(BF16), vb) * masks[h]
    mix_ref[:, 0:W_GROUP] = o_a

    lg = _log_gamma_tile(dec_ref[...])
    rq, rv, r_g = part(3), part(5), part(6)
    rk = part(4) * (HEAD_DIM ** -0.5)
    o = _retention_pairs(rq, rk, rv, lg, masks, seq_len, seq_len)
    mix_ref[:, W_GROUP:2 * W_GROUP] = _retention_finish(o, r_g, avg_ref[...])

    pos = lax.broadcasted_iota(jnp.int32, (seq_len, W_GROUP), 0).astype(F32)
    vb_r = rv.astype(BF16)
    for d in range(2):
        lane_lg = _lane_vector(lg, d * H_GROUP, masks)
        expo = (seq_len - 1.0 - pos) if d == 0 else pos
        kd = rk * jnp.exp(lane_lg * expo)
        full = _dot(kd.T.astype(BF16), vb_r)
        for h in range(H_GROUP):
            st_ref[d, h] = full[h * HEAD_DIM:(h + 1) * HEAD_DIM, h * HEAD_DIM:(h + 1) * HEAD_DIM]

    o_c, o_d = _conv_mixers(part(7), part(8), part(9), part(10), part(11),
                            scw_ref, cfw_ref, cfg_ref, cfb_ref, seq_len)
    mix_ref[:, 2 * W_GROUP:3 * W_GROUP] = o_c
    mix_ref[:, 3 * W_GROUP:4 * W_GROUP] = o_d


def _ctx_layer_kernel(*refs, seq_len, n_sub, has_prev):
    if has_prev:
        f_ref, pg_ref, plg_ref, plb_ref = refs[:4]
        refs = refs[4:]
    (x_ref, sh_ref, sc_ref, g_ref, win_ref, wout_ref, lg_ref, lb_ref,
     dec_ref, avg_ref, scw_ref, cfw_ref, cfg_ref, cfb_ref,
     o_ref, k_ref, v_ref, st_ref, proj_ref, mix_ref) = refs
    x = x_ref[...]
    if has_prev:
        x = _layernorm_rows(ALPHA * x + pg_ref[0] * f_ref[...].T, plg_ref[...], plb_ref[...])
    h = x * (1.0 + sc_ref[0]) + sh_ref[0]
    proj_ref[...] = _dot(h.astype(BF16), win_ref[...])
    for r in range(n_sub):
        rows = pl.ds(r * seq_len, seq_len)
        _ctx_mix_request(proj_ref.at[rows], dec_ref, avg_ref, scw_ref, cfw_ref, cfg_ref, cfb_ref,
                         mix_ref.at[rows], k_ref.at[rows], v_ref.at[rows], st_ref.at[r], seq_len)
    mix = _dot(mix_ref[...].astype(BF16), wout_ref[...])
    y = ALPHA * x + g_ref[0] * mix
    o_ref[...] = _layernorm_rows(y, lg_ref[...], lb_ref[...])


def _ctx_layer(x, sh, sc, gate, w_in, w_out, ln_g, ln_b, n_req, seq_len, dec_tile, avg, scw, cfw, cfg, cfb,
               prev=None):
    n, d = x.shape
    n_sub = TOKEN_BLOCK // seq_len
    tb = n_sub * seq_len
    n_in = w_in.shape[1]
    const = lambda shape: pl.BlockSpec(shape, lambda b: (0,) * len(shape))
    rows = lambda width: pl.BlockSpec((tb, width), lambda b: (b, 0))
    prev_specs = [] if prev is None else [pl.BlockSpec((d, tb), lambda b: (0, b)),
                                          const((1, 1, d)), const((1, d)), const((1, d))]
    prev_args = () if prev is None else tuple(prev)
    return pl.pallas_call(
        functools.partial(_ctx_layer_kernel, seq_len=seq_len, n_sub=n_sub, has_prev=prev is not None),
        grid=(n // tb,),
        in_specs=prev_specs + [
            rows(d), const((1, 1, d)), const((1, 1, d)), const((1, 1, d)),
            const((d, n_in)), const((d, d)), const((1, d)), const((1, d)),
            const((8, LANES)), const((W_GROUP, W_GROUP)), const((8, W_GROUP)),
            const((32, W_GROUP)), const((1, W_GROUP)), const((1, W_GROUP)),
        ],
        out_specs=[
            rows(d), rows(W_GROUP), rows(W_GROUP),
            pl.BlockSpec((n_sub, 2, H_GROUP, HEAD_DIM, HEAD_DIM), lambda b: (b, 0, 0, 0, 0)),
        ],
        out_shape=[
            jax.ShapeDtypeStruct((n, d), F32),
            jax.ShapeDtypeStruct((n, W_GROUP), F32),
            jax.ShapeDtypeStruct((n, W_GROUP), F32),
            jax.ShapeDtypeStruct((n_req, 2, H_GROUP, HEAD_DIM, HEAD_DIM), F32),
        ],
        scratch_shapes=[
            pltpu.VMEM((tb, n_in), F32),
            pltpu.VMEM((tb, d), F32),
        ],
        compiler_params=_params("arbitrary"),
        name="context_layer_mixing",
    )(*prev_args, x, sh, sc, gate, w_in, w_out, ln_g, ln_b, dec_tile, avg, scw, cfw, cfg, cfb)


def _lat_attn_kernel(q_ref, k_ref, v_ref, kc_ref, vc_ref, bias_ref, o_ref, *, rows):
    masks = _head_masks()
    kcb = kc_ref[0].astype(BF16)
    vcb = vc_ref[0].astype(BF16)
    wr = min(NA_WR, rows)

    def row_block(r, carry):
        rs = jnp.clip(r - wr // 2, 0, rows - wr)
        q0 = pl.multiple_of(r * GRID_W, GRID_W)
        k0 = pl.multiple_of(rs * GRID_W, GRID_W)
        qb = q_ref[pl.ds(q0, GRID_W), :] * (HEAD_DIM ** -0.5)
        qs = jnp.concatenate([qb * masks[h] for h in range(H_GROUP)], axis=0).astype(BF16)
        kl = k_ref[pl.ds(k0, wr * GRID_W), :].astype(BF16)
        vl = v_ref[pl.ds(k0, wr * GRID_W), :].astype(BF16)
        d0 = rs - r + (NA_WR - 1)
        off = pl.multiple_of((d0 // 2) * (2 * GRID_W), 2 * GRID_W)
        width = wr * GRID_W
        bias = jnp.where(d0 % 2 == 0, bias_ref[0, :, pl.ds(off, width)], bias_ref[1, :, pl.ds(off, width)])
        s_loc = _dot_nt(qs, kl) + bias
        s_ctx = _dot_nt(qs, kcb)
        m = jnp.maximum(jnp.max(s_loc, axis=-1, keepdims=True), jnp.max(s_ctx, axis=-1, keepdims=True))
        e_loc = jnp.exp(s_loc - m)
        e_ctx = jnp.exp(s_ctx - m)
        z = jnp.sum(e_loc, axis=-1, keepdims=True) + jnp.sum(e_ctx, axis=-1, keepdims=True)
        o = (_dot(e_loc.astype(BF16), vl) + _dot(e_ctx.astype(BF16), vcb)) / z
        acc = o[0:GRID_W] * masks[0]
        for h in range(1, H_GROUP):
            acc = acc + o[h * GRID_W:(h + 1) * GRID_W] * masks[h]
        o_ref[pl.ds(q0, GRID_W), :] = acc
        return carry

    lax.fori_loop(0, rows, row_block, 0)


def _lat_attention(proj, n_req, seq_len, kc, vc, bias):
    rows = seq_len // GRID_W
    part = lambda j: pl.BlockSpec((seq_len, W_GROUP), lambda b, j=j: (b, j))
    past = kc.shape[1]
    return pl.pallas_call(
        functools.partial(_lat_attn_kernel, rows=rows),
        grid=(n_req,),
        in_specs=[
            part(0), part(1), part(2),
            pl.BlockSpec((1, past, W_GROUP), lambda b: (b, 0, 0)),
            pl.BlockSpec((1, past, W_GROUP), lambda b: (b, 0, 0)),
            pl.BlockSpec(bias.shape, lambda b: (0, 0, 0)),
        ],
        out_specs=pl.BlockSpec((seq_len, W_GROUP), lambda b: (b, 0)),
        out_shape=jax.ShapeDtypeStruct((n_req * seq_len, W_GROUP), F32),
        compiler_params=_params("arbitrary"),
        name="latent_attention",
    )(proj, proj, proj, kc, vc, bias)


def _swap_head_halves(x):
    half = HEAD_DIM // 2
    lane = lax.broadcasted_iota(jnp.int32, x.shape, 1)
    return jnp.where((lane % HEAD_DIM) < half, pltpu.roll(x, W_GROUP - half, 1), pltpu.roll(x, half, 1))


def _lat_ret_kernel(q_ref, k_ref, v_ref, g_ref, cos_ref, sin_ref, s0_ref,
                    dec_ref, avg_ref, o_ref, *, seq_len):
    masks = _head_masks()
    lg = _log_gamma_tile(dec_ref[...])
    cos = cos_ref[...]
    sin = sin_ref[...]
    q = q_ref[...]
    k = k_ref[...]
    rq = q * cos + _swap_head_halves(q) * sin
    rk = (k * cos + _swap_head_halves(k) * sin) * (HEAD_DIM ** -0.5)
    rv = v_ref[...]
    o = _retention_pairs(rq, rk, rv, lg, masks, seq_len, 256)
    pos = lax.broadcasted_iota(jnp.int32, (seq_len, W_GROUP), 0).astype(F32)
    rqb = rq.astype(BF16)
    lgf = _lane_vector(lg, 0, masks)
    lgb = _lane_vector(lg, H_GROUP, masks)
    o = o + _dot(rqb, s0_ref[0, 0].astype(BF16)) * jnp.exp(lgf * (pos + 1.0))
    o = o + _dot(rqb, s0_ref[0, 1].astype(BF16)) * jnp.exp(lgb * (seq_len - pos))
    o_ref[...] = _retention_finish(o, g_ref[...], avg_ref[...])


def _lat_retention(proj, n_req, seq_len, cos, sin, s0_bd, dec_tile, avg):
    part = lambda j: pl.BlockSpec((seq_len, W_GROUP), lambda b, j=j: (b, j))
    const = lambda shape: pl.BlockSpec(shape, lambda b: (0,) * len(shape))
    return pl.pallas_call(
        functools.partial(_lat_ret_kernel, seq_len=seq_len),
        grid=(n_req,),
        in_specs=[
            part(3), part(4), part(5), part(6),
            const((seq_len, W_GROUP)), const((seq_len, W_GROUP)),
            pl.BlockSpec((1, 2, W_GROUP, W_GROUP), lambda b: (b, 0, 0, 0)),
            const((8, LANES)), const((W_GROUP, W_GROUP)),
        ],
        out_specs=pl.BlockSpec((seq_len, W_GROUP), lambda b: (b, 0)),
        out_shape=jax.ShapeDtypeStruct((n_req * seq_len, W_GROUP), F32),
        compiler_params=_params("arbitrary"),
        name="latent_retention",
    )(proj, proj, proj, proj, cos, sin, s0_bd, dec_tile, avg)


def _lat_conv_kernel(b_ref, c_ref, x_ref, a_ref, gate_ref, scw_ref, cfw_ref, cfg_ref, cfb_ref,
                     o_ref, *, seq_len):
    o_c, o_d = _conv_mixers(b_ref[...], c_ref[...], x_ref[...], a_ref[...], gate_ref[...],
                            scw_ref, cfw_ref, cfg_ref, cfb_ref, seq_len)
    o_ref[:, 0:W_GROUP] = o_c
    o_ref[:, W_GROUP:2 * W_GROUP] = o_d


def _lat_convs(proj, n_req, seq_len, scw, cfw, cfg, cfb):
    part = lambda j: pl.BlockSpec((seq_len, W_GROUP), lambda b, j=j: (b, j))
    const = lambda shape: pl.BlockSpec(shape, lambda b: (0,) * len(shape))
    return pl.pallas_call(
        functools.partial(_lat_conv_kernel, seq_len=seq_len),
        grid=(n_req,),
        in_specs=[
            part(7), part(8), part(9), part(10), part(11),
            const((8, W_GROUP)), const((32, W_GROUP)), const((1, W_GROUP)), const((1, W_GROUP)),
        ],
        out_specs=pl.BlockSpec((seq_len, 2 * W_GROUP), lambda b: (b, 0)),
        out_shape=jax.ShapeDtypeStruct((n_req * seq_len, 2 * W_GROUP), F32),
        compiler_params=_params("arbitrary"),
        name="latent_convs",
    )(proj, proj, proj, proj, proj, scw, cfw, cfg, cfb)


def _outproj_kernel(*refs, widths):
    n_parts = len(widths)
    part_refs = refs[:n_parts]
    x_ref, g_ref, w_ref, lg_ref, lb_ref, o_ref = refs[n_parts:]
    mix = None
    row = 0
    for p_ref, wd in zip(part_refs, widths):
        term = _dot(p_ref[...].astype(BF16), w_ref[row:row + wd, :])
        mix = term if mix is None else mix + term
        row += wd
    y = ALPHA * x_ref[...] + g_ref[0] * mix
    o_ref[...] = _layernorm_rows(y, lg_ref[...], lb_ref[...])


def _outproj_ln(parts, x, gate, w_out, ln_g, ln_b, rows_per_cond):
    n, d = x.shape
    tb = TOKEN_BLOCK
    bpc = rows_per_cond // tb
    widths = tuple(p.shape[1] for p in parts)
    return pl.pallas_call(
        functools.partial(_outproj_kernel, widths=widths),
        grid=(n // tb,),
        in_specs=[pl.BlockSpec((tb, wd), lambda b: (b, 0)) for wd in widths] + [
            pl.BlockSpec((tb, d), lambda b: (b, 0)),
            pl.BlockSpec((1, 1, d), lambda b: (b // bpc, 0, 0)),
            pl.BlockSpec((d, d), lambda b: (0, 0)),
            pl.BlockSpec((1, d), lambda b: (0, 0)),
            pl.BlockSpec((1, d), lambda b: (0, 0)),
        ],
        out_specs=pl.BlockSpec((tb, d), lambda b: (b, 0)),
        out_shape=jax.ShapeDtypeStruct((n, d), F32),
        compiler_params=_params("arbitrary"),
        name="output_projection_ln",
    )(*parts, x, gate, w_out, ln_g, ln_b)


GATE_DTYPE = jnp.bfloat16


def _pair_words(x):
    bits = pltpu.bitcast(x.astype(F32), jnp.uint32)
    return bits | (bits >> 16)


def _packed_rows(word_row, rows):
    return pltpu.bitcast(jnp.broadcast_to(word_row, (rows // 2, word_row.shape[-1])), GATE_DTYPE)


def _bitonic_merge_desc(vals):
    vals = list(vals)
    n = len(vals)
    j = n // 2
    while j >= 1:
        for i in range(n):
            l = i ^ j
            if l > i:
                vals[i], vals[l] = jnp.maximum(vals[i], vals[l]), jnp.minimum(vals[i], vals[l])
        j //= 2
    return vals


def _bitonic_sort_desc(vals):
    vals = list(vals)
    n = len(vals)
    k = 2
    while k <= n:
        j = k // 2
        while j >= 1:
            for i in range(n):
                l = i ^ j
                if l > i:
                    hi, lo = jnp.maximum(vals[i], vals[l]), jnp.minimum(vals[i], vals[l])
                    vals[i], vals[l] = (hi, lo) if (i & k) == 0 else (lo, hi)
            j //= 2
        k *= 2
    return vals


def _extract_top(s, count, roll=None):
    roll = roll or (lambda v, r: pltpu.roll(v, r, 0))
    sub = 8
    assert s.shape[0] == sub * count
    slabs = _bitonic_sort_desc([s[sub * k:sub * (k + 1)] for k in range(count)])
    r = sub // 2
    while r >= 1:
        other = [roll(v, r) for v in slabs]
        slabs = _bitonic_merge_desc([jnp.maximum(slabs[k], other[count - 1 - k]) for k in range(count)])
        r //= 2
    return [v[0:1] for v in slabs]


def _peer_select_kernel(x_ref, sh_ref, sc_ref, wq_ref, keys_ref, hm_ref, a_ref, b_ref, tau_ref):
    hm = (x_ref[...] * (1.0 + sc_ref[0]) + sh_ref[0]).astype(BF16)
    hm_ref[...] = hm.T
    q = _dot(hm, wq_ref[...]).astype(BF16)
    dk = PEER_NKEYS
    as_gate = lambda v: v.astype(GATE_DTYPE).astype(F32)
    tops = [[None, None] for _ in range(PEER_HEADS)]
    for h in range(PEER_HEADS):
        for p in range(2):
            j = h * 2 + p
            s = _dot_nt(keys_ref[j], q[:, j * dk:(j + 1) * dk])
            top = _extract_top(s, PEER_TOPK)
            e = jnp.where(s >= top[-1], jnp.exp(s - top[0]), 0.0).astype(GATE_DTYPE)
            tops[h][p] = [as_gate(jnp.exp(t - top[0])) for t in top]
            if p == 0:
                a_ref[h] = _pair_words(e)
            else:
                b_ref[h] = e

    stack = lambda rows_per_head: jnp.concatenate(rows_per_head, axis=0)
    a_rank = [stack([tops[h][0][r] for h in range(PEER_HEADS)]) for r in range(PEER_TOPK)]
    b_rank = [stack([tops[h][1][r] for h in range(PEER_HEADS)]) for r in range(PEER_TOPK)]
    likely = [a_rank[r] * b_rank[c] for r in range(PEER_TOPK) for c in range(PEER_TOPK)
              if (r + 1) * (c + 1) <= PEER_TOPK]
    padded = likely + [jnp.full_like(likely[0], -1.0)] * (64 - len(likely))
    kth = _bitonic_sort_desc(padded)[PEER_TOPK - 1]
    a_all = jnp.stack(a_rank)[:, None]
    b_all = jnp.stack(b_rank)[None]
    cand = a_all * b_all
    chosen = cand >= kth
    z = jnp.sum(jnp.where(chosen, cand, 0.0), axis=(0, 1))
    rz = 1.0 / z
    scaled = (a_all.astype(GATE_DTYPE) * (b_all * rz).astype(GATE_DTYPE)).astype(F32)
    tau = jnp.min(jnp.where(chosen, scaled, jnp.inf), axis=(0, 1))
    tau_ref[...] = _pair_words(tau.astype(GATE_DTYPE))
    for h in range(PEER_HEADS):
        b_ref[h] = (b_ref[h].astype(F32) * rz[h:h + 1, :]).astype(GATE_DTYPE)


def _peer_select(x, sh, sc, wq, keys, rows_per_cond):
    n, d = x.shape
    tb = TOKEN_BLOCK
    bpc = rows_per_cond // tb
    vec = pl.BlockSpec((1, 1, d), lambda b: (b // bpc, 0, 0))
    grid_hn = lambda: pl.BlockSpec((PEER_HEADS, PEER_NKEYS, tb), lambda b: (0, 0, b))
    return pl.pallas_call(
        _peer_select_kernel,
        grid=(n // tb,),
        in_specs=[
            pl.BlockSpec((tb, d), lambda b: (b, 0)),
            vec, vec,
            pl.BlockSpec(wq.shape, lambda b: (0, 0)),
            pl.BlockSpec(keys.shape, lambda b: (0, 0, 0)),
        ],
        out_specs=[
            pl.BlockSpec((d, tb), lambda b: (0, b)),
            grid_hn(), grid_hn(),
            pl.BlockSpec((PEER_HEADS, tb), lambda b: (0, b)),
        ],
        out_shape=[
            jax.ShapeDtypeStruct((d, n), BF16),
            jax.ShapeDtypeStruct((PEER_HEADS, PEER_NKEYS, n), jnp.uint32),
            jax.ShapeDtypeStruct((PEER_HEADS, PEER_NKEYS, n), GATE_DTYPE),
            jax.ShapeDtypeStruct((PEER_HEADS, n), jnp.uint32),
        ],
        compiler_params=_params("arbitrary"),
        name="peer_select",
    )(x, sh, sc, wq, keys)


def _gelu(x):
    return 0.5 * x * (1.0 + lax.erf(x * (2.0 ** -0.5)))


def _peer_expert_kernel(hm_ref, a_ref, b_ref, tau_ref, u_ref, v_ref, o_ref, ht_ref, gh_ref, *, tok_block):
    c = pl.program_id(1)

    @pl.when(c == 0)
    def _():
        o_ref[...] = jnp.zeros_like(o_ref)

    rows_per_chunk = PEER_EXPERT_CHUNK // PEER_NKEYS
    group = PEER_TOKEN_GROUP
    n_groups = tok_block // group
    v_t = v_ref[0]

    halves = PEER_EXPERT_SPLIT
    half_rows = rows_per_chunk // halves
    half = PEER_EXPERT_CHUNK // halves

    def pre_activations(g, eh):
        tok = pl.ds(g * group, group)
        ex = pl.ds(eh * half, half)
        ht_ref[ex, tok] = _dot(u_ref[0, ex, :], hm_ref[:, tok]).astype(GATE_DTYPE)

    def gated_activations(g, eh):
        for sub in range(group // LANES):
            tok = pl.ds(g * group + sub * LANES, LANES)
            for i in range(eh * half_rows, (eh + 1) * half_rows):
                rows = pl.ds(i * PEER_NKEYS, PEER_NKEYS)
                gate = jnp.zeros((PEER_NKEYS, LANES), GATE_DTYPE)
                for h in range(PEER_HEADS):
                    e = _packed_rows(a_ref[h, i:i + 1, tok], PEER_NKEYS) * b_ref[h, :, tok]
                    tau = _packed_rows(tau_ref[h:h + 1, tok], PEER_NKEYS)
                    gate = jnp.where(e >= tau, gate + e, gate)
                act = _gelu(ht_ref[rows, tok])
                gh_ref[rows, tok] = (gate * act).astype(BF16)

    def accumulate(g):
        tok = pl.ds(g * group, group)
        o_ref[:, tok] += _dot(v_t, gh_ref[:, tok])

    units = [(g, eh) for g in range(n_groups) for eh in range(halves)]
    for k in range(len(units) + 1):
        if k < len(units):
            pre_activations(*units[k])
        if k >= 1:
            g, eh = units[k - 1]
            gated_activations(g, eh)
            if eh == halves - 1:
                accumulate(g)


def _peer_experts(hm, a_t, b_t, tau, u_all, v_all, layer):
    d, n = hm.shape
    tb = PEER_TOKEN_BLOCK
    ec = PEER_EXPERT_CHUNK
    n_chunks = N_EXPERTS // ec
    rpc = ec // PEER_NKEYS
    return pl.pallas_call(
        functools.partial(_peer_expert_kernel, tok_block=tb),
        grid=(n // tb, n_chunks),
        in_specs=[
            pl.BlockSpec((d, tb), lambda b, c: (0, b)),
            pl.BlockSpec((PEER_HEADS, rpc, tb), lambda b, c: (0, c, b)),
            pl.BlockSpec((PEER_HEADS, PEER_NKEYS, tb), lambda b, c: (0, 0, b)),
            pl.BlockSpec((PEER_HEADS, tb), lambda b, c: (0, b)),
            pl.BlockSpec((1, ec, d), lambda b, c: (layer, c, 0)),
            pl.BlockSpec((1, d, ec), lambda b, c: (layer, 0, c)),
        ],
        out_specs=pl.BlockSpec((d, tb), lambda b, c: (0, b)),
        out_shape=jax.ShapeDtypeStruct((d, n), F32),
        scratch_shapes=[
            pltpu.VMEM((ec, tb), GATE_DTYPE),
            pltpu.VMEM((ec, tb), BF16),
        ],
        compiler_params=_params("arbitrary", "arbitrary"),
        name="peer_experts",
    )(hm, a_t, b_t, tau, u_all, v_all)


def _residual_ln_kernel(x_ref, f_ref, g_ref, lg_ref, lb_ref, o_ref):
    y = ALPHA * x_ref[...] + g_ref[0] * f_ref[...].T
    o_ref[...] = _layernorm_rows(y, lg_ref[...], lb_ref[...])


def _residual_ln(x, f, gate, ln_g, ln_b, rows_per_cond):
    n, d = x.shape
    tb = TOKEN_BLOCK
    bpc = rows_per_cond // tb
    row = pl.BlockSpec((tb, d), lambda b: (b, 0))
    return pl.pallas_call(
        _residual_ln_kernel,
        grid=(n // tb,),
        in_specs=[row, pl.BlockSpec((d, tb), lambda b: (0, b)),
                  pl.BlockSpec((1, 1, d), lambda b: (b // bpc, 0, 0)),
                  pl.BlockSpec((1, d), lambda b: (0, 0)),
                  pl.BlockSpec((1, d), lambda b: (0, 0))],
        out_specs=row,
        out_shape=jax.ShapeDtypeStruct((n, d), F32),
        compiler_params=_params("arbitrary"),
        name="residual_ln",
    )(x, f, gate, ln_g, ln_b)


def _expert_tables_kernel(u_ref, v_ref, ub_ref, vt_ref):
    ub_ref[0] = u_ref[0].astype(BF16)
    vt_ref[0] = v_ref[0].astype(BF16).T


def _expert_tables(peer_u, peer_v):
    depth, n_exp, d = peer_u.shape
    eb = 1024
    rows = pl.BlockSpec((1, eb, d), lambda l, j: (l, j, 0))
    return pl.pallas_call(
        _expert_tables_kernel,
        grid=(depth, n_exp // eb),
        in_specs=[rows, rows],
        out_specs=[rows, pl.BlockSpec((1, d, eb), lambda l, j: (l, 0, j))],
        out_shape=[jax.ShapeDtypeStruct((depth, n_exp, d), BF16),
                   jax.ShapeDtypeStruct((depth, d, n_exp), BF16)],
        compiler_params=_params("arbitrary", "arbitrary"),
        name="expert_tables_bf16",
    )(peer_u, peer_v)


def _rope_tables(seq_len):
    t = np.arange(seq_len)
    row = (t // GRID_W).astype(np.float32)
    col = (t % GRID_W).astype(np.float32)
    n_freq = HEAD_DIM // 4
    inv = (ROPE_BASE ** (-np.arange(n_freq, dtype=np.float32) / n_freq)).astype(np.float32)
    ang = jnp.asarray(np.concatenate([row[:, None] * inv, col[:, None] * inv], -1).astype(np.float32))
    cos, sin = jnp.cos(ang), jnp.sin(ang)
    cos_t = jnp.tile(jnp.concatenate([cos, cos], -1), (1, H_GROUP))
    sin_t = jnp.tile(jnp.concatenate([-sin, sin], -1), (1, H_GROUP))
    return cos_t, sin_t


def _attention_bias(rpb, rows):
    cols = np.arange(GRID_W)
    c_start = np.clip(cols - NA_WC // 2, 0, GRID_W - NA_WC)
    col_in = (cols[None, :] >= c_start[:, None]) & (cols[None, :] < c_start[:, None] + NA_WC)
    col_idx = np.clip(cols[None, :] - cols[:, None], 1 - NA_WC, NA_WC - 1) + NA_WC - 1
    pick = jnp.asarray((col_idx[None] == np.arange(2 * NA_WC - 1)[:, None, None]).astype(np.float32))
    rpb_cols = jnp.einsum('hdj,jqk->hdqk', rpb, pick, precision=lax.Precision.HIGHEST)
    masked = jnp.where(jnp.asarray(col_in)[None, None], rpb_cols, NEG_BIG)
    n_rel = 2 * NA_WR - 1
    flat = masked.transpose(0, 2, 1, 3).reshape(H_GROUP * GRID_W, n_rel * GRID_W)
    wide = jnp.pad(flat, ((0, 0), (0, (n_rel + 3) * GRID_W - flat.shape[1])))
    keep = (n_rel + 1) * GRID_W
    return jnp.stack([wide[:, :keep], wide[:, GRID_W:GRID_W + keep]], axis=0)


def _block_diag_states(s0):
    b = s0.shape[0]
    eye = jnp.eye(H_GROUP, dtype=s0.dtype)
    full = jnp.einsum('bchde,hg->bchdge', s0, eye)
    return full.reshape(b, 2, H_GROUP * HEAD_DIM, H_GROUP * HEAD_DIM)


def kernel(x_prompt, x_sample, c, cache_na_k, cache_na_v, state_ret, c_ctx, w_mod, b_mod, w_in, w_out,
           na_rpb, ret_decay, sc_w, cf_w, cf_ln_g, cf_ln_b, ln1_g, ln1_b, ln2_g, ln2_b,
           peer_wq, peer_keys, peer_u, peer_v):
    n_req, seq, d = x_prompt.shape
    n_lat, lat_seq, _ = x_sample.shape
    past = cache_na_k.shape[2]
    rows = lat_seq // GRID_W

    cond8 = jnp.concatenate([c_ctx[None], c, jnp.zeros((8 - 1 - n_lat, d), F32)], axis=0)
    mod = _modulation(cond8, w_mod, b_mod)

    def mod_vec(l, j, lo, hi):
        return mod[l, lo:hi, j * d:(j + 1) * d][:, None, :]

    w_in_b = w_in.astype(BF16)
    w_out_b = w_out.astype(BF16)
    wq_b = peer_wq.astype(BF16)
    keys_b = peer_keys.astype(BF16).reshape(DEPTH, PEER_HEADS * 2, PEER_NKEYS, PEER_NKEYS)
    u_b, v_b = _expert_tables(peer_u, peer_v)

    lane_h = np.arange(W_GROUP) // HEAD_DIM
    avg = jnp.asarray((lane_h[:, None] == lane_h[None, :]).astype(np.float32) / HEAD_DIM).astype(BF16)
    cos_t, sin_t = _rope_tables(lat_seq)
    dec_tiles = jnp.broadcast_to(ret_decay.reshape(DEPTH, 2 * H_GROUP, 1), (DEPTH, 8, LANES))
    scw = jnp.pad(sc_w, ((0, 0), (0, 8 - SC_WIDTH), (0, 0)))
    cfw = jnp.pad(cf_w, ((0, 0), (0, 32 - CF_WIDTH), (0, 0)))
    kc_all = cache_na_k.reshape(n_lat, DEPTH, past, W_GROUP)
    vc_all = cache_na_v.reshape(n_lat, DEPTH, past, W_GROUP)

    xp = x_prompt.reshape(n_req * seq, d)
    xs = x_sample.reshape(n_lat * lat_seq, d)
    ks_out, vs_out, ss_out = [], [], []
    ctx_pending = None

    for l in range(DEPTH):
        row1 = lambda a: a[l][None, :]
        cfg, cfb = row1(cf_ln_g), row1(cf_ln_b)
        l1g, l1b, l2g, l2b = row1(ln1_g), row1(ln1_b), row1(ln2_g), row1(ln2_b)

        n_ctx = n_req * seq
        xp, k_ctx, v_ctx, s_ctx = _ctx_layer(
            xp, mod_vec(l, 0, 0, 1), mod_vec(l, 1, 0, 1), mod_vec(l, 2, 0, 1), w_in_b[l], w_out_b[l], l1g, l1b,
            n_req, seq, dec_tiles[l], avg, scw[l], cfw[l], cfg, cfb, prev=ctx_pending)
        hm, a_t, b_t, tau = _peer_select(xp, mod_vec(l, 3, 0, 1), mod_vec(l, 4, 0, 1), wq_b[l], keys_b[l], n_ctx)
        ffn = _peer_experts(hm, a_t, b_t, tau, u_b, v_b, l)
        ctx_pending = (ffn, mod_vec(l, 5, 0, 1), l2g, l2b)
        ks_out.append(k_ctx.reshape(n_req, seq, H_GROUP, HEAD_DIM))
        vs_out.append(v_ctx.reshape(n_req, seq, H_GROUP, HEAD_DIM))
        ss_out.append(s_ctx)

        hi = 1 + n_lat
        proj = _modmm(xs, mod_vec(l, 0, 1, hi), mod_vec(l, 1, 1, hi), w_in_b[l], lat_seq)
        bias = _attention_bias(na_rpb[l], rows)
        o_a = _lat_attention(proj, n_lat, lat_seq, kc_all[:, l], vc_all[:, l], bias)
        o_b = _lat_retention(proj, n_lat, lat_seq, cos_t, sin_t, _block_diag_states(state_ret[:, l]),
                             dec_tiles[l], avg)
        o_cd = _lat_convs(proj, n_lat, lat_seq, scw[l], cfw[l], cfg, cfb)
        xs = _outproj_ln([o_a, o_b, o_cd], xs, mod_vec(l, 2, 1, hi), w_out_b[l], l1g, l1b, lat_seq)
        hm, a_t, b_t, tau = _peer_select(xs, mod_vec(l, 3, 1, hi), mod_vec(l, 4, 1, hi), wq_b[l], keys_b[l], lat_seq)
        ffn = _peer_experts(hm, a_t, b_t, tau, u_b, v_b, l)
        xs = _residual_ln(xs, ffn, mod_vec(l, 5, 1, hi), l2g, l2b, lat_seq)

    xp = _residual_ln(xp, *ctx_pending, n_req * seq)
    return (xp.reshape(n_req, seq, d), xs.reshape(n_lat, lat_seq, d),
            jnp.stack(ks_out, axis=1), jnp.stack(vs_out, axis=1), jnp.stack(ss_out, axis=1))
```
